```python
import math
import numpy as np
import jax, jax.numpy as jnp
from jax import lax

D_MODEL = 1024
BATCH = 8
SEQ = 4096
DEPTH = 1

HEAD_DIM = 64
ROT_DIM = HEAD_DIM // 4
ROPE_THETA = 500000.0
NORM_EPS = 1e-6
NEG_INF = -1e30
BIG = 1e30

A_HEADS = 8
A_BLOCK = 256
A_TOPK = 3
A_QBLOCK = 64

B_HEADS = 8
B_KV_HEADS = 2
B_GROUP = B_HEADS // B_KV_HEADS
CMP_LEN = 32
CMP_STRIDE = 16
CMP_HIDDEN = 256
SLC_BLOCK = 64
SLC_TOPN = 16
WINDOW = 512
B_QBLOCK = 64

PEER_HEADS = 8
PEER_NKEYS = 128
PEER_N = PEER_NKEYS * PEER_NKEYS
PEER_QDIM = 256
PEER_HALF = PEER_QDIM // 2
PEER_TOPK = 16
PEER_CHUNK = 128

A_WIDTH = A_HEADS * HEAD_DIM
B_WIDTH = B_HEADS * HEAD_DIM
B_KV_WIDTH = B_KV_HEADS * HEAD_DIM
N_BRANCH = 2
IN_COLS = 3 * A_WIDTH + B_WIDTH + 6 * B_KV_WIDTH + 3 * B_HEADS + N_BRANCH * D_MODEL

kernel_name = "hybrid_moba_nsa_peer_block"


def rmsnorm(x, g):
    xf = x.astype(jnp.float32)
    y = xf * lax.rsqrt(jnp.mean(xf * xf, axis=-1, keepdims=True) + NORM_EPS)
    return (y * g.astype(jnp.float32)).astype(x.dtype)


def rope_tables(S):
    inv = ROPE_THETA ** (-jnp.arange(0, ROT_DIM, 2, dtype=jnp.float32) / ROT_DIM)
    ang = jnp.arange(S, dtype=jnp.float32)[:, None] * inv[None, :]
    return jnp.cos(ang), jnp.sin(ang)


def partial_rope(x, cos, sin):
    c = cos[None, :, None, :].astype(x.dtype)
    s = sin[None, :, None, :].astype(x.dtype)
    x1 = x[..., :ROT_DIM // 2]
    x2 = x[..., ROT_DIM // 2:ROT_DIM]
    return jnp.concatenate([x1 * c - x2 * s, x2 * c + x1 * s, x[..., ROT_DIM:]], axis=-1)


def masked_softmax(s, mask):
    p = jax.nn.softmax(jnp.where(mask, s.astype(jnp.float32), NEG_INF), axis=-1)
    return p * jnp.any(mask, axis=-1, keepdims=True)


def split_columns(t):
    sizes = (A_WIDTH, A_WIDTH, A_WIDTH, B_WIDTH, B_KV_WIDTH, B_KV_WIDTH, B_KV_WIDTH,
             B_KV_WIDTH, B_KV_WIDTH, B_KV_WIDTH, 3 * B_HEADS, N_BRANCH * D_MODEL)
    points, acc = [], 0
    for s in sizes[:-1]:
        acc += s
        points.append(acc)
    return jnp.split(t, points, axis=-1)


def moba_attention(q, k, v):
    B, S, H, D = q.shape
    n_blk = -(-S // A_BLOCK)
    pad = n_blk * A_BLOCK - S
    topk = min(A_TOPK, n_blk)
    nc = S // A_QBLOCK
    Q = A_QBLOCK
    L = A_BLOCK
    scale = HEAD_DIM ** -0.5

    def to_blocks(t):
        t = jnp.pad(t, ((0, 0), (0, pad), (0, 0), (0, 0)))
        return t.reshape(B, n_blk, L, H, D).transpose(0, 3, 1, 2, 4)

    kb, vb = to_blocks(k), to_blocks(v)
    qh = q.transpose(0, 2, 1, 3)
    k_mean = jnp.mean(kb.astype(jnp.float32), axis=3)
    gate = jnp.einsum('bhsd,bhnd->bhsn', qh.astype(jnp.float32), k_mean)
    own = jnp.arange(S) // A_BLOCK
    past = jnp.arange(n_blk)[None, :] < own[:, None]
    gate = jnp.where(past, gate, NEG_INF)
    top_v, sel = lax.top_k(gate, topk)
    valid = top_v > 0.5 * NEG_INF

    def per_batch(args):
        q_b, kb_b, vb_b, sel_b, valid_b = args

        def per_chunk(c):
            start = c * Q
            pos = start + jnp.arange(Q)
            qc = lax.dynamic_slice_in_dim(q_b, start, Q, axis=1)
            sc = lax.dynamic_slice_in_dim(sel_b, start, Q, axis=1)
            vc = lax.dynamic_slice_in_dim(valid_b, start, Q, axis=1)
            kg = jax.vmap(lambda kbh, ih: kbh[ih])(kb_b, sc)
            vg = jax.vmap(lambda vbh, ih: vbh[ih])(vb_b, sc)
            blk = start // A_BLOCK
            k_own = lax.dynamic_index_in_dim(kb_b, blk, axis=1, keepdims=False)
            v_own = lax.dynamic_index_in_dim(vb_b, blk, axis=1, keepdims=False)
            s_sel = jnp.einsum('hqd,hqkld->hqkl', qc, kg).reshape(H, Q, topk * L)
            s_own = jnp.einsum('hqd,hld->hql', qc, k_own)
            m_sel = jnp.broadcast_to(vc[..., None], (H, Q, topk, L)).reshape(H, Q, topk * L)
            kpos = blk * A_BLOCK + jnp.arange(L)
            m_own = jnp.broadcast_to(kpos[None, None, :] <= pos[None, :, None], (H, Q, L))
            s = jnp.concatenate([s_sel, s_own], axis=-1).astype(jnp.float32) * scale
            p = masked_softmax(s, jnp.concatenate([m_sel, m_own], axis=-1)).astype(v.dtype)
            o = jnp.einsum('hqkl,hqkld->hqd', p[..., :topk * L].reshape(H, Q, topk, L), vg)
            return o + jnp.einsum('hql,hld->hqd', p[..., topk * L:], v_own)

        outs = lax.map(per_chunk, jnp.arange(nc))
        return outs.transpose(0, 2, 1, 3).reshape(S, H, D)

    return lax.map(per_batch, (qh, kb, vb, sel, valid))


def compress(t, pos_emb, w1, b1, w2, b2):
    B, S, G, D = t.shape
    n_cmp = (S - CMP_LEN) // CMP_STRIDE + 1
    idx = np.arange(n_cmp)[:, None] * CMP_STRIDE + np.arange(CMP_LEN)[None, :]
    blocks = t[:, idx] + pos_emb[None, None, :, None, :]
    flat = blocks.transpose(0, 3, 1, 2, 4).reshape(B, G, n_cmp, CMP_LEN * D)
    return jax.nn.gelu(flat @ w1 + b1) @ w2 + b2


def nsa_attention(q, kc, vc, ks, vs, kw, vw, gates):
    B, S, H, D = q.shape
    G, R, Q = B_KV_HEADS, B_GROUP, B_QBLOCK
    n_cmp = kc.shape[2]
    n_slc = S // SLC_BLOCK
    topn = min(SLC_TOPN, n_slc)
    nc = S // Q
    scale = HEAD_DIM ** -0.5
    cmp_end = jnp.arange(n_cmp) * CMP_STRIDE + (CMP_LEN - 1)
    ci = np.arange(n_cmp)[:, None]
    sj = np.arange(n_slc)[None, :]
    overlap = jnp.asarray(((ci * CMP_STRIDE < (sj + 1) * SLC_BLOCK) &
                           (ci * CMP_STRIDE + CMP_LEN > sj * SLC_BLOCK)).astype(np.float32))

    qg = q.reshape(B, S, G, R, D).transpose(0, 2, 3, 1, 4)
    ksb = ks.transpose(0, 2, 1, 3).reshape(B, G, n_slc, SLC_BLOCK, D)
    vsb = vs.transpose(0, 2, 1, 3).reshape(B, G, n_slc, SLC_BLOCK, D)
    kwp = jnp.pad(kw.transpose(0, 2, 1, 3), ((0, 0), (0, 0), (WINDOW, 0), (0, 0)))
    vwp = jnp.pad(vw.transpose(0, 2, 1, 3), ((0, 0), (0, 0), (WINDOW, 0), (0, 0)))
    gg = gates.reshape(B, S, G, R, 3).transpose(0, 2, 3, 1, 4)

    def per_batch(args):
        q_b, kc_b, vc_b, ks_b, vs_b, kw_b, vw_b, g_b = args

        def per_chunk(c):
            start = c * Q
            pos = start + jnp.arange(Q)
            qc = lax.dynamic_slice_in_dim(q_b, start, Q, axis=2)
            gc = lax.dynamic_slice_in_dim(g_b, start, Q, axis=2)
            s = jnp.einsum('grqd,gnd->grqn', qc, kc_b).astype(jnp.float32) * scale
            p_cmp = masked_softmax(s, cmp_end[None, :] <= pos[:, None])
            o_cmp = jnp.einsum('grqn,gnd->grqd', p_cmp.astype(vc_b.dtype), vc_b)
            imp = jnp.einsum('grqn,nj->gqj', p_cmp, overlap)
            own = pos // SLC_BLOCK
            blk = jnp.arange(n_slc)[None, :]
            forced = (blk == 0) | (blk == own[:, None]) | (blk == own[:, None] - 1)
            imp = jnp.where(forced, BIG, imp)
            imp = jnp.where(blk <= own[:, None], imp, NEG_INF)
            top_v, sel = lax.top_k(imp, topn)
            valid = top_v > 0.5 * NEG_INF
            kg = jax.vmap(lambda kbg, ig: kbg[ig])(ks_b, sel)
            vg = jax.vmap(lambda vbg, ig: vbg[ig])(vs_b, sel)
            s = jnp.einsum('grqd,gqnld->grqnl', qc, kg).astype(jnp.float32) * scale
            kpos = sel[..., None] * SLC_BLOCK + jnp.arange(SLC_BLOCK)
            m = valid[..., None] & (kpos <= pos[None, :, None, None])
            p = masked_softmax(s.reshape(G, R, Q, topn * SLC_BLOCK),
                               m.reshape(G, 1, Q, topn * SLC_BLOCK))
            o_slc = jnp.einsum('grqnl,gqnld->grqd',
                               p.reshape(G, R, Q, topn, SLC_BLOCK).astype(vs_b.dtype), vg)
            kwc = lax.dynamic_slice_in_dim(kw_b, start, WINDOW + Q, axis=1)
            vwc = lax.dynamic_slice_in_dim(vw_b, start, WINDOW + Q, axis=1)
            kpos_w = start - WINDOW + jnp.arange(WINDOW + Q)
            m_w = ((kpos_w[None, :] <= pos[:, None]) & (kpos_w[None, :] > pos[:, None] - WINDOW)
                   & (kpos_w[None, :] >= 0))
            s = jnp.einsum('grqd,gkd->grqk', qc, kwc).astype(jnp.float32) * scale
            p = masked_softmax(s, m_w)
            o_win = jnp.einsum('grqk,gkd->grqd', p.astype(vwc.dtype), vwc)
            return gc[..., 0:1] * o_cmp + gc[..., 1:2] * o_slc + gc[..., 2:3] * o_win

        outs = lax.map(per_chunk, jnp.arange(nc))
        return outs.transpose(0, 3, 1, 2, 4).reshape(S, H, D)

    return lax.map(per_batch, (qg, kc, vc, ksb, vsb, kwp, vwp, gg))


def peer_ffn(h, w_q, sub_k1, sub_k2, expert_u, expert_v):
    B, S, Dm = h.shape
    T = B * S
    ht = h.reshape(T, Dm)
    q = (ht @ w_q).reshape(T, PEER_HEADS, PEER_QDIM)
    s1 = jnp.einsum('thd,nd->thn', q[..., :PEER_HALF], sub_k1).astype(jnp.float32)
    s2 = jnp.einsum('thd,nd->thn', q[..., PEER_HALF:], sub_k2).astype(jnp.float32)
    v1, i1 = lax.top_k(s1, PEER_TOPK)
    v2, i2 = lax.top_k(s2, PEER_TOPK)
    cand = (v1[..., :, None] + v2[..., None, :]).reshape(T, PEER_HEADS, PEER_TOPK * PEER_TOPK)
    cand_idx = (i1[..., :, None] * PEER_NKEYS + i2[..., None, :]).reshape(T, PEER_HEADS, PEER_TOPK * PEER_TOPK)
    vals, pos = lax.top_k(cand, PEER_TOPK)
    experts = jnp.take_along_axis(cand_idx, pos, axis=-1)
    gates = jax.nn.softmax(vals, axis=-1)

    def per_chunk(args):
        xc, ec, gc = args
        u = expert_u[ec]
        a = jnp.einsum('cd,chkd->chk', xc, u).astype(jnp.float32)
        w = (jax.nn.gelu(a) * gc).astype(xc.dtype)
        return jnp.einsum('chk,chkd->cd', w, expert_v[ec])

    nc = T // PEER_CHUNK
    out = lax.map(per_chunk, (ht.reshape(nc, PEER_CHUNK, Dm),
                              experts.reshape(nc, PEER_CHUNK, PEER_HEADS, PEER_TOPK),
                              gates.reshape(nc, PEER_CHUNK, PEER_HEADS, PEER_TOPK)))
    return out.reshape(B, S, Dm)


def hybrid_layer(x, cos, sin, norm1_g, w_in, b_merge, a_q_g, a_k_g, b_q_g, b_kc_g, b_ks_g, b_kw_g,
                 cmp_pos_k, cmp_k_w1, cmp_k_b1, cmp_k_w2, cmp_k_b2,
                 cmp_pos_v, cmp_v_w1, cmp_v_b1, cmp_v_w2, cmp_v_b2,
                 w_up_a, w_up_b, w_out, norm2_g, peer_wq, peer_k1, peer_k2, peer_u, peer_v):
    B, S, _ = x.shape
    h = rmsnorm(x, norm1_g)
    (aq, ak, av, bq, bkc, bvc, bks, bvs, bkw, bvw, bgate, mgate) = split_columns(h @ w_in)

    def heads(t, n):
        return t.reshape(B, S, n, HEAD_DIM)

    qa = partial_rope(rmsnorm(heads(aq, A_HEADS), a_q_g), cos, sin)
    ka = partial_rope(rmsnorm(heads(ak, A_HEADS), a_k_g), cos, sin)
    ya = moba_attention(qa, ka, heads(av, A_HEADS)).reshape(B, S, A_WIDTH)

    qb = partial_rope(rmsnorm(heads(bq, B_HEADS), b_q_g), cos, sin)
    kc = rmsnorm(compress(heads(bkc, B_KV_HEADS), cmp_pos_k, cmp_k_w1, cmp_k_b1, cmp_k_w2, cmp_k_b2), b_kc_g)
    vc = compress(heads(bvc, B_KV_HEADS), cmp_pos_v, cmp_v_w1, cmp_v_b1, cmp_v_w2, cmp_v_b2)
    ks = partial_rope(rmsnorm(heads(bks, B_KV_HEADS), b_ks_g), cos, sin)
    kw = partial_rope(rmsnorm(heads(bkw, B_KV_HEADS), b_kw_g), cos, sin)
    gb = jax.nn.sigmoid(bgate.astype(jnp.float32)).astype(x.dtype).reshape(B, S, B_HEADS, 3)
    yb = nsa_attention(qb, kc, vc, ks, heads(bvs, B_KV_HEADS), kw, heads(bvw, B_KV_HEADS), gb)
    yb = yb.reshape(B, S, B_WIDTH)

    g = jax.nn.sigmoid((mgate.reshape(B, S, N_BRANCH, D_MODEL) + b_merge).astype(jnp.float32)).astype(x.dtype)
    merged = g[:, :, 0] * (ya @ w_up_a) + g[:, :, 1] * (yb @ w_up_b)
    x = x + merged @ w_out

    return x + peer_ffn(rmsnorm(x, norm2_g), peer_wq, peer_k1, peer_k2, peer_u, peer_v)


def setup_inputs(seed: int = 0) -> dict:
    key = jax.random.key(seed)
    ks = jax.random.split(key, 32)
    L = DEPTH
    f32 = jnp.float32

    def nrm(k, shape, scale):
        return jax.random.normal(k, shape, f32) * scale

    def gain(k, shape):
        return 1.0 + 0.05 * jax.random.normal(k, shape, f32)

    return {
        "x": nrm(ks[0], (BATCH, SEQ, D_MODEL), 1.0),
        "norm1_g": gain(ks[1], (L, D_MODEL)),
        "w_in": nrm(ks[2], (L, D_MODEL, IN_COLS), D_MODEL ** -0.5),
        "b_merge": nrm(ks[3], (L, N_BRANCH, D_MODEL), 0.1),
        "a_q_g": gain(ks[4], (L, HEAD_DIM)),
        "a_k_g": gain(ks[5], (L, HEAD_DIM)),
        "b_q_g": gain(ks[6], (L, HEAD_DIM)),
        "b_kc_g": gain(ks[7], (L, HEAD_DIM)),
        "b_ks_g": gain(ks[8], (L, HEAD_DIM)),
        "b_kw_g": gain(ks[9], (L, HEAD_DIM)),
        "cmp_pos_k": nrm(ks[10], (L, CMP_LEN, HEAD_DIM), 0.1),
        "cmp_k_w1": nrm(ks[11], (L, CMP_LEN * HEAD_DIM, CMP_HIDDEN), (CMP_LEN * HEAD_DIM) ** -0.5),
        "cmp_k_b1": nrm(ks[12], (L, CMP_HIDDEN), 0.02),
        "cmp_k_w2": nrm(ks[13], (L, CMP_HIDDEN, HEAD_DIM), CMP_HIDDEN ** -0.5),
        "cmp_k_b2": nrm(ks[14], (L, HEAD_DIM), 0.02),
        "cmp_pos_v": nrm(ks[15], (L, CMP_LEN, HEAD_DIM), 0.1),
        "cmp_v_w1": nrm(ks[16], (L, CMP_LEN * HEAD_DIM, CMP_HIDDEN), (CMP_LEN * HEAD_DIM) ** -0.5),
        "cmp_v_b1": nrm(ks[17], (L, CMP_HIDDEN), 0.02),
        "cmp_v_w2": nrm(ks[18], (L, CMP_HIDDEN, HEAD_DIM), CMP_HIDDEN ** -0.5),
        "cmp_v_b2": nrm(ks[19], (L, HEAD_DIM), 0.02),
        "w_up_a": nrm(ks[20], (L, A_WIDTH, D_MODEL), A_WIDTH ** -0.5),
        "w_up_b": nrm(ks[21], (L, B_WIDTH, D_MODEL), B_WIDTH ** -0.5),
        "w_out": nrm(ks[22], (L, D_MODEL, D_MODEL), D_MODEL ** -0.5),
        "norm2_g": gain(ks[23], (L, D_MODEL)),
        "peer_wq": nrm(ks[24], (L, D_MODEL, PEER_HEADS * PEER_QDIM), D_MODEL ** -0.5),
        "peer_k1": nrm(ks[25], (L, PEER_NKEYS, PEER_HALF), PEER_HALF ** -0.5),
        "peer_k2": nrm(ks[26], (L, PEER_NKEYS, PEER_HALF), PEER_HALF ** -0.5),
        "peer_u": nrm(ks[27], (L, PEER_N, D_MODEL), D_MODEL ** -0.5),
        "peer_v": nrm(ks[28], (L, PEER_N, D_MODEL), 0.3),
    }


def reference(x, norm1_g, w_in, b_merge, a_q_g, a_k_g, b_q_g, b_kc_g, b_ks_g, b_kw_g,
              cmp_pos_k, cmp_k_w1, cmp_k_b1, cmp_k_w2, cmp_k_b2,
              cmp_pos_v, cmp_v_w1, cmp_v_b1, cmp_v_w2, cmp_v_b2,
              w_up_a, w_up_b, w_out, norm2_g, peer_wq, peer_k1, peer_k2, peer_u, peer_v):
    cos, sin = rope_tables(x.shape[1])
    for l in range(DEPTH):
        x = hybrid_layer(x, cos, sin, norm1_g[l], w_in[l], b_merge[l], a_q_g[l], a_k_g[l],
                         b_q_g[l], b_kc_g[l], b_ks_g[l], b_kw_g[l],
                         cmp_pos_k[l], cmp_k_w1[l], cmp_k_b1[l], cmp_k_w2[l], cmp_k_b2[l],
                         cmp_pos_v[l], cmp_v_w1[l], cmp_v_b1[l], cmp_v_w2[l], cmp_v_b2[l],
                         w_up_a[l], w_up_b[l], w_out[l], norm2_g[l],
                         peer_wq[l], peer_k1[l], peer_k2[l], peer_u[l], peer_v[l])
    return x
```

```python
import functools
import math

import jax
import jax.numpy as jnp
import numpy as np
from jax import lax
from jax.experimental import pallas as pl
from jax.experimental.pallas import tpu as pltpu

D_MODEL = 1024
HEAD_DIM = 64
ROT_DIM = HEAD_DIM // 4
ROT_HALF = ROT_DIM // 2
ROPE_THETA = 500000.0
NORM_EPS = 1e-6
NEG_INF = -1e30
BIG = 1e30

A_HEADS = 8
A_BLOCK = 256
A_TOPK = 3

B_HEADS = 8
B_KV_HEADS = 2
B_GROUP = B_HEADS // B_KV_HEADS
CMP_LEN = 32
CMP_STRIDE = 16
CMP_HIDDEN = 256
SLC_BLOCK = 64
SLC_TOPN = 16
WINDOW = 512

PEER_HEADS = 8
PEER_NKEYS = 128
PEER_N = PEER_NKEYS * PEER_NKEYS
PEER_QDIM = 256
PEER_HALF = PEER_QDIM // 2
PEER_TOPK = 16
PEER_SLOTS = PEER_HEADS * PEER_TOPK

A_WIDTH = A_HEADS * HEAD_DIM
B_WIDTH = B_HEADS * HEAD_DIM
B_KV_WIDTH = B_KV_HEADS * HEAD_DIM
N_BRANCH = 2

LANES = 128
SUBLANES = 8
VMEM_LIMIT_BYTES = 56 * 1024 * 1024

ROPE_COLS = 2 * A_WIDTH + B_WIDTH + 2 * B_KV_WIDTH
QK_AQ, QK_AK, QK_BQ, QK_BKS, QK_BKW = 0, 4, 8, 12, 13
PLAIN_COLS = A_WIDTH + 2 * B_KV_WIDTH
VV_AV, VV_BVS, VV_BVW = 0, 4, 5
MG_COLS = N_BRANCH * D_MODEL
B_PAIRS = B_WIDTH // LANES
B_HEAD_ORDER = tuple(h for p in range(B_PAIRS) for h in (p, B_GROUP + p))

TILE = 256
PEER_TB = 128
F32 = jnp.float32
BF16 = jnp.bfloat16
I32 = jnp.int32


def _const_spec(shape):
    n = len(shape)
    return pl.BlockSpec(shape, lambda *_: (0,) * n, pipeline_mode=pl.Buffered(1))


def _params(n_axes):
    return pltpu.CompilerParams(dimension_semantics=("arbitrary",) * n_axes,
                                vmem_limit_bytes=VMEM_LIMIT_BYTES)


def _dot(a, b):
    return jnp.dot(a, b, preferred_element_type=F32)


def _dot_nt(a, b):
    return lax.dot_general(a, b, (((1,), (1,)), ((), ())), preferred_element_type=F32)


def _split(a_f32):
    hi = a_f32.astype(BF16)
    lo = (a_f32 - hi.astype(F32)).astype(BF16)
    return hi, lo


def _split_dot(a_f32, b_bf16):
    hi, lo = _split(a_f32)
    return _dot(hi, b_bf16) + _dot(lo, b_bf16)


def _gelu_tanh(x):
    return 0.5 * x * (1.0 + jnp.tanh(math.sqrt(2.0 / math.pi) * (x + 0.044715 * (x * x * x))))


def _group_rmsnorm(y, bd, gain):
    ms = _split_dot(y * y, bd)
    return y * lax.rsqrt(ms + NORM_EPS) * gain


def _inproj_kernel(x_ref, g1_ref, w_rope_ref, w_plain_ref, w_kc_ref, w_vc_ref, w_gate_ref,
                   w_mg_ref, hg_ref, cos_ref, sin_ref, bd_ref, bm_ref,
                   qk_ref, vv_ref, kcin_ref, vcin_ref, gate_ref, mg_ref, kmean_ref):
    x = x_ref[...]
    h = x * lax.rsqrt(jnp.mean(x * x, axis=-1, keepdims=True) + NORM_EPS) * g1_ref[...]
    hb = h.astype(BF16)

    vv_ref[...] = _dot(hb, w_plain_ref[...]).astype(BF16)
    kcin_ref[...] = _dot(hb, w_kc_ref[...]).astype(BF16)
    vcin_ref[...] = _dot(hb, w_vc_ref[...]).astype(BF16)
    gate_ref[...] = jax.nn.sigmoid(_dot(hb, w_gate_ref[...]))
    mg_ref[...] = jax.nn.sigmoid(_dot(hb, w_mg_ref[...]) + bm_ref[...]).astype(BF16)

    cos = cos_ref[...]
    sin = sin_ref[...]
    bd = bd_ref[...]
    lane = lax.broadcasted_iota(I32, (TILE, LANES), 1)
    first = (lane % HEAD_DIM) < ROT_HALF
    for t in range(ROPE_COLS // LANES):
        sl = slice(t * LANES, (t + 1) * LANES)
        y = _dot(hb, w_rope_ref[:, sl])
        yn = _group_rmsnorm(y, bd, hg_ref[:, sl])
        up = pltpu.roll(yn, LANES - ROT_HALF, axis=1)
        dn = pltpu.roll(yn, ROT_HALF, axis=1)
        out = yn * cos + jnp.where(first, -up, dn) * sin
        qk_ref[:, sl] = out.astype(BF16)
        if QK_AK <= t < QK_AK + A_WIDTH // LANES:
            c = t - QK_AK
            kmean_ref[0, :, c * LANES:(c + 1) * LANES] = jnp.mean(out, axis=0, keepdims=True)


def _inproj(x2d, g1, w_rope, w_plain, w_kc, w_vc, w_gate, w_mg, hg, cos_t, sin_t, bd, bm, seq):
    tokens = x2d.shape[0]
    nt = tokens // TILE
    tiles_per_seq = seq // TILE
    row = lambda i: (i, 0)
    pos = lambda i: (i % tiles_per_seq, 0)
    return pl.pallas_call(
        _inproj_kernel,
        grid=(nt,),
        in_specs=[
            pl.BlockSpec((TILE, D_MODEL), row),
            _const_spec((1, D_MODEL)),
            _const_spec((D_MODEL, ROPE_COLS)),
            _const_spec((D_MODEL, PLAIN_COLS)),
            _const_spec((D_MODEL, LANES)),
            _const_spec((D_MODEL, LANES)),
            _const_spec((D_MODEL, LANES)),
            _const_spec((D_MODEL, MG_COLS)),
            _const_spec((1, ROPE_COLS)),
            pl.BlockSpec((TILE, LANES), pos),
            pl.BlockSpec((TILE, LANES), pos),
            _const_spec((LANES, LANES)),
            _const_spec((1, MG_COLS)),
        ],
        out_specs=[
            pl.BlockSpec((TILE, ROPE_COLS), row),
            pl.BlockSpec((TILE, PLAIN_COLS), row),
            pl.BlockSpec((TILE, LANES), row),
            pl.BlockSpec((TILE, LANES), row),
            pl.BlockSpec((TILE, LANES), row),
            pl.BlockSpec((TILE, MG_COLS), row),
            pl.BlockSpec((1, 1, A_WIDTH), lambda i: (i, 0, 0)),
        ],
        out_shape=[
            jax.ShapeDtypeStruct((tokens, ROPE_COLS), BF16),
            jax.ShapeDtypeStruct((tokens, PLAIN_COLS), BF16),
            jax.ShapeDtypeStruct((tokens, LANES), BF16),
            jax.ShapeDtypeStruct((tokens, LANES), BF16),
            jax.ShapeDtypeStruct((tokens, LANES), F32),
            jax.ShapeDtypeStruct((tokens, MG_COLS), BF16),
            jax.ShapeDtypeStruct((nt, 1, A_WIDTH), F32),
        ],
        compiler_params=_params(1),
        name="inproj",
    )(x2d, g1, w_rope, w_plain, w_kc, w_vc, w_gate, w_mg, hg, cos_t, sin_t, bd, bm)


def _online_init(m_ref, l_ref, acc_ref):
    m_ref[...] = jnp.full(m_ref.shape, NEG_INF, F32)
    l_ref[...] = jnp.zeros(l_ref.shape, F32)
    acc_ref[...] = jnp.zeros(acc_ref.shape, F32)


def _online_update(s, visible, v, m_ref, l_ref, acc_ref):
    s = jnp.where(visible, s, NEG_INF)
    m_old = m_ref[...]
    m_new = jnp.maximum(m_old, jnp.max(s, axis=-1, keepdims=True))
    alpha = jnp.exp(m_old - m_new)
    p = jnp.where(visible, jnp.exp(s - m_new), 0.0)
    l_ref[...] = alpha * l_ref[...] + jnp.sum(p, axis=-1, keepdims=True)
    acc_ref[...] = alpha * acc_ref[...] + _dot(p.astype(BF16), v)
    m_ref[...] = m_new


def _column_of(sel_b, index):
    onehot = jnp.where(lax.broadcasted_iota(I32, (LANES, TILE), 0) == index, 1.0, 0.0)
    return _dot(sel_b, onehot.astype(BF16)) > 0.5


def _moba_kernel(q_ref, k_ref, v_ref, kmh_ref, kml_ref, o_ref, m_ref, l_ref, acc_ref, *, topk):
    i = pl.program_id(2)
    scale = HEAD_DIM ** -0.5
    q_pair = q_ref[0]
    lane = lax.broadcasted_iota(I32, (TILE, LANES), 1)
    lane_f = lane.astype(F32)
    row = lax.broadcasted_iota(I32, (TILE, TILE), 0)
    col = lax.broadcasted_iota(I32, (TILE, TILE), 1)
    causal = col <= row
    own = pl.multiple_of(i * TILE, TILE)
    out = jnp.zeros((TILE, LANES), F32)
    for half in range(2):
        in_half = (lane // HEAD_DIM) == half
        q = jnp.where(in_half, q_pair, jnp.zeros_like(q_pair))
        gate = _dot_nt(q, kmh_ref[0]) + _dot_nt(q, kml_ref[0])
        gate = jnp.where(lane < i, gate, NEG_INF)
        sel = jnp.zeros((TILE, LANES), F32)
        for _ in range(topk):
            best = jnp.max(gate, axis=-1, keepdims=True)
            idx = jnp.min(jnp.where(gate == best, lane_f, float(LANES)), axis=-1, keepdims=True)
            pick = lane_f == idx
            sel = jnp.where(pick, jnp.where(best > 0.5 * NEG_INF, 1.0, sel), sel)
            gate = jnp.where(pick, NEG_INF, gate)
        sel_b = sel.astype(BF16)

        _online_init(m_ref, l_ref, acc_ref)
        s = _dot_nt(q, k_ref[0, pl.ds(own, TILE), :]) * scale
        _online_update(s, causal, v_ref[0, pl.ds(own, TILE), :], m_ref, l_ref, acc_ref)

        def past_block(n, carry):
            start = pl.multiple_of(n * TILE, TILE)
            s = _dot_nt(q, k_ref[0, pl.ds(start, TILE), :]) * scale
            _online_update(s, _column_of(sel_b, n), v_ref[0, pl.ds(start, TILE), :],
                           m_ref, l_ref, acc_ref)
            return carry

        lax.fori_loop(0, i, past_block, 0)
        out = jnp.where(in_half, acc_ref[...] / l_ref[...], out)
    o_ref[0] = out.astype(BF16)


def _moba(qk, vv, kmean_hi, kmean_lo, batch, seq):
    n_blk = seq // A_BLOCK
    pairs = A_WIDTH // LANES
    kernel = functools.partial(_moba_kernel, topk=min(A_TOPK, n_blk))
    return pl.pallas_call(
        kernel,
        grid=(batch, pairs, n_blk),
        in_specs=[
            pl.BlockSpec((1, TILE, LANES), lambda b, p, i: (b, i, QK_AQ + p)),
            pl.BlockSpec((1, seq, LANES), lambda b, p, i: (b, 0, QK_AK + p)),
            pl.BlockSpec((1, seq, LANES), lambda b, p, i: (b, 0, VV_AV + p)),
            pl.BlockSpec((1, LANES, LANES), lambda b, p, i: (b, 0, p)),
            pl.BlockSpec((1, LANES, LANES), lambda b, p, i: (b, 0, p)),
        ],
        out_specs=pl.BlockSpec((1, TILE, LANES), lambda b, p, i: (b, i, p)),
        out_shape=jax.ShapeDtypeStruct((batch, seq, A_WIDTH), BF16),
        scratch_shapes=[
            pltpu.VMEM((TILE, 1), F32),
            pltpu.VMEM((TILE, 1), F32),
            pltpu.VMEM((TILE, LANES), F32),
        ],
        compiler_params=_params(3),
        name="moba",
    )(qk, qk, vv, kmean_hi, kmean_lo)


def _compress_kernel(kin_ref, vin_ref, w1_ref, pos_ref, b1_ref, w2_ref, b2_ref, bd_ref, g_ref,
                     kc_ref, vc_ref):
    n_rows = kin_ref.shape[1]
    for c, (in_ref, out_ref) in enumerate(((kin_ref, kc_ref), (vin_ref, vc_ref))):
        r = in_ref[0]
        first = _dot(r, w1_ref[c, 0])
        second = _dot(r, w1_ref[c, 1])
        const = (_dot(pos_ref[c, 0], w1_ref[c, 0]) + _dot(pos_ref[c, 1], w1_ref[c, 1]))[0:1]
        hid = first + pltpu.roll(second, n_rows - 1, axis=0) + const + b1_ref[c]
        out = _dot(_gelu_tanh(hid).astype(BF16), w2_ref[c]) + b2_ref[c]
        if c == 0:
            out = _group_rmsnorm(out, bd_ref[...], g_ref[...])
        out_ref[0] = out.astype(BF16)


def _compress(kin, vin, w1, pos, b1, w2, b2, bd, gain, batch):
    n_rows = kin.shape[1]
    blk = lambda b: (b, 0, 0)
    hid2 = B_KV_HEADS * CMP_HIDDEN
    return pl.pallas_call(
        _compress_kernel,
        grid=(batch,),
        in_specs=[
            pl.BlockSpec((1, n_rows, CMP_STRIDE * LANES), blk),
            pl.BlockSpec((1, n_rows, CMP_STRIDE * LANES), blk),
            _const_spec((2, 2, CMP_STRIDE * LANES, hid2)),
            _const_spec((2, 2, SUBLANES, CMP_STRIDE * LANES)),
            _const_spec((2, 1, hid2)),
            _const_spec((2, hid2, LANES)),
            _const_spec((2, 1, LANES)),
            _const_spec((LANES, LANES)),
            _const_spec((1, LANES)),
        ],
        out_specs=[pl.BlockSpec((1, n_rows, LANES), blk), pl.BlockSpec((1, n_rows, LANES), blk)],
        out_shape=[jax.ShapeDtypeStruct((batch, n_rows, LANES), BF16)] * 2,
        compiler_params=_params(1),
        name="nsa_compress",
    )(kin, vin, w1, pos, b1, w2, b2, bd, gain)


def _nsa_cmp_kernel(q_ref, kc_ref, vc_ref, ov_ref, ocmp_ref, sel_ref, *, n_cmp, topn):
    i = pl.program_id(1)
    scale = HEAD_DIM ** -0.5
    n_pad = kc_ref.shape[1]
    lane = lax.broadcasted_iota(I32, (TILE, LANES), 1)
    pos = i * TILE + lax.broadcasted_iota(I32, (TILE, 1), 0)
    ncol = lax.broadcasted_iota(I32, (TILE, n_pad), 1)
    visible = (ncol * CMP_STRIDE + (CMP_LEN - 1) <= pos) & (ncol < n_cmp)
    kc = kc_ref[0]
    vc = vc_ref[0]
    ov = ov_ref[...]
    imp = jnp.zeros((TILE, LANES), F32)
    for p in range(B_PAIRS):
        q_pair = q_ref[0, :, p * LANES:(p + 1) * LANES]
        o_pair = jnp.zeros((TILE, LANES), F32)
        for half in range(2):
            in_half = (lane // HEAD_DIM) == half
            q = jnp.where(in_half, q_pair, jnp.zeros_like(q_pair))
            s = jnp.where(visible, _dot_nt(q, kc) * scale, NEG_INF)
            e = jnp.where(visible, jnp.exp(s - jnp.max(s, axis=-1, keepdims=True)), 0.0)
            tot = jnp.sum(e, axis=-1, keepdims=True)
            prob = jnp.where(tot > 0.0, e / tot, 0.0)
            o_pair = jnp.where(in_half, _dot(prob.astype(BF16), vc), o_pair)
            imp = imp + jnp.where(in_half, _split_dot(prob, ov), 0.0)
        ocmp_ref[0, :, p * LANES:(p + 1) * LANES] = o_pair.astype(BF16)

    j = lane % HEAD_DIM
    own = pos // SLC_BLOCK
    forced = (j == 0) | (j == own) | (j == own - 1)
    imp = jnp.where(forced, BIG, imp)
    imp = jnp.where(j <= own, imp, NEG_INF)
    low = lane < HEAD_DIM
    rank = jnp.zeros((TILE, LANES), F32)
    for k in range(HEAD_DIM):
        other = jnp.where(low, imp[:, k:k + 1], imp[:, HEAD_DIM + k:HEAD_DIM + k + 1])
        ahead = jnp.where(other > imp, 1.0, jnp.where(other == imp, jnp.where(k < j, 1.0, 0.0), 0.0))
        rank = rank + ahead
    keep = jnp.where(rank < float(topn), jnp.where(imp > 0.5 * NEG_INF, 1.0, 0.0), 0.0)
    sel_ref[0] = keep.astype(BF16)


def _nsa_cmp(qk, kc, vc, ov, batch, seq):
    n_pad = kc.shape[1]
    n_cmp = (seq - CMP_LEN) // CMP_STRIDE + 1
    topn = min(SLC_TOPN, seq // SLC_BLOCK)
    kernel = functools.partial(_nsa_cmp_kernel, n_cmp=n_cmp, topn=topn)
    return pl.pallas_call(
        kernel,
        grid=(batch, seq // TILE),
        in_specs=[
            pl.BlockSpec((1, TILE, B_WIDTH), lambda b, i: (b, i, QK_BQ * LANES // B_WIDTH)),
            pl.BlockSpec((1, n_pad, LANES), lambda b, i: (b, 0, 0)),
            pl.BlockSpec((1, n_pad, LANES), lambda b, i: (b, 0, 0)),
            _const_spec((n_pad, LANES)),
        ],
        out_specs=[
            pl.BlockSpec((1, TILE, B_WIDTH), lambda b, i: (b, i, 0)),
            pl.BlockSpec((1, TILE, LANES), lambda b, i: (b, i, 0)),
        ],
        out_shape=[
            jax.ShapeDtypeStruct((batch, seq, B_WIDTH), BF16),
            jax.ShapeDtypeStruct((batch, seq, LANES), BF16),
        ],
        compiler_params=_params(2),
        name="nsa_cmp_select",
    )(qk, kc, vc, ov)


def _nsa_attn_kernel(q_ref, ks_ref, vs_ref, kw_ref, vw_ref, sel_ref, ocmp_ref, gate_ref, eg_ref,
                     o_ref, m_ref, l_ref, acc_ref):
    i = pl.program_id(2)
    scale = HEAD_DIM ** -0.5
    blocks_per_tile = TILE // SLC_BLOCK
    q_pair = q_ref[0]
    sel_b = sel_ref[0]
    lane = lax.broadcasted_iota(I32, (TILE, LANES), 1)
    row = lax.broadcasted_iota(I32, (TILE, TILE), 0)
    col = lax.broadcasted_iota(I32, (TILE, TILE), 1)
    causal = col <= row
    ex_row = lax.broadcasted_iota(I32, (LANES, TILE), 0)
    ex_col = lax.broadcasted_iota(I32, (LANES, TILE), 1) // SLC_BLOCK
    own = pl.multiple_of(i * TILE, TILE)
    o_slc = jnp.zeros((TILE, LANES), F32)
    o_win = jnp.zeros((TILE, LANES), F32)
    for half in range(2):
        in_half = (lane // HEAD_DIM) == half
        q = jnp.where(in_half, q_pair, jnp.zeros_like(q_pair))

        def chosen(n):
            onehot = jnp.where(ex_row == half * HEAD_DIM + n * blocks_per_tile + ex_col, 1.0, 0.0)
            return _dot(sel_b, onehot.astype(BF16)) > 0.5

        _online_init(m_ref, l_ref, acc_ref)
        s = _dot_nt(q, ks_ref[0, pl.ds(own, TILE), :]) * scale
        _online_update(s, chosen(i) & causal, vs_ref[0, pl.ds(own, TILE), :], m_ref, l_ref, acc_ref)

        def past_tile(n, carry):
            start = pl.multiple_of(n * TILE, TILE)
            s = _dot_nt(q, ks_ref[0, pl.ds(start, TILE), :]) * scale
            _online_update(s, chosen(n), vs_ref[0, pl.ds(start, TILE), :], m_ref, l_ref, acc_ref)
            return carry

        lax.fori_loop(0, i, past_tile, 0)
        o_slc = jnp.where(in_half, acc_ref[...] / l_ref[...], o_slc)

        _online_init(m_ref, l_ref, acc_ref)
        s = _dot_nt(q, kw_ref[0, pl.ds(own, TILE), :]) * scale
        _online_update(s, causal, vw_ref[0, pl.ds(own, TILE), :], m_ref, l_ref, acc_ref)

        @pl.when(i >= 1)
        def _():
            start = pl.multiple_of((i - 1) * TILE, TILE)
            s = _dot_nt(q, kw_ref[0, pl.ds(start, TILE), :]) * scale
            _online_update(s, col >= 0, vw_ref[0, pl.ds(start, TILE), :], m_ref, l_ref, acc_ref)

        @pl.when(i >= 2)
        def _():
            start = pl.multiple_of((i - 2) * TILE, TILE)
            s = _dot_nt(q, kw_ref[0, pl.ds(start, TILE), :]) * scale
            _online_update(s, col > row, vw_ref[0, pl.ds(start, TILE), :], m_ref, l_ref, acc_ref)

        o_win = jnp.where(in_half, acc_ref[...] / l_ref[...], o_win)

    g = _split_dot(gate_ref[0], eg_ref[0])
    out = (g[:, 0:LANES] * ocmp_ref[0].astype(F32) + g[:, LANES:2 * LANES] * o_slc
           + g[:, 2 * LANES:3 * LANES] * o_win)
    o_ref[0] = out.astype(BF16)


def _nsa_attn(qk, vv, sel, ocmp, gate, eg, batch, seq):
    assert WINDOW == 2 * TILE
    kv = lambda t: pl.BlockSpec((1, seq, LANES), lambda b, p, i: (b, 0, t))
    return pl.pallas_call(
        _nsa_attn_kernel,
        grid=(batch, B_PAIRS, seq // TILE),
        in_specs=[
            pl.BlockSpec((1, TILE, LANES), lambda b, p, i: (b, i, QK_BQ + p)),
            kv(QK_BKS),
            pl.BlockSpec((1, seq, LANES), lambda b, p, i: (b, 0, VV_BVS)),
            kv(QK_BKW),
            pl.BlockSpec((1, seq, LANES), lambda b, p, i: (b, 0, VV_BVW)),
            pl.BlockSpec((1, TILE, LANES), lambda b, p, i: (b, i, 0)),
            pl.BlockSpec((1, TILE, LANES), lambda b, p, i: (b, i, p)),
            pl.BlockSpec((1, TILE, LANES), lambda b, p, i: (b, i, 0)),
            pl.BlockSpec((1, LANES, 3 * LANES), lambda b, p, i: (p, 0, 0)),
        ],
        out_specs=pl.BlockSpec((1, TILE, LANES), lambda b, p, i: (b, i, p)),
        out_shape=jax.ShapeDtypeStruct((batch, seq, B_WIDTH), BF16),
        scratch_shapes=[
            pltpu.VMEM((TILE, 1), F32),
            pltpu.VMEM((TILE, 1), F32),
            pltpu.VMEM((TILE, LANES), F32),
        ],
        compiler_params=_params(3),
        name="nsa_select_window",
    )(qk, qk, vv, qk, vv, sel, ocmp, gate, eg)


def _top_rows(v, row_f, count, fill):
    vals, rows = [], []
    for _ in range(count):
        best = jnp.max(v, axis=0, keepdims=True)
        idx = jnp.min(jnp.where(v == best, row_f, fill), axis=0, keepdims=True)
        vals.append(best)
        rows.append(idx)
        v = jnp.where(row_f == idx, -jnp.inf, v)
    return vals, rows


def _merge_peer_kernel(x_ref, ya_ref, yb_ref, mg_ref, wa_ref, wb_ref, wo_ref, g2_ref, wq_ref,
                       k1h_ref, k1l_ref, k2h_ref, k2l_ref,
                       x1_ref, hn_ref, idx_ref, gt_ref, q_scr):
    ua = _dot(ya_ref[...], wa_ref[...])
    ub = _dot(yb_ref[...], wb_ref[...])
    merged = (mg_ref[:, 0:D_MODEL].astype(F32) * ua + mg_ref[:, D_MODEL:2 * D_MODEL].astype(F32) * ub)
    x1 = x_ref[...] + _dot(merged.astype(BF16), wo_ref[...])
    x1_ref[...] = x1
    hn = x1 * lax.rsqrt(jnp.mean(x1 * x1, axis=-1, keepdims=True) + NORM_EPS) * g2_ref[...]
    hn_ref[...] = hn
    q_scr[...] = _dot(hn.astype(BF16), wq_ref[...])

    key_row = lax.broadcasted_iota(I32, (PEER_NKEYS, TILE), 0).astype(F32)
    n_cand = PEER_TOPK * PEER_TOPK
    cand_row = lax.broadcasted_iota(I32, (n_cand, TILE), 0).astype(F32)

    def per_head(h, carry):
        base = pl.multiple_of(h * PEER_QDIM, PEER_QDIM)
        q1h, q1l = _split(q_scr[:, pl.ds(base, PEER_HALF)])
        q2h, q2l = _split(q_scr[:, pl.ds(base + PEER_HALF, PEER_HALF)])
        k1h, k2h = k1h_ref[...], k2h_ref[...]
        s1 = _dot_nt(k1h, q1h) + _dot_nt(k1h, q1l) + _dot_nt(k1l_ref[...], q1h)
        s2 = _dot_nt(k2h, q2h) + _dot_nt(k2h, q2l) + _dot_nt(k2l_ref[...], q2h)
        v1, i1 = _top_rows(s1, key_row, PEER_TOPK, float(PEER_NKEYS))
        v2, i2 = _top_rows(s2, key_row, PEER_TOPK, float(PEER_NKEYS))
        v2m = jnp.concatenate(v2, axis=0)
        i2m = jnp.concatenate(i2, axis=0)
        cand = jnp.concatenate([v1[a] + v2m for a in range(PEER_TOPK)], axis=0)
        cidx = jnp.concatenate([i1[a] * float(PEER_NKEYS) + i2m for a in range(PEER_TOPK)], axis=0)
        vals, experts = [], []
        v = cand
        for _ in range(PEER_TOPK):
            best = jnp.max(v, axis=0, keepdims=True)
            at = jnp.min(jnp.where(v == best, cand_row, float(n_cand)), axis=0, keepdims=True)
            hit = cand_row == at
            experts.append(jnp.max(jnp.where(hit, cidx, -1.0), axis=0, keepdims=True))
            vals.append(best)
            v = jnp.where(hit, -jnp.inf, v)
        vals = jnp.concatenate(vals, axis=0)
        e = jnp.exp(vals - vals[0:1])
        out_row = pl.multiple_of(h * PEER_TOPK, PEER_TOPK)
        gt_ref[pl.ds(out_row, PEER_TOPK), :] = e / jnp.sum(e, axis=0, keepdims=True)
        idx_ref[pl.ds(out_row, PEER_TOPK), :] = jnp.concatenate(experts, axis=0).astype(I32)
        return carry

    lax.fori_loop(0, PEER_HEADS, per_head, 0)


def _merge_peer(x2d, ya, yb, mg, wa, wb, wo, g2, wq, k1h, k1l, k2h, k2l):
    tokens = x2d.shape[0]
    row = lambda i: (i, 0)
    colblk = lambda i: (0, i)
    return pl.pallas_call(
        _merge_peer_kernel,
        grid=(tokens // TILE,),
        in_specs=[
            pl.BlockSpec((TILE, D_MODEL), row),
            pl.BlockSpec((TILE, A_WIDTH), row),
            pl.BlockSpec((TILE, B_WIDTH), row),
            pl.BlockSpec((TILE, MG_COLS), row),
            _const_spec((A_WIDTH, D_MODEL)),
            _const_spec((B_WIDTH, D_MODEL)),
            _const_spec((D_MODEL, D_MODEL)),
            _const_spec((1, D_MODEL)),
            _const_spec((D_MODEL, PEER_HEADS * PEER_QDIM)),
            _const_spec((PEER_NKEYS, PEER_HALF)),
            _const_spec((PEER_NKEYS, PEER_HALF)),
            _const_spec((PEER_NKEYS, PEER_HALF)),
            _const_spec((PEER_NKEYS, PEER_HALF)),
        ],
        out_specs=[
            pl.BlockSpec((TILE, D_MODEL), row),
            pl.BlockSpec((TILE, D_MODEL), row),
            pl.BlockSpec((PEER_SLOTS, TILE), colblk),
            pl.BlockSpec((PEER_SLOTS, TILE), colblk),
        ],
        out_shape=[
            jax.ShapeDtypeStruct((tokens, D_MODEL), F32),
            jax.ShapeDtypeStruct((tokens, D_MODEL), F32),
            jax.ShapeDtypeStruct((PEER_SLOTS, tokens), I32),
            jax.ShapeDtypeStruct((PEER_SLOTS, tokens), F32),
        ],
        scratch_shapes=[pltpu.VMEM((TILE, PEER_HEADS * PEER_QDIM), F32)],
        compiler_params=_params(1),
        name="merge_peer_topk",
    )(x2d, ya, yb, mg, wa, wb, wo, g2, wq, k1h, k1l, k2h, k2l)


HALF_ROWS = SUBLANES // 2
HI_MASK = -65536


def _unpack(words):
    hi = pltpu.bitcast(words & HI_MASK, F32)
    lo = pltpu.bitcast(words << 16, F32)
    return hi, lo


def _fold_rows(v, shift):
    return v + pltpu.roll(v, shift, axis=0)


def _peer_u_kernel(idx_ref, hn_ref, gt_ref, tab_ref, w_ref):
    lane = lax.broadcasted_iota(I32, (SUBLANES, LANES), 1)
    sub = lax.broadcasted_iota(I32, (SUBLANES, LANES), 0)
    keep_hi_pair = (sub % 4) >= 2
    odd = (sub % 2) == 1
    groups = PEER_SLOTS // SUBLANES

    def per_token(t, accs):
        x = hn_ref[t]
        xa = jnp.concatenate([x[0:HALF_ROWS], x[0:HALF_ROWS]], axis=0)
        xb = jnp.concatenate([x[HALF_ROWS:], x[HALF_ROWS:]], axis=0)

        def pair(ja, jb):
            words = jnp.concatenate([tab_ref[idx_ref[ja, t]], tab_ref[idx_ref[jb, t]]], axis=0)
            hi, lo = _unpack(words)
            return hi * xa + lo * xb

        new = []
        for g in range(groups):
            j = g * SUBLANES
            quads = []
            for (a, b, c, d) in ((j + 3, j + 7, j + 1, j + 5), (j + 2, j + 6, j, j + 4)):
                w_ab = _fold_rows(pair(a, b), 2)
                w_cd = _fold_rows(pair(c, d), 2)
                z = jnp.where(keep_hi_pair, w_ab, pltpu.roll(w_cd, 6, axis=0))
                quads.append(_fold_rows(z, 1))
            folded = jnp.where(odd, quads[0], pltpu.roll(quads[1], 7, axis=0))
            total = jnp.sum(folded, axis=1, keepdims=True)
            new.append(jnp.where(lane == t, total, accs[g]))
        return tuple(new)

    init = tuple(jnp.zeros((SUBLANES, LANES), F32) for _ in range(groups))
    accs = lax.fori_loop(0, PEER_TB, per_token, init)
    a = jnp.concatenate(accs, axis=0)
    w_ref[...] = _gelu_tanh(a) * gt_ref[...]


def _peer_u(idx, hn3, gt, tab):
    tokens = hn3.shape[0]
    return pl.pallas_call(
        _peer_u_kernel,
        grid=(tokens // PEER_TB,),
        in_specs=[
            pl.BlockSpec((PEER_SLOTS, PEER_TB), lambda i: (0, i), memory_space=pltpu.SMEM),
            pl.BlockSpec((PEER_TB, SUBLANES, LANES), lambda i: (i, 0, 0)),
            pl.BlockSpec((PEER_SLOTS, PEER_TB), lambda i: (0, i)),
            _const_spec((PEER_N, HALF_ROWS, LANES)),
        ],
        out_specs=pl.BlockSpec((PEER_SLOTS, PEER_TB), lambda i: (0, i)),
        out_shape=jax.ShapeDtypeStruct((PEER_SLOTS, tokens), F32),
        compiler_params=_params(1),
        name="peer_expert_in",
    )(idx, hn3, gt, tab)


def _peer_v_kernel(idx_ref, w_ref, x1_ref, tab_ref, o_ref):
    def per_token(t, carry):
        acc_hi = jnp.zeros((HALF_ROWS, LANES), F32)
        acc_lo = jnp.zeros((HALF_ROWS, LANES), F32)
        for j in range(PEER_SLOTS):
            hi, lo = _unpack(tab_ref[idx_ref[j, t]])
            w = w_ref[j, t]
            acc_hi = acc_hi + w * hi
            acc_lo = acc_lo + w * lo
        o_ref[t] = x1_ref[t] + jnp.concatenate([acc_hi, acc_lo], axis=0)
        return carry

    lax.fori_loop(0, PEER_TB, per_token, 0)


def _peer_v(idx, w, x13, tab):
    tokens = x13.shape[0]
    smem = lambda: pl.BlockSpec((PEER_SLOTS, PEER_TB), lambda i: (0, i), memory_space=pltpu.SMEM)
    return pl.pallas_call(
        _peer_v_kernel,
        grid=(tokens // PEER_TB,),
        in_specs=[
            smem(),
            smem(),
            pl.BlockSpec((PEER_TB, SUBLANES, LANES), lambda i: (i, 0, 0)),
            _const_spec((PEER_N, HALF_ROWS, LANES)),
        ],
        out_specs=pl.BlockSpec((PEER_TB, SUBLANES, LANES), lambda i: (i, 0, 0)),
        out_shape=jax.ShapeDtypeStruct((tokens, SUBLANES, LANES), F32),
        compiler_params=_params(1),
        name="peer_expert_out",
    )(idx, w, x13, tab)


def _pack_table(t):
    bits = lax.bitcast_convert_type(t.astype(BF16), jnp.uint16).astype(jnp.uint32)
    half = D_MODEL // 2
    words = (bits[:, :half] << 16) | bits[:, half:]
    return lax.bitcast_convert_type(words, I32).reshape(t.shape[0], HALF_ROWS, LANES)


def _rope_tables(seq):
    inv = ROPE_THETA ** (-jnp.arange(0, ROT_DIM, 2, dtype=F32) / ROT_DIM)
    ang = jnp.arange(seq, dtype=F32)[:, None] * inv[None, :]
    d = np.arange(LANES) % HEAD_DIM
    cos = jnp.where(d[None, :] < ROT_DIM, jnp.cos(ang)[:, d % ROT_HALF], 1.0)
    sin = jnp.where(d[None, :] < ROT_DIM, jnp.sin(ang)[:, d % ROT_HALF], 0.0)
    return cos.astype(F32), sin.astype(F32)


def _compress_weights(w1, pos):
    out_w, out_p = [], []
    for part in range(2):
        wpart = w1[part * CMP_STRIDE * HEAD_DIM:(part + 1) * CMP_STRIDE * HEAD_DIM]
        wpart = wpart.reshape(CMP_STRIDE, HEAD_DIM, CMP_HIDDEN)
        full = jnp.zeros((CMP_STRIDE, B_KV_HEADS, HEAD_DIM, B_KV_HEADS, CMP_HIDDEN), F32)
        for g in range(B_KV_HEADS):
            full = full.at[:, g, :, g, :].set(wpart)
        out_w.append(full.reshape(CMP_STRIDE * LANES, B_KV_HEADS * CMP_HIDDEN))
        ppart = pos[part * CMP_STRIDE:(part + 1) * CMP_STRIDE]
        prow = jnp.tile(ppart[:, None, :], (1, B_KV_HEADS, 1)).reshape(1, CMP_STRIDE * LANES)
        out_p.append(jnp.tile(prow, (SUBLANES, 1)))
    return jnp.stack(out_w).astype(BF16), jnp.stack(out_p).astype(BF16)


def kernel(x, norm1_g, w_in, b_merge, a_q_g, a_k_g, b_q_g, b_kc_g, b_ks_g, b_kw_g, cmp_pos_k,
           cmp_k_w1, cmp_k_b1, cmp_k_w2, cmp_k_b2, cmp_pos_v, cmp_v_w1, cmp_v_b1, cmp_v_w2,
           cmp_v_b2, w_up_a, w_up_b, w_out, norm2_g, peer_wq, peer_k1, peer_k2, peer_u, peer_v):
    batch, seq, _ = x.shape
    assert seq % TILE == 0 and seq // SLC_BLOCK <= HEAD_DIM
    tokens = batch * seq
    l = 0
    x2d = x.reshape(tokens, D_MODEL)

    w = w_in[l]
    sizes = (A_WIDTH, A_WIDTH, A_WIDTH, B_WIDTH) + (B_KV_WIDTH,) * 6 + (3 * B_HEADS, MG_COLS)
    offs = np.concatenate([[0], np.cumsum(sizes)])
    seg = lambda k: w[:, offs[k]:offs[k + 1]]
    aq, ak, av, bq, bkc, bvc, bks, bvs, bkw, bvw, bgate, mgate = (seg(k) for k in range(12))
    bq_perm = bq.reshape(D_MODEL, B_HEADS, HEAD_DIM)[:, np.array(B_HEAD_ORDER)].reshape(D_MODEL, B_WIDTH)
    w_rope = jnp.concatenate([aq, ak, bq_perm, bks, bkw], axis=1).astype(BF16)
    w_plain = jnp.concatenate([av, bvs, bvw], axis=1).astype(BF16)
    w_gate = jnp.pad(bgate, ((0, 0), (0, LANES - 3 * B_HEADS))).astype(BF16)
    hg = jnp.concatenate([jnp.tile(a_q_g[l], A_HEADS), jnp.tile(a_k_g[l], A_HEADS),
                          jnp.tile(b_q_g[l], B_HEADS), jnp.tile(b_ks_g[l], B_KV_HEADS),
                          jnp.tile(b_kw_g[l], B_KV_HEADS)])[None, :]
    cos_t, sin_t = _rope_tables(seq)
    bd = jnp.asarray(np.kron(np.eye(LANES // HEAD_DIM), np.full((HEAD_DIM, HEAD_DIM), 1.0 / HEAD_DIM)),
                     BF16)

    qk, vv, kcin, vcin, gate, mg, kmean = _inproj(
        x2d, norm1_g[l][None, :], w_rope, w_plain, bkc.astype(BF16), bvc.astype(BF16), w_gate,
        mgate.astype(BF16), hg, cos_t, sin_t, bd, b_merge[l].reshape(1, MG_COLS), seq)
    qk = qk.reshape(batch, seq, ROPE_COLS)
    vv = vv.reshape(batch, seq, PLAIN_COLS)

    n_blk = seq // A_BLOCK
    kmean = jnp.pad(kmean.reshape(batch, n_blk, A_WIDTH), ((0, 0), (0, LANES - n_blk), (0, 0)))
    km_hi = kmean.astype(BF16)
    km_lo = (kmean - km_hi.astype(F32)).astype(BF16)
    ya = _moba(qk, vv, km_hi, km_lo, batch, seq)

    n_rows = seq // CMP_STRIDE
    wk1, pk = _compress_weights(cmp_k_w1[l], cmp_pos_k[l])
    wv1, pv = _compress_weights(cmp_v_w1[l], cmp_pos_v[l])
    blockdiag = lambda m: jnp.kron(jnp.eye(B_KV_HEADS, dtype=F32), m)
    w2 = jnp.stack([blockdiag(cmp_k_w2[l]), blockdiag(cmp_v_w2[l])]).astype(BF16)
    b1 = jnp.stack([jnp.tile(cmp_k_b1[l], B_KV_HEADS), jnp.tile(cmp_v_b1[l], B_KV_HEADS)])[:, None, :]
    b2 = jnp.stack([jnp.tile(cmp_k_b2[l], B_KV_HEADS), jnp.tile(cmp_v_b2[l], B_KV_HEADS)])[:, None, :]
    kc, vc = _compress(kcin.reshape(batch, n_rows, CMP_STRIDE * LANES),
                       vcin.reshape(batch, n_rows, CMP_STRIDE * LANES),
                       jnp.stack([wk1, wv1]), jnp.stack([pk, pv]), b1, w2, b2, bd,
                       jnp.tile(b_kc_g[l], B_KV_HEADS)[None, :], batch)

    n_cmp = (seq - CMP_LEN) // CMP_STRIDE + 1
    n_slc = seq // SLC_BLOCK
    ci = np.arange(n_rows)[:, None]
    sj = np.arange(HEAD_DIM)[None, :]
    ov = ((ci * CMP_STRIDE < (sj + 1) * SLC_BLOCK) & (ci * CMP_STRIDE + CMP_LEN > sj * SLC_BLOCK)
          & (ci < n_cmp) & (sj < n_slc)).astype(np.float32)
    ov = jnp.asarray(np.concatenate([ov, ov], axis=1), BF16)
    ocmp, sel = _nsa_cmp(qk, kc, vc, ov, batch, seq)

    eg = np.zeros((B_PAIRS, LANES, 3 * LANES), np.float32)
    for p in range(B_PAIRS):
        for half in range(2):
            head = B_HEAD_ORDER[2 * p + half]
            for c in range(3):
                eg[p, head * 3 + c, c * LANES + half * HEAD_DIM:c * LANES + (half + 1) * HEAD_DIM] = 1.0
    yb = _nsa_attn(qk, vv, sel, ocmp, gate.reshape(batch, seq, LANES), jnp.asarray(eg, BF16),
                   batch, seq)

    wb_perm = w_up_b[l].reshape(B_HEADS, HEAD_DIM, D_MODEL)[np.array(B_HEAD_ORDER)].reshape(
        B_WIDTH, D_MODEL)
    k1h, k1l = _split(peer_k1[l])
    k2h, k2l = _split(peer_k2[l])
    x1, hn, idx, gt = _merge_peer(
        x2d, ya.reshape(tokens, A_WIDTH), yb.reshape(tokens, B_WIDTH), mg,
        w_up_a[l].astype(BF16), wb_perm.astype(BF16), w_out[l].astype(BF16),
        norm2_g[l][None, :], peer_wq[l].astype(BF16), k1h, k1l, k2h, k2l)

    wts = _peer_u(idx, hn.reshape(tokens, SUBLANES, LANES), gt, _pack_table(peer_u[l]))
    out = _peer_v(idx, wts, x1.reshape(tokens, SUBLANES, LANES), _pack_table(peer_v[l]))
    return out.reshape(batch, seq, D_MODEL)
```

```python
import functools
import math

import jax
import jax.numpy as jnp
import numpy as np
from jax import lax
from jax.experimental import pallas as pl
from jax.experimental.pallas import tpu as pltpu

D_MODEL = 1024
HEAD_DIM = 64
ROT_DIM = HEAD_DIM // 4
ROT_HALF = ROT_DIM // 2
ROPE_THETA = 500000.0
NORM_EPS = 1e-6
NEG_INF = -1e30
BIG = 1e30

A_HEADS = 8
A_BLOCK = 256
A_TOPK = 3

B_HEADS = 8
B_KV_HEADS = 2
B_GROUP = B_HEADS // B_KV_HEADS
CMP_LEN = 32
CMP_STRIDE = 16
CMP_HIDDEN = 256
SLC_BLOCK = 64
SLC_TOPN = 16
WINDOW = 512

PEER_HEADS = 8
PEER_NKEYS = 128
PEER_N = PEER_NKEYS * PEER_NKEYS
PEER_QDIM = 256
PEER_HALF = PEER_QDIM // 2
PEER_TOPK = 16
PEER_SLOTS = PEER_HEADS * PEER_TOPK

A_WIDTH = A_HEADS * HEAD_DIM
B_WIDTH = B_HEADS * HEAD_DIM
B_KV_WIDTH = B_KV_HEADS * HEAD_DIM
N_BRANCH = 2

LANES = 128
SUBLANES = 8
VMEM_LIMIT_BYTES = 56 * 1024 * 1024

ROPE_COLS = 2 * A_WIDTH + B_WIDTH + 2 * B_KV_WIDTH
QK_AQ, QK_AK, QK_BQ, QK_BKS, QK_BKW = 0, 4, 8, 12, 13
PLAIN_COLS = A_WIDTH + 2 * B_KV_WIDTH
VV_AV, VV_BVS, VV_BVW = 0, 4, 5
MG_COLS = N_BRANCH * D_MODEL
B_PAIRS = B_WIDTH // LANES
B_HEAD_ORDER = tuple(h for p in range(B_PAIRS) for h in (p, B_GROUP + p))

TILE = 256
PEER_TB = 128
F32 = jnp.float32
BF16 = jnp.bfloat16
I32 = jnp.int32


def _const_spec(shape):
    n = len(shape)
    return pl.BlockSpec(shape, lambda *_: (0,) * n, pipeline_mode=pl.Buffered(1))


def _params(n_axes):
    return pltpu.CompilerParams(dimension_semantics=("arbitrary",) * n_axes,
                                vmem_limit_bytes=VMEM_LIMIT_BYTES)


def _dot(a, b):
    return jnp.dot(a, b, preferred_element_type=F32)


def _dot_nt(a, b):
    return lax.dot_general(a, b, (((1,), (1,)), ((), ())), preferred_element_type=F32)


def _split(a_f32):
    hi = a_f32.astype(BF16)
    lo = (a_f32 - hi.astype(F32)).astype(BF16)
    return hi, lo


def _split_dot(a_f32, b_bf16):
    hi, lo = _split(a_f32)
    return _dot(hi, b_bf16) + _dot(lo, b_bf16)


def _gelu_tanh(x):
    return 0.5 * x * (1.0 + jnp.tanh(math.sqrt(2.0 / math.pi) * (x + 0.044715 * (x * x * x))))


def _group_rmsnorm(y, bd, gain):
    ms = _split_dot(y * y, bd)
    return y * lax.rsqrt(ms + NORM_EPS) * gain


def _inproj_kernel(x_ref, g1_ref, w_rope_ref, w_plain_ref, w_kc_ref, w_vc_ref, w_gate_ref,
                   w_mg_ref, hg_ref, cos_ref, sin_ref, bd_ref, bm_ref,
                   qk_ref, vv_ref, kcin_ref, vcin_ref, gate_ref, mg_ref, kmean_ref):
    x = x_ref[...]
    h = x * lax.rsqrt(jnp.mean(x * x, axis=-1, keepdims=True) + NORM_EPS) * g1_ref[...]
    hb = h.astype(BF16)

    vv_ref[...] = _dot(hb, w_plain_ref[...]).astype(BF16)
    kcin_ref[...] = _dot(hb, w_kc_ref[...]).astype(BF16)
    vcin_ref[...] = _dot(hb, w_vc_ref[...]).astype(BF16)
    gate_ref[...] = jax.nn.sigmoid(_dot(hb, w_gate_ref[...]))
    mg_ref[...] = jax.nn.sigmoid(_dot(hb, w_mg_ref[...]) + bm_ref[...]).astype(BF16)

    cos = cos_ref[...]
    sin = sin_ref[...]
    bd = bd_ref[...]
    lane = lax.broadcasted_iota(I32, (TILE, LANES), 1)
    first = (lane % HEAD_DIM) < ROT_HALF
    for t in range(ROPE_COLS // LANES):
        sl = slice(t * LANES, (t + 1) * LANES)
        y = _dot(hb, w_rope_ref[:, sl])
        yn = _group_rmsnorm(y, bd, hg_ref[:, sl])
        up = pltpu.roll(yn, LANES - ROT_HALF, axis=1)
        dn = pltpu.roll(yn, ROT_HALF, axis=1)
        out = yn * cos + jnp.where(first, -up, dn) * sin
        qk_ref[:, sl] = out.astype(BF16)
        if QK_AK <= t < QK_AK + A_WIDTH // LANES:
            c = t - QK_AK
            kmean_ref[0, :, c * LANES:(c + 1) * LANES] = jnp.mean(out, axis=0, keepdims=True)


def _inproj(x2d, g1, w_rope, w_plain, w_kc, w_vc, w_gate, w_mg, hg, cos_t, sin_t, bd, bm, seq):
    tokens = x2d.shape[0]
    nt = tokens // TILE
    tiles_per_seq = seq // TILE
    row = lambda i: (i, 0)
    pos = lambda i: (i % tiles_per_seq, 0)
    return pl.pallas_call(
        _inproj_kernel,
        grid=(nt,),
        in_specs=[
            pl.BlockSpec((TILE, D_MODEL), row),
            _const_spec((1, D_MODEL)),
            _const_spec((D_MODEL, ROPE_COLS)),
            _const_spec((D_MODEL, PLAIN_COLS)),
            _const_spec((D_MODEL, LANES)),
            _const_spec((D_MODEL, LANES)),
            _const_spec((D_MODEL, LANES)),
            _const_spec((D_MODEL, MG_COLS)),
            _const_spec((1, ROPE_COLS)),
            pl.BlockSpec((TILE, LANES), pos),
            pl.BlockSpec((TILE, LANES), pos),
            _const_spec((LANES, LANES)),
            _const_spec((1, MG_COLS)),
        ],
        out_specs=[
            pl.BlockSpec((TILE, ROPE_COLS), row),
            pl.BlockSpec((TILE, PLAIN_COLS), row),
            pl.BlockSpec((TILE, LANES), row),
            pl.BlockSpec((TILE, LANES), row),
            pl.BlockSpec((TILE, LANES), row),
            pl.BlockSpec((TILE, MG_COLS), row),
            pl.BlockSpec((1, 1, A_WIDTH), lambda i: (i, 0, 0)),
        ],
        out_shape=[
            jax.ShapeDtypeStruct((tokens, ROPE_COLS), BF16),
            jax.ShapeDtypeStruct((tokens, PLAIN_COLS), BF16),
            jax.ShapeDtypeStruct((tokens, LANES), BF16),
            jax.ShapeDtypeStruct((tokens, LANES), BF16),
            jax.ShapeDtypeStruct((tokens, LANES), F32),
            jax.ShapeDtypeStruct((tokens, MG_COLS), BF16),
            jax.ShapeDtypeStruct((nt, 1, A_WIDTH), F32),
        ],
        compiler_params=_params(1),
        name="inproj",
    )(x2d, g1, w_rope, w_plain, w_kc, w_vc, w_gate, w_mg, hg, cos_t, sin_t, bd, bm)


def _online_init(m_ref, l_ref, acc_ref):
    m_ref[...] = jnp.full(m_ref.shape, NEG_INF, F32)
    l_ref[...] = jnp.zeros(l_ref.shape, F32)
    acc_ref[...] = jnp.zeros(acc_ref.shape, F32)


def _online_update(s, visible, v, m_ref, l_ref, acc_ref):
    s = jnp.where(visible, s, NEG_INF)
    m_old = m_ref[...]
    m_new = jnp.maximum(m_old, jnp.max(s, axis=-1, keepdims=True))
    alpha = jnp.exp(m_old - m_new)
    p = jnp.where(visible, jnp.exp(s - m_new), 0.0)
    l_ref[...] = alpha * l_ref[...] + jnp.sum(p, axis=-1, keepdims=True)
    acc_ref[...] = alpha * acc_ref[...] + _dot(p.astype(BF16), v)
    m_ref[...] = m_new


def _column_of(sel_b, index):
    onehot = jnp.where(lax.broadcasted_iota(I32, (LANES, TILE), 0) == index, 1.0, 0.0)
    return _dot(sel_b, onehot.astype(BF16)) > 0.5


def _moba_kernel(q_ref, k_ref, v_ref, kmh_ref, kml_ref, o_ref, m_ref, l_ref, acc_ref, *, topk):
    i = pl.program_id(2)
    scale = HEAD_DIM ** -0.5
    q_pair = q_ref[0]
    lane = lax.broadcasted_iota(I32, (TILE, LANES), 1)
    lane_f = lane.astype(F32)
    row = lax.broadcasted_iota(I32, (TILE, TILE), 0)
    col = lax.broadcasted_iota(I32, (TILE, TILE), 1)
    causal = col <= row
    own = pl.multiple_of(i * TILE, TILE)
    out = jnp.zeros((TILE, LANES), F32)
    for half in range(2):
        in_half = (lane // HEAD_DIM) == half
        q = jnp.where(in_half, q_pair, jnp.zeros_like(q_pair))
        gate = _dot_nt(q, kmh_ref[0]) + _dot_nt(q, kml_ref[0])
        gate = jnp.where(lane < i, gate, NEG_INF)
        sel = jnp.zeros((TILE, LANES), F32)
        for _ in range(topk):
            best = jnp.max(gate, axis=-1, keepdims=True)
            idx = jnp.min(jnp.where(gate == best, lane_f, float(LANES)), axis=-1, keepdims=True)
            pick = lane_f == idx
            sel = jnp.where(pick, jnp.where(best > 0.5 * NEG_INF, 1.0, sel), sel)
            gate = jnp.where(pick, NEG_INF, gate)
        sel_b = sel.astype(BF16)

        _online_init(m_ref, l_ref, acc_ref)
        s = _dot_nt(q, k_ref[0, pl.ds(own, TILE), :]) * scale
        _online_update(s, causal, v_ref[0, pl.ds(own, TILE), :], m_ref, l_ref, acc_ref)

        def past_block(n, carry):
            start = pl.multiple_of(n * TILE, TILE)
            s = _dot_nt(q, k_ref[0, pl.ds(start, TILE), :]) * scale
            _online_update(s, _column_of(sel_b, n), v_ref[0, pl.ds(start, TILE), :],
                           m_ref, l_ref, acc_ref)
            return carry

        lax.fori_loop(0, i, past_block, 0)
        out = jnp.where(in_half, acc_ref[...] / l_ref[...], out)
    o_ref[0] = out.astype(BF16)


def _moba(qk, vv, kmean_hi, kmean_lo, batch, seq):
    n_blk = seq // A_BLOCK
    pairs = A_WIDTH // LANES
    kernel = functools.partial(_moba_kernel, topk=min(A_TOPK, n_blk))
    return pl.pallas_call(
        kernel,
        grid=(batch, pairs, n_blk),
        in_specs=[
            pl.BlockSpec((1, TILE, LANES), lambda b, p, i: (b, i, QK_AQ + p)),
            pl.BlockSpec((1, seq, LANES), lambda b, p, i: (b, 0, QK_AK + p)),
            pl.BlockSpec((1, seq, LANES), lambda b, p, i: (b, 0, VV_AV + p)),
            pl.BlockSpec((1, LANES, LANES), lambda b, p, i: (b, 0, p)),
            pl.BlockSpec((1, LANES, LANES), lambda b, p, i: (b, 0, p)),
        ],
        out_specs=pl.BlockSpec((1, TILE, LANES), lambda b, p, i: (b, i, p)),
        out_shape=jax.ShapeDtypeStruct((batch, seq, A_WIDTH), BF16),
        scratch_shapes=[
            pltpu.VMEM((TILE, 1), F32),
            pltpu.VMEM((TILE, 1), F32),
            pltpu.VMEM((TILE, LANES), F32),
        ],
        compiler_params=_params(3),
        name="moba",
    )(qk, qk, vv, kmean_hi, kmean_lo)


def _compress_kernel(kin_ref, vin_ref, w1_ref, pos_ref, b1_ref, w2_ref, b2_ref, bd_ref, g_ref,
                     kc_ref, vc_ref):
    n_rows = kin_ref.shape[1]
    for c, (in_ref, out_ref) in enumerate(((kin_ref, kc_ref), (vin_ref, vc_ref))):
        r = in_ref[0]
        first = _dot(r, w1_ref[c, 0])
        second = _dot(r, w1_ref[c, 1])
        const = (_dot(pos_ref[c, 0], w1_ref[c, 0]) + _dot(pos_ref[c, 1], w1_ref[c, 1]))[0:1]
        hid = first + pltpu.roll(second, n_rows - 1, axis=0) + const + b1_ref[c]
        out = _dot(_gelu_tanh(hid).astype(BF16), w2_ref[c]) + b2_ref[c]
        if c == 0:
            out = _group_rmsnorm(out, bd_ref[...], g_ref[...])
        out_ref[0] = out.astype(BF16)


def _compress(kin, vin, w1, pos, b1, w2, b2, bd, gain, batch):
    n_rows = kin.shape[1]
    blk = lambda b: (b, 0, 0)
    hid2 = B_KV_HEADS * CMP_HIDDEN
    return pl.pallas_call(
        _compress_kernel,
        grid=(batch,),
        in_specs=[
            pl.BlockSpec((1, n_rows, CMP_STRIDE * LANES), blk),
            pl.BlockSpec((1, n_rows, CMP_STRIDE * LANES), blk),
            _const_spec((2, 2, CMP_STRIDE * LANES, hid2)),
            _const_spec((2, 2, SUBLANES, CMP_STRIDE * LANES)),
            _const_spec((2, 1, hid2)),
            _const_spec((2, hid2, LANES)),
            _const_spec((2, 1, LANES)),
            _const_spec((LANES, LANES)),
            _const_spec((1, LANES)),
        ],
        out_specs=[pl.BlockSpec((1, n_rows, LANES), blk), pl.BlockSpec((1, n_rows, LANES), blk)],
        out_shape=[jax.ShapeDtypeStruct((batch, n_rows, LANES), BF16)] * 2,
        compiler_params=_params(1),
        name="nsa_compress",
    )(kin, vin, w1, pos, b1, w2, b2, bd, gain)


def _nsa_cmp_kernel(q_ref, kc_ref, vc_ref, ov_ref, ocmp_ref, sel_ref, *, n_cmp, topn):
    i = pl.program_id(1)
    scale = HEAD_DIM ** -0.5
    n_pad = kc_ref.shape[1]
    lane = lax.broadcasted_iota(I32, (TILE, LANES), 1)
    pos = i * TILE + lax.broadcasted_iota(I32, (TILE, 1), 0)
    ncol = lax.broadcasted_iota(I32, (TILE, n_pad), 1)
    visible = (ncol * CMP_STRIDE + (CMP_LEN - 1) <= pos) & (ncol < n_cmp)
    kc = kc_ref[0]
    vc = vc_ref[0]
    ov = ov_ref[...]
    imp = jnp.zeros((TILE, LANES), F32)
    for p in range(B_PAIRS):
        q_pair = q_ref[0, :, p * LANES:(p + 1) * LANES]
        o_pair = jnp.zeros((TILE, LANES), F32)
        for half in range(2):
            in_half = (lane // HEAD_DIM) == half
            q = jnp.where(in_half, q_pair, jnp.zeros_like(q_pair))
            s = jnp.where(visible, _dot_nt(q, kc) * scale, NEG_INF)
            e = jnp.where(visible, jnp.exp(s - jnp.max(s, axis=-1, keepdims=True)), 0.0)
            tot = jnp.sum(e, axis=-1, keepdims=True)
            prob = jnp.where(tot > 0.0, e / tot, 0.0)
            o_pair = jnp.where(in_half, _dot(prob.astype(BF16), vc), o_pair)
            imp = imp + jnp.where(in_half, _split_dot(prob, ov), 0.0)
        ocmp_ref[0, :, p * LANES:(p + 1) * LANES] = o_pair.astype(BF16)

    j = lane % HEAD_DIM
    own = pos // SLC_BLOCK
    forced = (j == 0) | (j == own) | (j == own - 1)
    imp = jnp.where(forced, BIG, imp)
    imp = jnp.where(j <= own, imp, NEG_INF)
    low = lane < HEAD_DIM
    rank = jnp.zeros((TILE, LANES), F32)
    for k in range(HEAD_DIM):
        other = jnp.where(low, imp[:, k:k + 1], imp[:, HEAD_DIM + k:HEAD_DIM + k + 1])
        ahead = jnp.where(other > imp, 1.0, jnp.where(other == imp, jnp.where(k < j, 1.0, 0.0), 0.0))
        rank = rank + ahead
    keep = jnp.where(rank < float(topn), jnp.where(imp > 0.5 * NEG_INF, 1.0, 0.0), 0.0)
    sel_ref[0] = keep.astype(BF16)


def _nsa_cmp(qk, kc, vc, ov, batch, seq):
    n_pad = kc.shape[1]
    n_cmp = (seq - CMP_LEN) // CMP_STRIDE + 1
    topn = min(SLC_TOPN, seq // SLC_BLOCK)
    kernel = functools.partial(_nsa_cmp_kernel, n_cmp=n_cmp, topn=topn)
    return pl.pallas_call(
        kernel,
        grid=(batch, seq // TILE),
        in_specs=[
            pl.BlockSpec((1, TILE, B_WIDTH), lambda b, i: (b, i, QK_BQ * LANES // B_WIDTH)),
            pl.BlockSpec((1, n_pad, LANES), lambda b, i: (b, 0, 0)),
            pl.BlockSpec((1, n_pad, LANES), lambda b, i: (b, 0, 0)),
            _const_spec((n_pad, LANES)),
        ],
        out_specs=[
            pl.BlockSpec((1, TILE, B_WIDTH), lambda b, i: (b, i, 0)),
            pl.BlockSpec((1, TILE, LANES), lambda b, i: (b, i, 0)),
        ],
        out_shape=[
            jax.ShapeDtypeStruct((batch, seq, B_WIDTH), BF16),
            jax.ShapeDtypeStruct((batch, seq, LANES), BF16),
        ],
        compiler_params=_params(2),
        name="nsa_cmp_select",
    )(qk, kc, vc, ov)


def _nsa_attn_kernel(q_ref, ks_ref, vs_ref, kw_ref, vw_ref, sel_ref, ocmp_ref, gate_ref, eg_ref,
                     o_ref, m_ref, l_ref, acc_ref):
    i = pl.program_id(2)
    scale = HEAD_DIM ** -0.5
    blocks_per_tile = TILE // SLC_BLOCK
    q_pair = q_ref[0]
    sel_b = sel_ref[0]
    lane = lax.broadcasted_iota(I32, (TILE, LANES), 1)
    row = lax.broadcasted_iota(I32, (TILE, TILE), 0)
    col = lax.broadcasted_iota(I32, (TILE, TILE), 1)
    causal = col <= row
    ex_row = lax.broadcasted_iota(I32, (LANES, TILE), 0)
    ex_col = lax.broadcasted_iota(I32, (LANES, TILE), 1) // SLC_BLOCK
    own = pl.multiple_of(i * TILE, TILE)
    o_slc = jnp.zeros((TILE, LANES), F32)
    o_win = jnp.zeros((TILE, LANES), F32)
    for half in range(2):
        in_half = (lane // HEAD_DIM) == half
        q = jnp.where(in_half, q_pair, jnp.zeros_like(q_pair))

        def chosen(n):
            onehot = jnp.where(ex_row == half * HEAD_DIM + n * blocks_per_tile + ex_col, 1.0, 0.0)
            return _dot(sel_b, onehot.astype(BF16)) > 0.5

        _online_init(m_ref, l_ref, acc_ref)
        s = _dot_nt(q, ks_ref[0, pl.ds(own, TILE), :]) * scale
        _online_update(s, chosen(i) & causal, vs_ref[0, pl.ds(own, TILE), :], m_ref, l_ref, acc_ref)

        def past_tile(n, carry):
            start = pl.multiple_of(n * TILE, TILE)
            s = _dot_nt(q, ks_ref[0, pl.ds(start, TILE), :]) * scale
            _online_update(s, chosen(n), vs_ref[0, pl.ds(start, TILE), :], m_ref, l_ref, acc_ref)
            return carry

        lax.fori_loop(0, i, past_tile, 0)
        o_slc = jnp.where(in_half, acc_ref[...] / l_ref[...], o_slc)

        _online_init(m_ref, l_ref, acc_ref)
        s = _dot_nt(q, kw_ref[0, pl.ds(own, TILE), :]) * scale
        _online_update(s, causal, vw_ref[0, pl.ds(own, TILE), :], m_ref, l_ref, acc_ref)

        @pl.when(i >= 1)
        def _():
            start = pl.multiple_of((i - 1) * TILE, TILE)
            s = _dot_nt(q, kw_ref[0, pl.ds(start, TILE), :]) * scale
            _online_update(s, col >= 0, vw_ref[0, pl.ds(start, TILE), :], m_ref, l_ref, acc_ref)

        @pl.when(i >= 2)
        def _():
            start = pl.multiple_of((i - 2) * TILE, TILE)
            s = _dot_nt(q, kw_ref[0, pl.ds(start, TILE), :]) * scale
            _online_update(s, col > row, vw_ref[0, pl.ds(start, TILE), :], m_ref, l_ref, acc_ref)

        o_win = jnp.where(in_half, acc_ref[...] / l_ref[...], o_win)

    g = _split_dot(gate_ref[0], eg_ref[0])
    out = (g[:, 0:LANES] * ocmp_ref[0].astype(F32) + g[:, LANES:2 * LANES] * o_slc
           + g[:, 2 * LANES:3 * LANES] * o_win)
    o_ref[0] = out.astype(BF16)


def _nsa_attn(qk, vv, sel, ocmp, gate, eg, batch, seq):
    assert WINDOW == 2 * TILE
    kv = lambda t: pl.BlockSpec((1, seq, LANES), lambda b, p, i: (b, 0, t))
    return pl.pallas_call(
        _nsa_attn_kernel,
        grid=(batch, B_PAIRS, seq // TILE),
        in_specs=[
            pl.BlockSpec((1, TILE, LANES), lambda b, p, i: (b, i, QK_BQ + p)),
            kv(QK_BKS),
            pl.BlockSpec((1, seq, LANES), lambda b, p, i: (b, 0, VV_BVS)),
            kv(QK_BKW),
            pl.BlockSpec((1, seq, LANES), lambda b, p, i: (b, 0, VV_BVW)),
            pl.BlockSpec((1, TILE, LANES), lambda b, p, i: (b, i, 0)),
            pl.BlockSpec((1, TILE, LANES), lambda b, p, i: (b, i, p)),
            pl.BlockSpec((1, TILE, LANES), lambda b, p, i: (b, i, 0)),
            pl.BlockSpec((1, LANES, 3 * LANES), lambda b, p, i: (p, 0, 0)),
        ],
        out_specs=pl.BlockSpec((1, TILE, LANES), lambda b, p, i: (b, i, p)),
        out_shape=jax.ShapeDtypeStruct((batch, seq, B_WIDTH), BF16),
        scratch_shapes=[
            pltpu.VMEM((TILE, 1), F32),
            pltpu.VMEM((TILE, 1), F32),
            pltpu.VMEM((TILE, LANES), F32),
        ],
        compiler_params=_params(3),
        name="nsa_select_window",
    )(qk, qk, vv, qk, vv, sel, ocmp, gate, eg)


def _top_rows(v, row_f, count, fill):
    vals, rows = [], []
    for _ in range(count):
        best = jnp.max(v, axis=0, keepdims=True)
        idx = jnp.min(jnp.where(v == best, row_f, fill), axis=0, keepdims=True)
        vals.append(best)
        rows.append(idx)
        v = jnp.where(row_f == idx, -jnp.inf, v)
    return vals, rows


def _merge_peer_kernel(x_ref, ya_ref, yb_ref, mg_ref, wa_ref, wb_ref, wo_ref, g2_ref, wq_ref,
                       k1h_ref, k1l_ref, k2h_ref, k2l_ref,
                       x1_ref, hn_ref, idx_ref, gt_ref, q_scr, ex_scr, gt_scr):
    ua = _dot(ya_ref[...], wa_ref[...])
    ub = _dot(yb_ref[...], wb_ref[...])
    merged = (mg_ref[:, 0:D_MODEL].astype(F32) * ua + mg_ref[:, D_MODEL:2 * D_MODEL].astype(F32) * ub)
    x1 = x_ref[...] + _dot(merged.astype(BF16), wo_ref[...])
    x1_ref[...] = x1
    hn = x1 * lax.rsqrt(jnp.mean(x1 * x1, axis=-1, keepdims=True) + NORM_EPS) * g2_ref[...]
    hn_ref[...] = hn
    q_scr[...] = _dot(hn.astype(BF16), wq_ref[...])

    key_row = lax.broadcasted_iota(I32, (PEER_NKEYS, TILE), 0).astype(F32)
    n_cand = PEER_TOPK * PEER_TOPK
    cand_row = lax.broadcasted_iota(I32, (n_cand, TILE), 0).astype(F32)

    def per_head(h, carry):
        base = pl.multiple_of(h * PEER_QDIM, PEER_QDIM)
        q1h, q1l = _split(q_scr[:, pl.ds(base, PEER_HALF)])
        q2h, q2l = _split(q_scr[:, pl.ds(base + PEER_HALF, PEER_HALF)])
        k1h, k2h = k1h_ref[...], k2h_ref[...]
        s1 = _dot_nt(k1h, q1h) + _dot_nt(k1h, q1l) + _dot_nt(k1l_ref[...], q1h)
        s2 = _dot_nt(k2h, q2h) + _dot_nt(k2h, q2l) + _dot_nt(k2l_ref[...], q2h)
        v1, i1 = _top_rows(s1, key_row, PEER_TOPK, float(PEER_NKEYS))
        v2, i2 = _top_rows(s2, key_row, PEER_TOPK, float(PEER_NKEYS))
        v2m = jnp.concatenate(v2, axis=0)
        i2m = jnp.concatenate(i2, axis=0)
        cand = jnp.concatenate([v1[a] + v2m for a in range(PEER_TOPK)], axis=0)
        cidx = jnp.concatenate([i1[a] * float(PEER_NKEYS) + i2m for a in range(PEER_TOPK)], axis=0)
        vals, experts = [], []
        v = cand
        for _ in range(PEER_TOPK):
            best = jnp.max(v, axis=0, keepdims=True)
            at = jnp.min(jnp.where(v == best, cand_row, float(n_cand)), axis=0, keepdims=True)
            hit = cand_row == at
            experts.append(jnp.max(jnp.where(hit, cidx, -1.0), axis=0, keepdims=True))
            vals.append(best)
            v = jnp.where(hit, -jnp.inf, v)
        vals = jnp.concatenate(vals, axis=0)
        e = jnp.exp(vals - vals[0:1])
        out_row = pl.multiple_of(h * PEER_TOPK, PEER_TOPK)
        gt_scr[pl.ds(out_row, PEER_TOPK), :] = e / jnp.sum(e, axis=0, keepdims=True)
        ex_scr[pl.ds(out_row, PEER_TOPK), :] = jnp.concatenate(experts, axis=0)
        return carry

    lax.fori_loop(0, PEER_HEADS, per_head, 0)
    gt_ref[...] = jnp.transpose(gt_scr[...])
    idx_ref[...] = (jnp.transpose(ex_scr[...]) * float(HALF_ROWS)).astype(I32)


def _merge_peer(x2d, ya, yb, mg, wa, wb, wo, g2, wq, k1h, k1l, k2h, k2l):
    tokens = x2d.shape[0]
    row = lambda i: (i, 0)
    return pl.pallas_call(
        _merge_peer_kernel,
        grid=(tokens // TILE,),
        in_specs=[
            pl.BlockSpec((TILE, D_MODEL), row),
            pl.BlockSpec((TILE, A_WIDTH), row),
            pl.BlockSpec((TILE, B_WIDTH), row),
            pl.BlockSpec((TILE, MG_COLS), row),
            _const_spec((A_WIDTH, D_MODEL)),
            _const_spec((B_WIDTH, D_MODEL)),
            _const_spec((D_MODEL, D_MODEL)),
            _const_spec((1, D_MODEL)),
            _const_spec((D_MODEL, PEER_HEADS * PEER_QDIM)),
            _const_spec((PEER_NKEYS, PEER_HALF)),
            _const_spec((PEER_NKEYS, PEER_HALF)),
            _const_spec((PEER_NKEYS, PEER_HALF)),
            _const_spec((PEER_NKEYS, PEER_HALF)),
        ],
        out_specs=[
            pl.BlockSpec((TILE, D_MODEL), row),
            pl.BlockSpec((TILE, D_MODEL), row),
            pl.BlockSpec((TILE, PEER_SLOTS), row),
            pl.BlockSpec((TILE, PEER_SLOTS), row),
        ],
        out_shape=[
            jax.ShapeDtypeStruct((tokens, D_MODEL), F32),
            jax.ShapeDtypeStruct((tokens, D_MODEL), F32),
            jax.ShapeDtypeStruct((tokens, PEER_SLOTS), I32),
            jax.ShapeDtypeStruct((tokens, PEER_SLOTS), F32),
        ],
        scratch_shapes=[pltpu.VMEM((TILE, PEER_HEADS * PEER_QDIM), F32),
                        pltpu.VMEM((PEER_SLOTS, TILE), F32),
                        pltpu.VMEM((PEER_SLOTS, TILE), F32)],
        compiler_params=_params(1),
        name="merge_peer_topk",
    )(x2d, ya, yb, mg, wa, wb, wo, g2, wq, k1h, k1l, k2h, k2l)


HALF_ROWS = SUBLANES // 2
HI_MASK = -65536
ROW_MASK = 65535
PEER_UNROLL = 2


def _unpack(words):
    hi = pltpu.bitcast(words & HI_MASK, F32)
    lo = pltpu.bitcast(words << 16, F32)
    return hi, lo


def _expert_row(tab_ref, row):
    return tab_ref[pl.ds(pl.multiple_of(row, HALF_ROWS), HALF_ROWS), :]


def _fold_rows(v, shift):
    return v + pltpu.roll(v, shift, axis=0)


def _peer_u_kernel(row_ref, rowv_ref, hn_ref, gt_ref, tab_ref, route_ref):
    lane = lax.broadcasted_iota(I32, (SUBLANES, LANES), 1)
    sub = lax.broadcasted_iota(I32, (SUBLANES, LANES), 0)
    keep_hi_pair = (sub % 4) >= 2
    odd = (sub % 2) == 1
    groups = PEER_SLOTS // SUBLANES

    def one_token(t, accs):
        x = hn_ref[t]
        xa = jnp.concatenate([x[0:HALF_ROWS], x[0:HALF_ROWS]], axis=0)
        xb = jnp.concatenate([x[HALF_ROWS:], x[HALF_ROWS:]], axis=0)

        def pair(ja, jb):
            words = jnp.concatenate([_expert_row(tab_ref, row_ref[t, ja]),
                                     _expert_row(tab_ref, row_ref[t, jb])], axis=0)
            hi, lo = _unpack(words)
            return hi * xa + lo * xb

        new = []
        for g in range(groups):
            j = g * SUBLANES
            quads = []
            for (a, b, c, d) in ((j + 3, j + 7, j + 1, j + 5), (j + 2, j + 6, j, j + 4)):
                w_ab = _fold_rows(pair(a, b), 2)
                w_cd = _fold_rows(pair(c, d), 2)
                z = jnp.where(keep_hi_pair, w_ab, pltpu.roll(w_cd, 6, axis=0))
                quads.append(_fold_rows(z, 1))
            folded = jnp.where(odd, quads[0], pltpu.roll(quads[1], 7, axis=0))
            total = jnp.sum(folded, axis=1, keepdims=True)
            new.append(jnp.where(lane == t, total, accs[g]))
        return tuple(new)

    def step(i, accs):
        for k in range(PEER_UNROLL):
            accs = one_token(i * PEER_UNROLL + k, accs)
        return accs

    init = tuple(jnp.zeros((SUBLANES, LANES), F32) for _ in range(groups))
    accs = lax.fori_loop(0, PEER_TB // PEER_UNROLL, step, init)
    a = jnp.transpose(jnp.concatenate(accs, axis=0))
    w = (_gelu_tanh(a) * gt_ref[...]).astype(BF16).astype(F32)
    route_ref[...] = (pltpu.bitcast(w, I32) & HI_MASK) | rowv_ref[...]


def _peer_u(rows, hn3, gt, tab):
    tokens = hn3.shape[0]
    blk = lambda i: (i, 0)
    return pl.pallas_call(
        _peer_u_kernel,
        grid=(tokens // PEER_TB,),
        in_specs=[
            pl.BlockSpec((PEER_TB, PEER_SLOTS), blk, memory_space=pltpu.SMEM),
            pl.BlockSpec((PEER_TB, PEER_SLOTS), blk),
            pl.BlockSpec((PEER_TB, SUBLANES, LANES), lambda i: (i, 0, 0)),
            pl.BlockSpec((PEER_TB, PEER_SLOTS), blk),
            _const_spec((PEER_N * HALF_ROWS, LANES)),
        ],
        out_specs=pl.BlockSpec((PEER_TB, PEER_SLOTS), blk),
        out_shape=jax.ShapeDtypeStruct((tokens, PEER_SLOTS), I32),
        compiler_params=_params(1),
        name="peer_expert_in",
    )(rows, rows, hn3, gt, tab)


def _peer_v_kernel(route_ref, x1_ref, tab_ref, o_ref):
    def per_token(t, carry):
        acc_hi = jnp.zeros((HALF_ROWS, LANES), F32)
        acc_lo = jnp.zeros((HALF_ROWS, LANES), F32)
        for j in range(PEER_SLOTS):
            word = route_ref[t, j]
            hi, lo = _unpack(_expert_row(tab_ref, word & ROW_MASK))
            w = pltpu.bitcast(jnp.full((HALF_ROWS, LANES), word, I32) & HI_MASK, F32)
            acc_hi = acc_hi + w * hi
            acc_lo = acc_lo + w * lo
        o_ref[t] = x1_ref[t] + jnp.concatenate([acc_hi, acc_lo], axis=0)
        return carry

    lax.fori_loop(0, PEER_TB, per_token, 0)


def _peer_v(route, x13, tab):
    tokens = x13.shape[0]
    return pl.pallas_call(
        _peer_v_kernel,
        grid=(tokens // PEER_TB,),
        in_specs=[
            pl.BlockSpec((PEER_TB, PEER_SLOTS), lambda i: (i, 0), memory_space=pltpu.SMEM),
            pl.BlockSpec((PEER_TB, SUBLANES, LANES), lambda i: (i, 0, 0)),
            _const_spec((PEER_N * HALF_ROWS, LANES)),
        ],
        out_specs=pl.BlockSpec((PEER_TB, SUBLANES, LANES), lambda i: (i, 0, 0)),
        out_shape=jax.ShapeDtypeStruct((tokens, SUBLANES, LANES), F32),
        compiler_params=_params(1),
        name="peer_expert_out",
    )(route, x13, tab)


def _pack_table(t):
    bits = lax.bitcast_convert_type(t.astype(BF16), jnp.uint16).astype(jnp.uint32)
    half = D_MODEL // 2
    words = (bits[:, :half] << 16) | bits[:, half:]
    return lax.bitcast_convert_type(words, I32).reshape(t.shape[0] * HALF_ROWS, LANES)


def _rope_tables(seq):
    inv = ROPE_THETA ** (-jnp.arange(0, ROT_DIM, 2, dtype=F32) / ROT_DIM)
    ang = jnp.arange(seq, dtype=F32)[:, None] * inv[None, :]
    d = np.arange(LANES) % HEAD_DIM
    cos = jnp.where(d[None, :] < ROT_DIM, jnp.cos(ang)[:, d % ROT_HALF], 1.0)
    sin = jnp.where(d[None, :] < ROT_DIM, jnp.sin(ang)[:, d % ROT_HALF], 0.0)
    return cos.astype(F32), sin.astype(F32)


def _compress_weights(w1, pos):
    out_w, out_p = [], []
    for part in range(2):
        wpart = w1[part * CMP_STRIDE * HEAD_DIM:(part + 1) * CMP_STRIDE * HEAD_DIM]
        wpart = wpart.reshape(CMP_STRIDE, HEAD_DIM, CMP_HIDDEN)
        full = jnp.zeros((CMP_STRIDE, B_KV_HEADS, HEAD_DIM, B_KV_HEADS, CMP_HIDDEN), F32)
        for g in range(B_KV_HEADS):
            full = full.at[:, g, :, g, :].set(wpart)
        out_w.append(full.reshape(CMP_STRIDE * LANES, B_KV_HEADS * CMP_HIDDEN))
        ppart = pos[part * CMP_STRIDE:(part + 1) * CMP_STRIDE]
        prow = jnp.tile(ppart[:, None, :], (1, B_KV_HEADS, 1)).reshape(1, CMP_STRIDE * LANES)
        out_p.append(jnp.tile(prow, (SUBLANES, 1)))
    return jnp.stack(out_w).astype(BF16), jnp.stack(out_p).astype(BF16)


def kernel(x, norm1_g, w_in, b_merge, a_q_g, a_k_g, b_q_g, b_kc_g, b_ks_g, b_kw_g, cmp_pos_k,
           cmp_k_w1, cmp_k_b1, cmp_k_w2, cmp_k_b2, cmp_pos_v, cmp_v_w1, cmp_v_b1, cmp_v_w2,
           cmp_v_b2, w_up_a, w_up_b, w_out, norm2_g, peer_wq, peer_k1, peer_k2, peer_u, peer_v):
    batch, seq, _ = x.shape
    assert seq % TILE == 0 and seq // SLC_BLOCK <= HEAD_DIM
    tokens = batch * seq
    l = 0
    x2d = x.reshape(tokens, D_MODEL)

    w = w_in[l]
    sizes = (A_WIDTH, A_WIDTH, A_WIDTH, B_WIDTH) + (B_KV_WIDTH,) * 6 + (3 * B_HEADS, MG_COLS)
    offs = np.concatenate([[0], np.cumsum(sizes)])
    seg = lambda k: w[:, offs[k]:offs[k + 1]]
    aq, ak, av, bq, bkc, bvc, bks, bvs, bkw, bvw, bgate, mgate = (seg(k) for k in range(12))
    bq_perm = bq.reshape(D_MODEL, B_HEADS, HEAD_DIM)[:, np.array(B_HEAD_ORDER)].reshape(D_MODEL, B_WIDTH)
    w_rope = jnp.concatenate([aq, ak, bq_perm, bks, bkw], axis=1).astype(BF16)
    w_plain = jnp.concatenate([av, bvs, bvw], axis=1).astype(BF16)
    w_gate = jnp.pad(bgate, ((0, 0), (0, LANES - 3 * B_HEADS))).astype(BF16)
    hg = jnp.concatenate([jnp.tile(a_q_g[l], A_HEADS), jnp.tile(a_k_g[l], A_HEADS),
                          jnp.tile(b_q_g[l], B_HEADS), jnp.tile(b_ks_g[l], B_KV_HEADS),
                          jnp.tile(b_kw_g[l], B_KV_HEADS)])[None, :]
    cos_t, sin_t = _rope_tables(seq)
    bd = jnp.asarray(np.kron(np.eye(LANES // HEAD_DIM), np.full((HEAD_DIM, HEAD_DIM), 1.0 / HEAD_DIM)),
                     BF16)

    qk, vv, kcin, vcin, gate, mg, kmean = _inproj(
        x2d, norm1_g[l][None, :], w_rope, w_plain, bkc.astype(BF16), bvc.astype(BF16), w_gate,
        mgate.astype(BF16), hg, cos_t, sin_t, bd, b_merge[l].reshape(1, MG_COLS), seq)
    qk = qk.reshape(batch, seq, ROPE_COLS)
    vv = vv.reshape(batch, seq, PLAIN_COLS)

    n_blk = seq // A_BLOCK
    kmean = jnp.pad(kmean.reshape(batch, n_blk, A_WIDTH), ((0, 0), (0, LANES - n_blk), (0, 0)))
    km_hi = kmean.astype(BF16)
    km_lo = (kmean - km_hi.astype(F32)).astype(BF16)
    ya = _moba(qk, vv, km_hi, km_lo, batch, seq)

    n_rows = seq // CMP_STRIDE
    wk1, pk = _compress_weights(cmp_k_w1[l], cmp_pos_k[l])
    wv1, pv = _compress_weights(cmp_v_w1[l], cmp_pos_v[l])
    blockdiag = lambda m: jnp.kron(jnp.eye(B_KV_HEADS, dtype=F32), m)
    w2 = jnp.stack([blockdiag(cmp_k_w2[l]), blockdiag(cmp_v_w2[l])]).astype(BF16)
    b1 = jnp.stack([jnp.tile(cmp_k_b1[l], B_KV_HEADS), jnp.tile(cmp_v_b1[l], B_KV_HEADS)])[:, None, :]
    b2 = jnp.stack([jnp.tile(cmp_k_b2[l], B_KV_HEADS), jnp.tile(cmp_v_b2[l], B_KV_HEADS)])[:, None, :]
    kc, vc = _compress(kcin.reshape(batch, n_rows, CMP_STRIDE * LANES),
                       vcin.reshape(batch, n_rows, CMP_STRIDE * LANES),
                       jnp.stack([wk1, wv1]), jnp.stack([pk, pv]), b1, w2, b2, bd,
                       jnp.tile(b_kc_g[l], B_KV_HEADS)[None, :], batch)

    n_cmp = (seq - CMP_LEN) // CMP_STRIDE + 1
    n_slc = seq // SLC_BLOCK
    ci = np.arange(n_rows)[:, None]
    sj = np.arange(HEAD_DIM)[None, :]
    ov = ((ci * CMP_STRIDE < (sj + 1) * SLC_BLOCK) & (ci * CMP_STRIDE + CMP_LEN > sj * SLC_BLOCK)
          & (ci < n_cmp) & (sj < n_slc)).astype(np.float32)
    ov = jnp.asarray(np.concatenate([ov, ov], axis=1), BF16)
    ocmp, sel = _nsa_cmp(qk, kc, vc, ov, batch, seq)

    eg = np.zeros((B_PAIRS, LANES, 3 * LANES), np.float32)
    for p in range(B_PAIRS):
        for half in range(2):
            head = B_HEAD_ORDER[2 * p + half]
            for c in range(3):
                eg[p, head * 3 + c, c * LANES + half * HEAD_DIM:c * LANES + (half + 1) * HEAD_DIM] = 1.0
    yb = _nsa_attn(qk, vv, sel, ocmp, gate.reshape(batch, seq, LANES), jnp.asarray(eg, BF16),
                   batch, seq)

    wb_perm = w_up_b[l].reshape(B_HEADS, HEAD_DIM, D_MODEL)[np.array(B_HEAD_ORDER)].reshape(
        B_WIDTH, D_MODEL)
    k1h, k1l = _split(peer_k1[l])
    k2h, k2l = _split(peer_k2[l])
    x1, hn, rows, gt = _merge_peer(
        x2d, ya.reshape(tokens, A_WIDTH), yb.reshape(tokens, B_WIDTH), mg,
        w_up_a[l].astype(BF16), wb_perm.astype(BF16), w_out[l].astype(BF16),
        norm2_g[l][None, :], peer_wq[l].astype(BF16), k1h, k1l, k2h, k2l)

    route = _peer_u(rows, hn.reshape(tokens, SUBLANES, LANES), gt, _pack_table(peer_u[l]))
    out = _peer_v(route, x1.reshape(tokens, SUBLANES, LANES), _pack_table(peer_v[l]))
    return out.reshape(batch, seq, D_MODEL)
```

```python
import functools
import math

import jax
import jax.numpy as jnp
import numpy as np
from jax import lax
from jax.experimental import pallas as pl
from jax.experimental.pallas import tpu as pltpu

D_MODEL = 1024
HEAD_DIM = 64
ROT_DIM = HEAD_DIM // 4
ROT_HALF = ROT_DIM // 2
ROPE_THETA = 500000.0
NORM_EPS = 1e-6
NEG_INF = -1e30
BIG = 1e30

A_HEADS = 8
A_BLOCK = 256
A_TOPK = 3

B_HEADS = 8
B_KV_HEADS = 2
B_GROUP = B_HEADS // B_KV_HEADS
CMP_LEN = 32
CMP_STRIDE = 16
CMP_HIDDEN = 256
SLC_BLOCK = 64
SLC_TOPN = 16
WINDOW = 512

PEER_HEADS = 8
PEER_NKEYS = 128
PEER_N = PEER_NKEYS * PEER_NKEYS
PEER_QDIM = 256
PEER_HALF = PEER_QDIM // 2
PEER_TOPK = 16
PEER_SLOTS = PEER_HEADS * PEER_TOPK

A_WIDTH = A_HEADS * HEAD_DIM
B_WIDTH = B_HEADS * HEAD_DIM
B_KV_WIDTH = B_KV_HEADS * HEAD_DIM
N_BRANCH = 2

LANES = 128
SUBLANES = 8
VMEM_LIMIT_BYTES = 56 * 1024 * 1024

ROPE_COLS = 2 * A_WIDTH + B_WIDTH + 2 * B_KV_WIDTH
QK_AQ, QK_AK, QK_BQ, QK_BKS, QK_BKW = 0, 4, 8, 12, 13
PLAIN_COLS = A_WIDTH + 2 * B_KV_WIDTH
VV_AV, VV_BVS, VV_BVW = 0, 4, 5
MG_COLS = N_BRANCH * D_MODEL
B_PAIRS = B_WIDTH // LANES
B_HEAD_ORDER = tuple(h for p in range(B_PAIRS) for h in (p, B_GROUP + p))

TILE = 256
PEER_TB = 128
F32 = jnp.float32
BF16 = jnp.bfloat16
I32 = jnp.int32


def _const_spec(shape):
    n = len(shape)
    return pl.BlockSpec(shape, lambda *_: (0,) * n, pipeline_mode=pl.Buffered(1))


def _params(n_axes):
    return pltpu.CompilerParams(dimension_semantics=("arbitrary",) * n_axes,
                                vmem_limit_bytes=VMEM_LIMIT_BYTES)


def _dot(a, b):
    return jnp.dot(a, b, preferred_element_type=F32)


def _dot_nt(a, b):
    return lax.dot_general(a, b, (((1,), (1,)), ((), ())), preferred_element_type=F32)


def _split(a_f32):
    hi = a_f32.astype(BF16)
    lo = (a_f32 - hi.astype(F32)).astype(BF16)
    return hi, lo


def _split_dot(a_f32, b_bf16):
    hi, lo = _split(a_f32)
    return _dot(hi, b_bf16) + _dot(lo, b_bf16)


def _gelu_tanh(x):
    return 0.5 * x * (1.0 + jnp.tanh(math.sqrt(2.0 / math.pi) * (x + 0.044715 * (x * x * x))))


def _group_rmsnorm(y, bd, gain):
    ms = _split_dot(y * y, bd)
    return y * lax.rsqrt(ms + NORM_EPS) * gain


def _inproj_kernel(x_ref, g1_ref, w_rope_ref, w_plain_ref, w_kc_ref, w_vc_ref, w_gate_ref,
                   w_mg_ref, hg_ref, cos_ref, sin_ref, bd_ref, bm_ref,
                   qk_ref, vvt_ref, kcin_ref, vcin_ref, gate_ref, mg_ref, kmean_ref):
    x = x_ref[...]
    h = x * lax.rsqrt(jnp.mean(x * x, axis=-1, keepdims=True) + NORM_EPS) * g1_ref[...]
    hb = h.astype(BF16)

    vvt_ref[0] = _dot_nt(w_plain_ref[...], hb).astype(BF16)
    kcin_ref[...] = _dot(hb, w_kc_ref[...]).astype(BF16)
    vcin_ref[...] = _dot(hb, w_vc_ref[...]).astype(BF16)
    gate_ref[...] = jax.nn.sigmoid(_dot(hb, w_gate_ref[...]))
    mg_ref[...] = jax.nn.sigmoid(_dot(hb, w_mg_ref[...]) + bm_ref[...]).astype(BF16)

    cos = cos_ref[...]
    sin = sin_ref[...]
    bd = bd_ref[...]
    lane = lax.broadcasted_iota(I32, (TILE, LANES), 1)
    first = (lane % HEAD_DIM) < ROT_HALF
    for t in range(ROPE_COLS // LANES):
        sl = slice(t * LANES, (t + 1) * LANES)
        y = _dot(hb, w_rope_ref[:, sl])
        yn = _group_rmsnorm(y, bd, hg_ref[:, sl])
        up = pltpu.roll(yn, LANES - ROT_HALF, axis=1)
        dn = pltpu.roll(yn, ROT_HALF, axis=1)
        out = yn * cos + jnp.where(first, -up, dn) * sin
        qk_ref[:, sl] = out.astype(BF16)
        if QK_AK <= t < QK_AK + A_WIDTH // LANES:
            c = t - QK_AK
            kmean_ref[0, :, c * LANES:(c + 1) * LANES] = jnp.mean(out, axis=0, keepdims=True)


def _inproj(x2d, g1, w_rope, w_plain, w_kc, w_vc, w_gate, w_mg, hg, cos_t, sin_t, bd, bm, seq):
    tokens = x2d.shape[0]
    nt = tokens // TILE
    tiles_per_seq = seq // TILE
    row = lambda i: (i, 0)
    pos = lambda i: (i % tiles_per_seq, 0)
    return pl.pallas_call(
        _inproj_kernel,
        grid=(nt,),
        in_specs=[
            pl.BlockSpec((TILE, D_MODEL), row),
            _const_spec((1, D_MODEL)),
            _const_spec((D_MODEL, ROPE_COLS)),
            _const_spec((PLAIN_COLS, D_MODEL)),
            _const_spec((D_MODEL, LANES)),
            _const_spec((D_MODEL, LANES)),
            _const_spec((D_MODEL, LANES)),
            _const_spec((D_MODEL, MG_COLS)),
            _const_spec((1, ROPE_COLS)),
            pl.BlockSpec((TILE, LANES), pos),
            pl.BlockSpec((TILE, LANES), pos),
            _const_spec((LANES, LANES)),
            _const_spec((1, MG_COLS)),
        ],
        out_specs=[
            pl.BlockSpec((TILE, ROPE_COLS), row),
            pl.BlockSpec((1, PLAIN_COLS, TILE), lambda i: (i // tiles_per_seq, 0, i % tiles_per_seq)),
            pl.BlockSpec((TILE, LANES), row),
            pl.BlockSpec((TILE, LANES), row),
            pl.BlockSpec((TILE, LANES), row),
            pl.BlockSpec((TILE, MG_COLS), row),
            pl.BlockSpec((1, 1, A_WIDTH), lambda i: (i, 0, 0)),
        ],
        out_shape=[
            jax.ShapeDtypeStruct((tokens, ROPE_COLS), BF16),
            jax.ShapeDtypeStruct((tokens // seq, PLAIN_COLS, seq), BF16),
            jax.ShapeDtypeStruct((tokens, LANES), BF16),
            jax.ShapeDtypeStruct((tokens, LANES), BF16),
            jax.ShapeDtypeStruct((tokens, LANES), F32),
            jax.ShapeDtypeStruct((tokens, MG_COLS), BF16),
            jax.ShapeDtypeStruct((nt, 1, A_WIDTH), F32),
        ],
        compiler_params=_params(1),
        name="inproj",
    )(x2d, g1, w_rope, w_plain, w_kc, w_vc, w_gate, w_mg, hg, cos_t, sin_t, bd, bm)


LOG2E = math.log2(math.e)


def _scaled_halves(q_pair):
    lane = lax.broadcasted_iota(I32, (TILE, LANES), 1)
    q = q_pair.astype(F32)
    c = HEAD_DIM ** -0.5 * LOG2E
    plain = [jnp.where((lane // HEAD_DIM) == h, q, 0.0).astype(BF16) for h in range(2)]
    scaled = [jnp.where((lane // HEAD_DIM) == h, q * c, 0.0).astype(BF16) for h in range(2)]
    return plain, scaled


def _col_max(s_t):
    part = jnp.max(s_t.reshape(-1, SUBLANES, s_t.shape[-1]), axis=0)
    return jnp.max(part, axis=0, keepdims=True)


def _flash_step(s_t, shift_bias, v_t, m, l8, acc_ref, rows):
    m_new = jnp.maximum(m, _col_max(s_t) + shift_bias)
    p_t = jnp.exp2(s_t - (m_new - shift_bias))
    alpha = jnp.exp2(m - m_new)
    l8 = alpha * l8 + jnp.sum(p_t.reshape(-1, SUBLANES, p_t.shape[-1]), axis=0)
    acc_ref[rows, :] = alpha * acc_ref[rows, :] + _dot(v_t, p_t.astype(BF16))
    return m_new, l8


def _flash_finish(l8s, acc_ref):
    inv = [1.0 / jnp.sum(l8, axis=0, keepdims=True) for l8 in l8s]
    out_t = jnp.concatenate([acc_ref[0:HEAD_DIM, :] * inv[0], acc_ref[HEAD_DIM:, :] * inv[1]], axis=0)
    return jnp.transpose(out_t)


def _moba_kernel(q_ref, k_ref, vt_ref, kmh_ref, kml_ref, o_ref, acc_ref, *, topk, gate_rows):
    i = pl.program_id(1)
    pairs = A_WIDTH // LANES
    heads = [(p, h) for p in range(pairs) for h in range(2)]
    key_i = lax.broadcasted_iota(I32, (TILE, TILE), 0)
    qry_i = lax.broadcasted_iota(I32, (TILE, TILE), 1)
    causal = key_i <= qry_i
    blk = lax.broadcasted_iota(I32, (gate_rows, TILE), 0)
    blk_f = blk.astype(F32)
    own = pl.multiple_of(i * TILE, TILE)
    lanes_of = lambda p: slice(p * LANES, (p + 1) * LANES)
    rows_of = lambda p, h: slice((2 * p + h) * HEAD_DIM, (2 * p + h + 1) * HEAD_DIM)
    zero = jnp.zeros((1, TILE), F32)
    m0 = jnp.full((1, TILE), NEG_INF, F32)
    l0 = jnp.zeros((SUBLANES, TILE), F32)

    plain, scaled = [], []
    for p in range(pairs):
        plain_p, scaled_p = _scaled_halves(q_ref[0, :, lanes_of(p)])
        plain.extend(plain_p)
        scaled.extend(scaled_p)
    gates = [(_dot_nt(kmh_ref[0, :, lanes_of(p)], plain[c])
              + _dot_nt(kml_ref[0, :, lanes_of(p)], plain[c]))[0:gate_rows]
             for c, (p, h) in enumerate(heads)]
    own_scores = [_dot_nt(k_ref[0, pl.ds(own, TILE), lanes_of(p)], scaled[c])
                  for c, (p, h) in enumerate(heads)]

    biases, state = [], []
    for c, (p, h) in enumerate(heads):
        gate = jnp.where(blk < i, gates[c], NEG_INF)
        bias = jnp.full((gate_rows, TILE), NEG_INF, F32)
        for _ in range(topk):
            best = jnp.max(gate, axis=0, keepdims=True)
            idx = jnp.min(jnp.where(gate == best, blk_f, float(gate_rows)), axis=0, keepdims=True)
            pick = blk_f == idx
            bias = jnp.where(pick, jnp.where(best > 0.5 * NEG_INF, 0.0, bias), bias)
            gate = jnp.where(pick, NEG_INF, gate)
        biases.append(bias)
        acc_ref[rows_of(p, h), :] = jnp.zeros((HEAD_DIM, TILE), F32)
        state.extend(_flash_step(jnp.where(causal, own_scores[c], NEG_INF), zero,
                                 vt_ref[0, rows_of(p, h), pl.ds(own, TILE)], m0, l0,
                                 acc_ref, rows_of(p, h)))

    def past_block(n, carry):
        start = pl.multiple_of(n * TILE, TILE)
        out = []
        scores = [_dot_nt(k_ref[0, pl.ds(start, TILE), lanes_of(p)], scaled[c])
                  for c, (p, h) in enumerate(heads)]
        for c, (p, h) in enumerate(heads):
            b = jnp.max(jnp.where(blk == n, biases[c], NEG_INF), axis=0, keepdims=True)
            out.extend(_flash_step(scores[c], b, vt_ref[0, rows_of(p, h), pl.ds(start, TILE)],
                                   carry[2 * c], carry[2 * c + 1], acc_ref, rows_of(p, h)))
        return tuple(out)

    state = lax.fori_loop(0, i, past_block, tuple(state))
    for p in range(pairs):
        l8s = (state[4 * p + 1], state[4 * p + 3])
        o_ref[0, :, lanes_of(p)] = _flash_finish(l8s, acc_ref.at[lanes_of(p)]).astype(BF16)


def _moba(qk, vvt, kmean_hi, kmean_lo, batch, seq):
    n_blk = seq // A_BLOCK
    gate_rows = -(-n_blk // SUBLANES) * SUBLANES
    kernel = functools.partial(_moba_kernel, topk=min(A_TOPK, n_blk), gate_rows=gate_rows)
    return pl.pallas_call(
        kernel,
        grid=(batch, n_blk),
        in_specs=[
            pl.BlockSpec((1, TILE, A_WIDTH), lambda b, i: (b, i, QK_AQ * LANES // A_WIDTH)),
            pl.BlockSpec((1, seq, A_WIDTH), lambda b, i: (b, 0, QK_AK * LANES // A_WIDTH)),
            pl.BlockSpec((1, A_WIDTH, seq), lambda b, i: (b, VV_AV * LANES // A_WIDTH, 0)),
            pl.BlockSpec((1, LANES, A_WIDTH), lambda b, i: (b, 0, 0)),
            pl.BlockSpec((1, LANES, A_WIDTH), lambda b, i: (b, 0, 0)),
        ],
        out_specs=pl.BlockSpec((1, TILE, A_WIDTH), lambda b, i: (b, i, 0)),
        out_shape=jax.ShapeDtypeStruct((batch, seq, A_WIDTH), BF16),
        scratch_shapes=[pltpu.VMEM((A_WIDTH, TILE), F32)],
        compiler_params=_params(2),
        name="moba",
    )(qk, qk, vvt, kmean_hi, kmean_lo)


def _compress_kernel(kin_ref, vin_ref, w1_ref, pos_ref, b1_ref, w2_ref, b2_ref, bd_ref, g_ref,
                     kc_ref, vc_ref):
    n_rows = kin_ref.shape[1]
    for c, (in_ref, out_ref) in enumerate(((kin_ref, kc_ref), (vin_ref, vc_ref))):
        r = in_ref[0]
        first = _dot(r, w1_ref[c, 0])
        second = _dot(r, w1_ref[c, 1])
        const = (_dot(pos_ref[c, 0], w1_ref[c, 0]) + _dot(pos_ref[c, 1], w1_ref[c, 1]))[0:1]
        hid = first + pltpu.roll(second, n_rows - 1, axis=0) + const + b1_ref[c]
        out = _dot(_gelu_tanh(hid).astype(BF16), w2_ref[c]) + b2_ref[c]
        if c == 0:
            out = _group_rmsnorm(out, bd_ref[...], g_ref[...])
        out_ref[0] = out.astype(BF16)


def _compress(kin, vin, w1, pos, b1, w2, b2, bd, gain, batch):
    n_rows = kin.shape[1]
    blk = lambda b: (b, 0, 0)
    hid2 = B_KV_HEADS * CMP_HIDDEN
    return pl.pallas_call(
        _compress_kernel,
        grid=(batch,),
        in_specs=[
            pl.BlockSpec((1, n_rows, CMP_STRIDE * LANES), blk),
            pl.BlockSpec((1, n_rows, CMP_STRIDE * LANES), blk),
            _const_spec((2, 2, CMP_STRIDE * LANES, hid2)),
            _const_spec((2, 2, SUBLANES, CMP_STRIDE * LANES)),
            _const_spec((2, 1, hid2)),
            _const_spec((2, hid2, LANES)),
            _const_spec((2, 1, LANES)),
            _const_spec((LANES, LANES)),
            _const_spec((1, LANES)),
        ],
        out_specs=[pl.BlockSpec((1, n_rows, LANES), blk), pl.BlockSpec((1, n_rows, LANES), blk)],
        out_shape=[jax.ShapeDtypeStruct((batch, n_rows, LANES), BF16)] * 2,
        compiler_params=_params(1),
        name="nsa_compress",
    )(kin, vin, w1, pos, b1, w2, b2, bd, gain)


def _nsa_cmp_kernel(q_ref, kc_ref, vc_ref, ov_ref, ocmp_ref, selb_ref, *, n_cmp, topn):
    i = pl.program_id(1)
    scale = HEAD_DIM ** -0.5
    n_pad = kc_ref.shape[1]
    lane = lax.broadcasted_iota(I32, (TILE, LANES), 1)
    pos = i * TILE + lax.broadcasted_iota(I32, (TILE, 1), 0)
    ncol = lax.broadcasted_iota(I32, (TILE, n_pad), 1)
    visible = (ncol * CMP_STRIDE + (CMP_LEN - 1) <= pos) & (ncol < n_cmp)
    kc = kc_ref[0]
    vc = vc_ref[0]
    ov_t = ov_ref[...]
    heads = [(p, h) for p in range(B_PAIRS) for h in range(2)]
    qs = []
    for p in range(B_PAIRS):
        q_pair = q_ref[0, :, p * LANES:(p + 1) * LANES]
        qs.extend(jnp.where((lane // HEAD_DIM) == h, q_pair, jnp.zeros_like(q_pair)) for h in range(2))
    scores = [_dot_nt(q, kc) for q in qs]
    probs = []
    for s in scores:
        s = jnp.where(visible, s * scale, NEG_INF)
        e = jnp.where(visible, jnp.exp(s - jnp.max(s, axis=-1, keepdims=True)), 0.0)
        tot = jnp.sum(e, axis=-1, keepdims=True)
        probs.append(jnp.where(tot > 0.0, e / tot, 0.0))
    outs = [_dot(prob.astype(BF16), vc) for prob in probs]
    parts = []
    for prob in probs:
        hi, lo = _split(prob)
        parts.append(_dot_nt(ov_t, hi) + _dot_nt(ov_t, lo))
    for p in range(B_PAIRS):
        o_pair = jnp.where(lane < HEAD_DIM, outs[2 * p], outs[2 * p + 1])
        ocmp_ref[0, :, p * LANES:(p + 1) * LANES] = o_pair.astype(BF16)

    blk = lax.broadcasted_iota(I32, (HEAD_DIM, TILE), 0)
    own = (i * TILE + lax.broadcasted_iota(I32, (1, TILE), 1)) // SLC_BLOCK
    forced = (blk == 0) | (blk == own) | (blk == own - 1)
    for g in range(B_KV_HEADS):
        imp = functools.reduce(lambda a, b: a + b, [parts[c] for c, (p, h) in enumerate(heads) if h == g])
        imp = jnp.where(forced, BIG, imp)
        imp = jnp.where(blk <= own, imp, NEG_INF)
        rank = jnp.zeros((HEAD_DIM, TILE), F32)
        for k in range(HEAD_DIM):
            other = imp[k:k + 1]
            ahead = jnp.where(other > imp, 1.0, jnp.where(other == imp, jnp.where(k < blk, 1.0, 0.0), 0.0))
            rank = rank + ahead
        keep = jnp.where(rank < float(topn), jnp.where(imp > 0.5 * NEG_INF, 0.0, NEG_INF), NEG_INF)
        selb_ref[0, g * HEAD_DIM:(g + 1) * HEAD_DIM, :] = keep


def _nsa_cmp(qk, kc, vc, ov, batch, seq):
    n_pad = kc.shape[1]
    n_cmp = (seq - CMP_LEN) // CMP_STRIDE + 1
    topn = min(SLC_TOPN, seq // SLC_BLOCK)
    kernel = functools.partial(_nsa_cmp_kernel, n_cmp=n_cmp, topn=topn)
    return pl.pallas_call(
        kernel,
        grid=(batch, seq // TILE),
        in_specs=[
            pl.BlockSpec((1, TILE, B_WIDTH), lambda b, i: (b, i, QK_BQ * LANES // B_WIDTH)),
            pl.BlockSpec((1, n_pad, LANES), lambda b, i: (b, 0, 0)),
            pl.BlockSpec((1, n_pad, LANES), lambda b, i: (b, 0, 0)),
            _const_spec((HEAD_DIM, n_pad)),
        ],
        out_specs=[
            pl.BlockSpec((1, TILE, B_WIDTH), lambda b, i: (b, i, 0)),
            pl.BlockSpec((1, LANES, TILE), lambda b, i: (b, 0, i)),
        ],
        out_shape=[
            jax.ShapeDtypeStruct((batch, seq, B_WIDTH), BF16),
            jax.ShapeDtypeStruct((batch, LANES, seq), F32),
        ],
        compiler_params=_params(2),
        name="nsa_cmp_select",
    )(qk, kc, vc, ov)


def _nsa_attn_kernel(q_ref, ks_ref, vst_ref, kw_ref, vwt_ref, selb_ref, ocmp_ref, gate_ref, egt_ref,
                     o_ref, acc_s_ref, acc_w_ref):
    i = pl.program_id(1)
    blocks_per_tile = TILE // SLC_BLOCK
    heads = [(p, h) for p in range(B_PAIRS) for h in range(2)]
    key_i = lax.broadcasted_iota(I32, (TILE, TILE), 0)
    qry_i = lax.broadcasted_iota(I32, (TILE, TILE), 1)
    causal = key_i <= qry_i
    sub = lax.broadcasted_iota(I32, (SUBLANES, TILE), 0)
    own = pl.multiple_of(i * TILE, TILE)
    lanes_of = lambda p: slice(p * LANES, (p + 1) * LANES)
    rows_of = lambda p, h: slice((2 * p + h) * HEAD_DIM, (2 * p + h + 1) * HEAD_DIM)
    kv_rows = (slice(0, HEAD_DIM), slice(HEAD_DIM, LANES))
    zero = jnp.zeros((1, TILE), F32)
    m0 = jnp.full((1, TILE), NEG_INF, F32)
    l0 = jnp.zeros((SUBLANES, TILE), F32)
    scaled = []
    for p in range(B_PAIRS):
        scaled.extend(_scaled_halves(q_ref[0, :, lanes_of(p)])[1])

    def block_biases(h, n):
        first = h * HEAD_DIM + n * blocks_per_tile
        base = pl.multiple_of((first >> 3) << 3, SUBLANES)
        rows8 = selb_ref[0, pl.ds(base, SUBLANES), :]
        off = first - base
        return [jnp.max(jnp.where(sub == off + r, rows8, NEG_INF), axis=0, keepdims=True)
                for r in range(blocks_per_tile)]

    def slc_step(p, h, n, bias, s_t, m, l8):
        parts = [s_t[r * SLC_BLOCK:(r + 1) * SLC_BLOCK] for r in range(blocks_per_tile)]
        cand = [_col_max(parts[r]) + bias[r] for r in range(blocks_per_tile)]
        m_new = functools.reduce(jnp.maximum, cand, m)
        p_t = jnp.concatenate([jnp.exp2(parts[r] - (m_new - bias[r])) for r in range(blocks_per_tile)],
                              axis=0)
        alpha = jnp.exp2(m - m_new)
        l8 = alpha * l8 + jnp.sum(p_t.reshape(-1, SUBLANES, TILE), axis=0)
        start = pl.multiple_of(n * TILE, TILE)
        v_t = vst_ref[0, kv_rows[h], pl.ds(start, TILE)]
        acc_s_ref[rows_of(p, h), :] = alpha * acc_s_ref[rows_of(p, h), :] + _dot(v_t, p_t.astype(BF16))
        return m_new, l8

    def scores_of(k_ref_, start):
        k = k_ref_[0, pl.ds(start, TILE), :]
        return [_dot_nt(k, scaled[c]) for c in range(len(heads))]

    state = []
    own_bias = [block_biases(h, i) for h in range(2)]
    own_scores = scores_of(ks_ref, own)
    for c, (p, h) in enumerate(heads):
        acc_s_ref[rows_of(p, h), :] = jnp.zeros((HEAD_DIM, TILE), F32)
        state.extend(slc_step(p, h, i, own_bias[h], jnp.where(causal, own_scores[c], NEG_INF), m0, l0))

    def past_tile(n, carry):
        scores = scores_of(ks_ref, pl.multiple_of(n * TILE, TILE))
        bias = [block_biases(h, n) for h in range(2)]
        out = []
        for c, (p, h) in enumerate(heads):
            out.extend(slc_step(p, h, n, bias[h], scores[c], carry[2 * c], carry[2 * c + 1]))
        return tuple(out)

    state = lax.fori_loop(0, i, past_tile, tuple(state))

    wstate = [(m0, l0)] * len(heads)
    for back, mask in ((0, causal), (1, None), (2, key_i > qry_i)):
        start = pl.multiple_of(jnp.maximum(i - back, 0) * TILE, TILE)
        scores = scores_of(kw_ref, start)
        exists = jnp.where(i >= back, 0.0, NEG_INF) + zero
        for c, (p, h) in enumerate(heads):
            if back == 0:
                acc_w_ref[rows_of(p, h), :] = jnp.zeros((HEAD_DIM, TILE), F32)
            s_t = scores[c] if mask is None else jnp.where(mask, scores[c], NEG_INF)
            wstate[c] = _flash_step(s_t, exists, vwt_ref[0, kv_rows[h], pl.ds(start, TILE)],
                                    wstate[c][0], wstate[c][1], acc_w_ref, rows_of(p, h))
    wstate = [l8 for (_, l8) in wstate]

    for p in range(B_PAIRS):
        o_slc = _flash_finish((state[4 * p + 1], state[4 * p + 3]), acc_s_ref.at[lanes_of(p)])
        o_win = _flash_finish((wstate[2 * p], wstate[2 * p + 1]), acc_w_ref.at[lanes_of(p)])
        g = _split_dot(gate_ref[0], egt_ref[p])
        out = (g[:, 0:LANES] * ocmp_ref[0, :, lanes_of(p)].astype(F32) + g[:, LANES:2 * LANES] * o_slc
               + g[:, 2 * LANES:3 * LANES] * o_win)
        o_ref[0, :, lanes_of(p)] = out.astype(BF16)


def _nsa_attn(qk, vvt, selb, ocmp, gate, eg, batch, seq):
    assert WINDOW == 2 * TILE
    k_spec = lambda t: pl.BlockSpec((1, seq, LANES), lambda b, i: (b, 0, t))
    vt_spec = lambda t: pl.BlockSpec((1, LANES, seq), lambda b, i: (b, t, 0))
    return pl.pallas_call(
        _nsa_attn_kernel,
        grid=(batch, seq // TILE),
        in_specs=[
            pl.BlockSpec((1, TILE, B_WIDTH), lambda b, i: (b, i, QK_BQ * LANES // B_WIDTH)),
            k_spec(QK_BKS),
            vt_spec(VV_BVS),
            k_spec(QK_BKW),
            vt_spec(VV_BVW),
            pl.BlockSpec((1, LANES, TILE), lambda b, i: (b, 0, i)),
            pl.BlockSpec((1, TILE, B_WIDTH), lambda b, i: (b, i, 0)),
            pl.BlockSpec((1, TILE, LANES), lambda b, i: (b, i, 0)),
            _const_spec((B_PAIRS, LANES, 3 * LANES)),
        ],
        out_specs=pl.BlockSpec((1, TILE, B_WIDTH), lambda b, i: (b, i, 0)),
        out_shape=jax.ShapeDtypeStruct((batch, seq, B_WIDTH), BF16),
        scratch_shapes=[pltpu.VMEM((B_WIDTH, TILE), F32), pltpu.VMEM((B_WIDTH, TILE), F32)],
        compiler_params=_params(2),
        name="nsa_select_window",
    )(qk, qk, vvt, qk, vvt, selb, ocmp, gate, eg)


def _top_rows(v, row_f, count, fill):
    vals, rows = [], []
    for _ in range(count):
        best = jnp.max(v, axis=0, keepdims=True)
        idx = jnp.min(jnp.where(v == best, row_f, fill), axis=0, keepdims=True)
        vals.append(best)
        rows.append(idx)
        v = jnp.where(row_f == idx, -jnp.inf, v)
    return vals, rows


def _merge_peer_kernel(x_ref, ya_ref, yb_ref, mg_ref, wa_ref, wb_ref, wo_ref, g2_ref, wq_ref,
                       k1h_ref, k1l_ref, k2h_ref, k2l_ref,
                       x1_ref, hn_ref, idx_ref, gt_ref, q_scr, ex_scr, gt_scr):
    ua = _dot(ya_ref[...], wa_ref[...])
    ub = _dot(yb_ref[...], wb_ref[...])
    merged = (mg_ref[:, 0:D_MODEL].astype(F32) * ua + mg_ref[:, D_MODEL:2 * D_MODEL].astype(F32) * ub)
    x1 = x_ref[...] + _dot(merged.astype(BF16), wo_ref[...])
    x1_ref[...] = x1
    hn = x1 * lax.rsqrt(jnp.mean(x1 * x1, axis=-1, keepdims=True) + NORM_EPS) * g2_ref[...]
    hn_ref[...] = hn
    q_scr[...] = _dot(hn.astype(BF16), wq_ref[...])

    key_row = lax.broadcasted_iota(I32, (PEER_NKEYS, TILE), 0).astype(F32)
    n_cand = sum(-(-(PEER_TOPK // (a + 1)) // SUBLANES) * SUBLANES for a in range(SUBLANES)) + SUBLANES
    cand_row = lax.broadcasted_iota(I32, (n_cand, TILE), 0).astype(F32)

    def per_head(h, carry):
        base = pl.multiple_of(h * PEER_QDIM, PEER_QDIM)
        q1h, q1l = _split(q_scr[:, pl.ds(base, PEER_HALF)])
        q2h, q2l = _split(q_scr[:, pl.ds(base + PEER_HALF, PEER_HALF)])
        k1h, k2h = k1h_ref[...], k2h_ref[...]
        s1 = _dot_nt(k1h, q1h) + _dot_nt(k1h, q1l) + _dot_nt(k1l_ref[...], q1h)
        s2 = _dot_nt(k2h, q2h) + _dot_nt(k2h, q2l) + _dot_nt(k2l_ref[...], q2h)
        v1, i1 = _top_rows(s1, key_row, PEER_TOPK, float(PEER_NKEYS))
        v2, i2 = _top_rows(s2, key_row, PEER_TOPK, float(PEER_NKEYS))
        v1m, i1m = jnp.concatenate(v1, axis=0), jnp.concatenate(i1, axis=0)
        v2m, i2m = jnp.concatenate(v2, axis=0), jnp.concatenate(i2, axis=0)
        cand, cidx = [], []
        for a in range(SUBLANES):
            n_b = PEER_TOPK // (a + 1)
            rows = -(-n_b // SUBLANES) * SUBLANES
            keep = lax.broadcasted_iota(I32, (rows, TILE), 0) < n_b
            cand.append(jnp.where(keep, v1[a] + v2m[0:rows], -jnp.inf))
            cidx.append(i1[a] * float(PEER_NKEYS) + i2m[0:rows])
        cand.append(v1m[SUBLANES:] + v2[0])
        cidx.append(i1m[SUBLANES:] * float(PEER_NKEYS) + i2[0])
        cand = jnp.concatenate(cand, axis=0)
        cidx = jnp.concatenate(cidx, axis=0)
        vals, experts = [], []
        v = cand
        for _ in range(PEER_TOPK):
            best = jnp.max(v, axis=0, keepdims=True)
            at = jnp.min(jnp.where(v == best, cand_row, float(n_cand)), axis=0, keepdims=True)
            hit = cand_row == at
            experts.append(jnp.max(jnp.where(hit, cidx, -1.0), axis=0, keepdims=True))
            vals.append(best)
            v = jnp.where(hit, -jnp.inf, v)
        vals = jnp.concatenate(vals, axis=0)
        e = jnp.exp(vals - vals[0:1])
        out_row = pl.multiple_of(h * PEER_TOPK, PEER_TOPK)
        gt_scr[pl.ds(out_row, PEER_TOPK), :] = e / jnp.sum(e, axis=0, keepdims=True)
        ex_scr[pl.ds(out_row, PEER_TOPK), :] = jnp.concatenate(experts, axis=0)
        return carry

    lax.fori_loop(0, PEER_HEADS, per_head, 0)
    gt_ref[...] = jnp.transpose(gt_scr[...])
    idx_ref[...] = (jnp.transpose(ex_scr[...]) * float(HALF_ROWS)).astype(I32)


def _merge_peer(x2d, ya, yb, mg, wa, wb, wo, g2, wq, k1h, k1l, k2h, k2l):
    tokens = x2d.shape[0]
    row = lambda i: (i, 0)
    return pl.pallas_call(
        _merge_peer_kernel,
        grid=(tokens // TILE,),
        in_specs=[
            pl.BlockSpec((TILE, D_MODEL), row),
            pl.BlockSpec((TILE, A_WIDTH), row),
            pl.BlockSpec((TILE, B_WIDTH), row),
            pl.BlockSpec((TILE, MG_COLS), row),
            _const_spec((A_WIDTH, D_MODEL)),
            _const_spec((B_WIDTH, D_MODEL)),
            _const_spec((D_MODEL, D_MODEL)),
            _const_spec((1, D_MODEL)),
            _const_spec((D_MODEL, PEER_HEADS * PEER_QDIM)),
            _const_spec((PEER_NKEYS, PEER_HALF)),
            _const_spec((PEER_NKEYS, PEER_HALF)),
            _const_spec((PEER_NKEYS, PEER_HALF)),
            _const_spec((PEER_NKEYS, PEER_HALF)),
        ],
        out_specs=[
            pl.BlockSpec((TILE, D_MODEL), row),
            pl.BlockSpec((TILE, D_MODEL), row),
            pl.BlockSpec((TILE, PEER_SLOTS), row),
            pl.BlockSpec((TILE, PEER_SLOTS), row),
        ],
        out_shape=[
            jax.ShapeDtypeStruct((tokens, D_MODEL), F32),
            jax.ShapeDtypeStruct((tokens, D_MODEL), F32),
            jax.ShapeDtypeStruct((tokens, PEER_SLOTS), I32),
            jax.ShapeDtypeStruct((tokens, PEER_SLOTS), F32),
        ],
        scratch_shapes=[pltpu.VMEM((TILE, PEER_HEADS * PEER_QDIM), F32),
                        pltpu.VMEM((PEER_SLOTS, TILE), F32),
                        pltpu.VMEM((PEER_SLOTS, TILE), F32)],
        compiler_params=_params(1),
        name="merge_peer_topk",
    )(x2d, ya, yb, mg, wa, wb, wo, g2, wq, k1h, k1l, k2h, k2l)


HALF_ROWS = SUBLANES // 2
HI_MASK = -65536
ROW_MASK = 65535
PEER_UNROLL = 2


def _unpack(words):
    hi = pltpu.bitcast(words & HI_MASK, F32)
    lo = pltpu.bitcast(words << 16, F32)
    return hi, lo


def _expert_row(tab_ref, row):
    return tab_ref[pl.ds(pl.multiple_of(row, HALF_ROWS), HALF_ROWS), :]


def _fold_rows(v, shift):
    return v + pltpu.roll(v, shift, axis=0)


def _peer_u_kernel(row_ref, rowv_ref, hn_ref, gt_ref, tab_ref, route_ref):
    lane = lax.broadcasted_iota(I32, (SUBLANES, LANES), 1)
    sub = lax.broadcasted_iota(I32, (SUBLANES, LANES), 0)
    keep_hi_pair = (sub % 4) >= 2
    odd = (sub % 2) == 1
    groups = PEER_SLOTS // SUBLANES

    def one_token(t, accs):
        x = hn_ref[t]
        xa = jnp.concatenate([x[0:HALF_ROWS], x[0:HALF_ROWS]], axis=0)
        xb = jnp.concatenate([x[HALF_ROWS:], x[HALF_ROWS:]], axis=0)

        def pair(ja, jb):
            words = jnp.concatenate([_expert_row(tab_ref, row_ref[t, ja]),
                                     _expert_row(tab_ref, row_ref[t, jb])], axis=0)
            hi, lo = _unpack(words)
            return hi * xa + lo * xb

        new = []
        for g in range(groups):
            j = g * SUBLANES
            quads = []
            for (a, b, c, d) in ((j + 3, j + 7, j + 1, j + 5), (j + 2, j + 6, j, j + 4)):
                w_ab = _fold_rows(pair(a, b), 2)
                w_cd = _fold_rows(pair(c, d), 2)
                z = jnp.where(keep_hi_pair, w_ab, pltpu.roll(w_cd, 6, axis=0))
                quads.append(_fold_rows(z, 1))
            folded = jnp.where(odd, quads[0], pltpu.roll(quads[1], 7, axis=0))
            total = jnp.sum(folded, axis=1, keepdims=True)
            new.append(jnp.where(lane == t, total, accs[g]))
        return tuple(new)

    def step(i, accs):
        for k in range(PEER_UNROLL):
            accs = one_token(i * PEER_UNROLL + k, accs)
        return accs

    init = tuple(jnp.zeros((SUBLANES, LANES), F32) for _ in range(groups))
    accs = lax.fori_loop(0, PEER_TB // PEER_UNROLL, step, init)
    a = jnp.transpose(jnp.concatenate(accs, axis=0))
    w = (_gelu_tanh(a) * gt_ref[...]).astype(BF16).astype(F32)
    route_ref[...] = (pltpu.bitcast(w, I32) & HI_MASK) | rowv_ref[...]


def _peer_u(rows, hn3, gt, tab):
    tokens = hn3.shape[0]
    blk = lambda i: (i, 0)
    return pl.pallas_call(
        _peer_u_kernel,
        grid=(tokens // PEER_TB,),
        in_specs=[
            pl.BlockSpec((PEER_TB, PEER_SLOTS), blk, memory_space=pltpu.SMEM),
            pl.BlockSpec((PEER_TB, PEER_SLOTS), blk),
            pl.BlockSpec((PEER_TB, SUBLANES, LANES), lambda i: (i, 0, 0)),
            pl.BlockSpec((PEER_TB, PEER_SLOTS), blk),
            _const_spec((PEER_N * HALF_ROWS, LANES)),
        ],
        out_specs=pl.BlockSpec((PEER_TB, PEER_SLOTS), blk),
        out_shape=jax.ShapeDtypeStruct((tokens, PEER_SLOTS), I32),
        compiler_params=_params(1),
        name="peer_expert_in",
    )(rows, rows, hn3, gt, tab)


def _peer_v_kernel(route_ref, x1_ref, tab_ref, o_ref):
    def per_token(t, carry):
        acc_hi = jnp.zeros((HALF_ROWS, LANES), F32)
        acc_lo = jnp.zeros((HALF_ROWS, LANES), F32)
        for j in range(PEER_SLOTS):
            word = route_ref[t, j]
            hi, lo = _unpack(_expert_row(tab_ref, word & ROW_MASK))
            w = pltpu.bitcast(jnp.full((HALF_ROWS, LANES), word, I32) & HI_MASK, F32)
            acc_hi = acc_hi + w * hi
            acc_lo = acc_lo + w * lo
        o_ref[t] = x1_ref[t] + jnp.concatenate([acc_hi, acc_lo], axis=0)
        return carry

    lax.fori_loop(0, PEER_TB, per_token, 0)


def _peer_v(route, x13, tab):
    tokens = x13.shape[0]
    return pl.pallas_call(
        _peer_v_kernel,
        grid=(tokens // PEER_TB,),
        in_specs=[
            pl.BlockSpec((PEER_TB, PEER_SLOTS), lambda i: (i, 0), memory_space=pltpu.SMEM),
            pl.BlockSpec((PEER_TB, SUBLANES, LANES), lambda i: (i, 0, 0)),
            _const_spec((PEER_N * HALF_ROWS, LANES)),
        ],
        out_specs=pl.BlockSpec((PEER_TB, SUBLANES, LANES), lambda i: (i, 0, 0)),
        out_shape=jax.ShapeDtypeStruct((tokens, SUBLANES, LANES), F32),
        compiler_params=_params(1),
        name="peer_expert_out",
    )(route, x13, tab)


def _pack_table(t):
    bits = lax.bitcast_convert_type(t.astype(BF16), jnp.uint16).astype(jnp.uint32)
    half = D_MODEL // 2
    words = (bits[:, :half] << 16) | bits[:, half:]
    return lax.bitcast_convert_type(words, I32).reshape(t.shape[0] * HALF_ROWS, LANES)


def _rope_tables(seq):
    inv = ROPE_THETA ** (-jnp.arange(0, ROT_DIM, 2, dtype=F32) / ROT_DIM)
    ang = jnp.arange(seq, dtype=F32)[:, None] * inv[None, :]
    d = np.arange(LANES) % HEAD_DIM
    cos = jnp.where(d[None, :] < ROT_DIM, jnp.cos(ang)[:, d % ROT_HALF], 1.0)
    sin = jnp.where(d[None, :] < ROT_DIM, jnp.sin(ang)[:, d % ROT_HALF], 0.0)
    return cos.astype(F32), sin.astype(F32)


def _compress_weights(w1, pos):
    out_w, out_p = [], []
    for part in range(2):
        wpart = w1[part * CMP_STRIDE * HEAD_DIM:(part + 1) * CMP_STRIDE * HEAD_DIM]
        wpart = wpart.reshape(CMP_STRIDE, HEAD_DIM, CMP_HIDDEN)
        full = jnp.zeros((CMP_STRIDE, B_KV_HEADS, HEAD_DIM, B_KV_HEADS, CMP_HIDDEN), F32)
        for g in range(B_KV_HEADS):
            full = full.at[:, g, :, g, :].set(wpart)
        out_w.append(full.reshape(CMP_STRIDE * LANES, B_KV_HEADS * CMP_HIDDEN))
        ppart = pos[part * CMP_STRIDE:(part + 1) * CMP_STRIDE]
        prow = jnp.tile(ppart[:, None, :], (1, B_KV_HEADS, 1)).reshape(1, CMP_STRIDE * LANES)
        out_p.append(jnp.tile(prow, (SUBLANES, 1)))
    return jnp.stack(out_w).astype(BF16), jnp.stack(out_p).astype(BF16)


def kernel(x, norm1_g, w_in, b_merge, a_q_g, a_k_g, b_q_g, b_kc_g, b_ks_g, b_kw_g, cmp_pos_k,
           cmp_k_w1, cmp_k_b1, cmp_k_w2, cmp_k_b2, cmp_pos_v, cmp_v_w1, cmp_v_b1, cmp_v_w2,
           cmp_v_b2, w_up_a, w_up_b, w_out, norm2_g, peer_wq, peer_k1, peer_k2, peer_u, peer_v):
    batch, seq, _ = x.shape
    assert seq % TILE == 0 and seq // SLC_BLOCK <= HEAD_DIM
    tokens = batch * seq
    l = 0
    x2d = x.reshape(tokens, D_MODEL)

    w = w_in[l]
    sizes = (A_WIDTH, A_WIDTH, A_WIDTH, B_WIDTH) + (B_KV_WIDTH,) * 6 + (3 * B_HEADS, MG_COLS)
    offs = np.concatenate([[0], np.cumsum(sizes)])
    seg = lambda k: w[:, offs[k]:offs[k + 1]]
    aq, ak, av, bq, bkc, bvc, bks, bvs, bkw, bvw, bgate, mgate = (seg(k) for k in range(12))
    bq_perm = bq.reshape(D_MODEL, B_HEADS, HEAD_DIM)[:, np.array(B_HEAD_ORDER)].reshape(D_MODEL, B_WIDTH)
    w_rope = jnp.concatenate([aq, ak, bq_perm, bks, bkw], axis=1).astype(BF16)
    w_plain = jnp.concatenate([av, bvs, bvw], axis=1).T.astype(BF16)
    w_gate = jnp.pad(bgate, ((0, 0), (0, LANES - 3 * B_HEADS))).astype(BF16)
    hg = jnp.concatenate([jnp.tile(a_q_g[l], A_HEADS), jnp.tile(a_k_g[l], A_HEADS),
                          jnp.tile(b_q_g[l], B_HEADS), jnp.tile(b_ks_g[l], B_KV_HEADS),
                          jnp.tile(b_kw_g[l], B_KV_HEADS)])[None, :]
    cos_t, sin_t = _rope_tables(seq)
    bd = jnp.asarray(np.kron(np.eye(LANES // HEAD_DIM), np.full((HEAD_DIM, HEAD_DIM), 1.0 / HEAD_DIM)),
                     BF16)

    qk, vvt, kcin, vcin, gate, mg, kmean = _inproj(
        x2d, norm1_g[l][None, :], w_rope, w_plain, bkc.astype(BF16), bvc.astype(BF16), w_gate,
        mgate.astype(BF16), hg, cos_t, sin_t, bd, b_merge[l].reshape(1, MG_COLS), seq)
    qk = qk.reshape(batch, seq, ROPE_COLS)

    n_blk = seq // A_BLOCK
    kmean = jnp.pad(kmean.reshape(batch, n_blk, A_WIDTH), ((0, 0), (0, LANES - n_blk), (0, 0)))
    km_hi = kmean.astype(BF16)
    km_lo = (kmean - km_hi.astype(F32)).astype(BF16)
    ya = _moba(qk, vvt, km_hi, km_lo, batch, seq)

    n_rows = seq // CMP_STRIDE
    wk1, pk = _compress_weights(cmp_k_w1[l], cmp_pos_k[l])
    wv1, pv = _compress_weights(cmp_v_w1[l], cmp_pos_v[l])
    blockdiag = lambda m: jnp.kron(jnp.eye(B_KV_HEADS, dtype=F32), m)
    w2 = jnp.stack([blockdiag(cmp_k_w2[l]), blockdiag(cmp_v_w2[l])]).astype(BF16)
    b1 = jnp.stack([jnp.tile(cmp_k_b1[l], B_KV_HEADS), jnp.tile(cmp_v_b1[l], B_KV_HEADS)])[:, None, :]
    b2 = jnp.stack([jnp.tile(cmp_k_b2[l], B_KV_HEADS), jnp.tile(cmp_v_b2[l], B_KV_HEADS)])[:, None, :]
    kc, vc = _compress(kcin.reshape(batch, n_rows, CMP_STRIDE * LANES),
                       vcin.reshape(batch, n_rows, CMP_STRIDE * LANES),
                       jnp.stack([wk1, wv1]), jnp.stack([pk, pv]), b1, w2, b2, bd,
                       jnp.tile(b_kc_g[l], B_KV_HEADS)[None, :], batch)

    n_cmp = (seq - CMP_LEN) // CMP_STRIDE + 1
    n_slc = seq // SLC_BLOCK
    ci = np.arange(n_rows)[:, None]
    sj = np.arange(HEAD_DIM)[None, :]
    ov = ((ci * CMP_STRIDE < (sj + 1) * SLC_BLOCK) & (ci * CMP_STRIDE + CMP_LEN > sj * SLC_BLOCK)
          & (ci < n_cmp) & (sj < n_slc)).astype(np.float32)
    ov = jnp.asarray(ov.T, BF16)
    ocmp, selb = _nsa_cmp(qk, kc, vc, ov, batch, seq)

    eg = np.zeros((B_PAIRS, LANES, 3 * LANES), np.float32)
    for p in range(B_PAIRS):
        for half in range(2):
            head = B_HEAD_ORDER[2 * p + half]
            for c in range(3):
                eg[p, head * 3 + c, c * LANES + half * HEAD_DIM:c * LANES + (half + 1) * HEAD_DIM] = 1.0
    yb = _nsa_attn(qk, vvt, selb, ocmp, gate.reshape(batch, seq, LANES), jnp.asarray(eg, BF16),
                   batch, seq)

    wb_perm = w_up_b[l].reshape(B_HEADS, HEAD_DIM, D_MODEL)[np.array(B_HEAD_ORDER)].reshape(
        B_WIDTH, D_MODEL)
    k1h, k1l = _split(peer_k1[l])
    k2h, k2l = _split(peer_k2[l])
    x1, hn, rows, gt = _merge_peer(
        x2d, ya.reshape(tokens, A_WIDTH), yb.reshape(tokens, B_WIDTH), mg,
        w_up_a[l].astype(BF16), wb_perm.astype(BF16), w_out[l].astype(BF16),
        norm2_g[l][None, :], peer_wq[l].astype(BF16), k1h, k1l, k2h, k2l)

    route = _peer_u(rows, hn.reshape(tokens, SUBLANES, LANES), gt, _pack_table(peer_u[l]))
    out = _peer_v(route, x1.reshape(tokens, SUBLANES, LANES), _pack_table(peer_v[l]))
    return out.reshape(batch, seq, D_MODEL)
```

```python
import functools
import math

import jax
import jax.numpy as jnp
import numpy as np
from jax import lax
from jax.experimental import pallas as pl
from jax.experimental.pallas import tpu as pltpu

D_MODEL = 1024
HEAD_DIM = 64
ROT_DIM = HEAD_DIM // 4
ROT_HALF = ROT_DIM // 2
ROPE_THETA = 500000.0
NORM_EPS = 1e-6
NEG_INF = -1e30
BIG = 1e30

A_HEADS = 8
A_BLOCK = 256
A_TOPK = 3

B_HEADS = 8
B_KV_HEADS = 2
B_GROUP = B_HEADS // B_KV_HEADS
CMP_LEN = 32
CMP_STRIDE = 16
CMP_HIDDEN = 256
SLC_BLOCK = 64
SLC_TOPN = 16
WINDOW = 512

PEER_HEADS = 8
PEER_NKEYS = 128
PEER_N = PEER_NKEYS * PEER_NKEYS
PEER_QDIM = 256
PEER_HALF = PEER_QDIM // 2
PEER_TOPK = 16
PEER_SLOTS = PEER_HEADS * PEER_TOPK
PEER_HEAD_UNROLL = 2

A_WIDTH = A_HEADS * HEAD_DIM
B_WIDTH = B_HEADS * HEAD_DIM
B_KV_WIDTH = B_KV_HEADS * HEAD_DIM
N_BRANCH = 2

LANES = 128
SUBLANES = 8
VMEM_LIMIT_BYTES = 56 * 1024 * 1024

ROPE_COLS = 2 * A_WIDTH + B_WIDTH + 2 * B_KV_WIDTH
QK_AQ, QK_AK, QK_BQ, QK_BKS, QK_BKW = 0, 4, 8, 12, 13
PLAIN_COLS = A_WIDTH + 2 * B_KV_WIDTH
VV_AV, VV_BVS, VV_BVW = 0, 4, 5
MG_COLS = N_BRANCH * D_MODEL
B_PAIRS = B_WIDTH // LANES
B_HEAD_ORDER = tuple(h for p in range(B_PAIRS) for h in (p, B_GROUP + p))

TILE = 256
PEER_TB = 128
F32 = jnp.float32
BF16 = jnp.bfloat16
I32 = jnp.int32


def _const_spec(shape):
    n = len(shape)
    return pl.BlockSpec(shape, lambda *_: (0,) * n, pipeline_mode=pl.Buffered(1))


def _params(n_axes):
    return pltpu.CompilerParams(dimension_semantics=("arbitrary",) * n_axes,
                                vmem_limit_bytes=VMEM_LIMIT_BYTES)


def _dot(a, b):
    return jnp.dot(a, b, preferred_element_type=F32)


def _dot_nt(a, b):
    return lax.dot_general(a, b, (((1,), (1,)), ((), ())), preferred_element_type=F32)


def _split(a_f32):
    hi = a_f32.astype(BF16)
    lo = (a_f32 - hi.astype(F32)).astype(BF16)
    return hi, lo


def _split_dot(a_f32, b_bf16):
    hi, lo = _split(a_f32)
    return _dot(hi, b_bf16) + _dot(lo, b_bf16)


def _gelu_tanh(x):
    return 0.5 * x * (1.0 + jnp.tanh(math.sqrt(2.0 / math.pi) * (x + 0.044715 * (x * x * x))))


def _group_rmsnorm(y, bd, gain):
    ms = _split_dot(y * y, bd)
    return y * lax.rsqrt(ms + NORM_EPS) * gain


def _inproj_kernel(x_ref, g1_ref, w_rope_ref, w_plain_ref, w_small_ref, w_mg_ref, hg_ref,
                   cos_ref, sin_ref, bd_ref, bm_ref,
                   qk_ref, vvt_ref, kcin_ref, vcin_ref, gate_ref, mg_ref, kmean_ref):
    x = x_ref[...]
    h = x * lax.rsqrt(jnp.mean(x * x, axis=-1, keepdims=True) + NORM_EPS) * g1_ref[...]
    hb = h.astype(BF16)

    y_all = _dot(hb, w_rope_ref[...])
    vvt_ref[0] = _dot_nt(w_plain_ref[...], hb).astype(BF16)
    small = _dot(hb, w_small_ref[...])
    kcin_ref[...] = small[:, 0:LANES].astype(BF16)
    vcin_ref[...] = small[:, LANES:2 * LANES].astype(BF16)
    gate_ref[...] = jax.nn.sigmoid(small[:, 2 * LANES:3 * LANES])
    mg_ref[...] = jax.nn.sigmoid(_dot(hb, w_mg_ref[...]) + bm_ref[...]).astype(BF16)

    cos = cos_ref[...]
    sin = sin_ref[...]
    bd = bd_ref[...]
    lane = lax.broadcasted_iota(I32, (TILE, LANES), 1)
    first = (lane % HEAD_DIM) < ROT_HALF
    wide = 2 * LANES
    for t2 in range(ROPE_COLS // wide):
        y2 = y_all[:, t2 * wide:(t2 + 1) * wide]
        ms2 = _split_dot(y2 * y2, bd)
        for half in range(2):
            t = 2 * t2 + half
            sl = slice(t * LANES, (t + 1) * LANES)
            hs = slice(half * LANES, (half + 1) * LANES)
            yn = y2[:, hs] * lax.rsqrt(ms2[:, hs] + NORM_EPS) * hg_ref[:, sl]
            up = pltpu.roll(yn, LANES - ROT_HALF, axis=1)
            dn = pltpu.roll(yn, ROT_HALF, axis=1)
            out = yn * cos + jnp.where(first, -up, dn) * sin
            qk_ref[:, sl] = out.astype(BF16)
            if QK_AK <= t < QK_AK + A_WIDTH // LANES:
                c = t - QK_AK
                kmean_ref[0, :, c * LANES:(c + 1) * LANES] = jnp.mean(out, axis=0, keepdims=True)


def _inproj(x2d, g1, w_rope, w_plain, w_small, w_mg, hg, cos_t, sin_t, bd2, bm, seq):
    tokens = x2d.shape[0]
    nt = tokens // TILE
    tiles_per_seq = seq // TILE
    row = lambda i: (i, 0)
    pos = lambda i: (i % tiles_per_seq, 0)
    return pl.pallas_call(
        _inproj_kernel,
        grid=(nt,),
        in_specs=[
            pl.BlockSpec((TILE, D_MODEL), row),
            _const_spec((1, D_MODEL)),
            _const_spec((D_MODEL, ROPE_COLS)),
            _const_spec((PLAIN_COLS, D_MODEL)),
            _const_spec((D_MODEL, 3 * LANES)),
            _const_spec((D_MODEL, MG_COLS)),
            _const_spec((1, ROPE_COLS)),
            pl.BlockSpec((TILE, LANES), pos),
            pl.BlockSpec((TILE, LANES), pos),
            _const_spec((2 * LANES, 2 * LANES)),
            _const_spec((1, MG_COLS)),
        ],
        out_specs=[
            pl.BlockSpec((TILE, ROPE_COLS), row),
            pl.BlockSpec((1, PLAIN_COLS, TILE), lambda i: (i // tiles_per_seq, 0, i % tiles_per_seq)),
            pl.BlockSpec((TILE, LANES), row),
            pl.BlockSpec((TILE, LANES), row),
            pl.BlockSpec((TILE, LANES), row),
            pl.BlockSpec((TILE, MG_COLS), row),
            pl.BlockSpec((1, 1, A_WIDTH), lambda i: (i, 0, 0)),
        ],
        out_shape=[
            jax.ShapeDtypeStruct((tokens, ROPE_COLS), BF16),
            jax.ShapeDtypeStruct((tokens // seq, PLAIN_COLS, seq), BF16),
            jax.ShapeDtypeStruct((tokens, LANES), BF16),
            jax.ShapeDtypeStruct((tokens, LANES), BF16),
            jax.ShapeDtypeStruct((tokens, LANES), F32),
            jax.ShapeDtypeStruct((tokens, MG_COLS), BF16),
            jax.ShapeDtypeStruct((nt, 1, A_WIDTH), F32),
        ],
        compiler_params=_params(1),
        name="inproj",
    )(x2d, g1, w_rope, w_plain, w_small, w_mg, hg, cos_t, sin_t, bd2, bm)


LOG2E = math.log2(math.e)


def _scaled_halves(q_pair):
    lane = lax.broadcasted_iota(I32, (TILE, LANES), 1)
    q = q_pair.astype(F32)
    c = HEAD_DIM ** -0.5 * LOG2E
    plain = [jnp.where((lane // HEAD_DIM) == h, q, 0.0).astype(BF16) for h in range(2)]
    scaled = [jnp.where((lane // HEAD_DIM) == h, q * c, 0.0).astype(BF16) for h in range(2)]
    return plain, scaled


def _col_max(s_t):
    part = jnp.max(s_t.reshape(-1, SUBLANES, s_t.shape[-1]), axis=0)
    return jnp.max(part, axis=0, keepdims=True)


def _flash_step(s_t, shift_bias, v_t, m, l8, acc_ref, rows):
    m_new = jnp.maximum(m, _col_max(s_t) + shift_bias)
    p_t = jnp.exp2(s_t - (m_new - shift_bias))
    alpha = jnp.exp2(m - m_new)
    l8 = alpha * l8 + jnp.sum(p_t.reshape(-1, SUBLANES, p_t.shape[-1]), axis=0)
    acc_ref[rows, :] = alpha * acc_ref[rows, :] + _dot(v_t, p_t.astype(BF16))
    return m_new, l8


def _flash_finish(l8s, acc_ref):
    inv = [1.0 / jnp.sum(l8, axis=0, keepdims=True) for l8 in l8s]
    out_t = jnp.concatenate([acc_ref[0:HEAD_DIM, :] * inv[0], acc_ref[HEAD_DIM:, :] * inv[1]], axis=0)
    return jnp.transpose(out_t)


def _moba_kernel(q_ref, k_ref, vt_ref, kmh_ref, kml_ref, o_ref, acc_ref, *, topk, gate_rows):
    i = pl.program_id(1)
    pairs = A_WIDTH // LANES
    heads = [(p, h) for p in range(pairs) for h in range(2)]
    key_i = lax.broadcasted_iota(I32, (TILE, TILE), 0)
    qry_i = lax.broadcasted_iota(I32, (TILE, TILE), 1)
    causal = key_i <= qry_i
    blk = lax.broadcasted_iota(I32, (gate_rows, TILE), 0)
    blk_f = blk.astype(F32)
    own = pl.multiple_of(i * TILE, TILE)
    lanes_of = lambda p: slice(p * LANES, (p + 1) * LANES)
    rows_of = lambda p, h: slice((2 * p + h) * HEAD_DIM, (2 * p + h + 1) * HEAD_DIM)
    zero = jnp.zeros((1, TILE), F32)
    m0 = jnp.full((1, TILE), NEG_INF, F32)
    l0 = jnp.zeros((SUBLANES, TILE), F32)

    plain, scaled = [], []
    for p in range(pairs):
        plain_p, scaled_p = _scaled_halves(q_ref[0, :, lanes_of(p)])
        plain.extend(plain_p)
        scaled.extend(scaled_p)
    gates = [(_dot_nt(kmh_ref[0, :, lanes_of(p)], plain[c])
              + _dot_nt(kml_ref[0, :, lanes_of(p)], plain[c]))[0:gate_rows]
             for c, (p, h) in enumerate(heads)]
    own_scores = [_dot_nt(k_ref[0, pl.ds(own, TILE), lanes_of(p)], scaled[c])
                  for c, (p, h) in enumerate(heads)]

    biases, state = [], []
    for c, (p, h) in enumerate(heads):
        gate = jnp.where(blk < i, gates[c], NEG_INF)
        bias = jnp.full((gate_rows, TILE), NEG_INF, F32)
        for _ in range(topk):
            best = jnp.max(gate, axis=0, keepdims=True)
            idx = jnp.min(jnp.where(gate == best, blk_f, float(gate_rows)), axis=0, keepdims=True)
            pick = blk_f == idx
            bias = jnp.where(pick, jnp.where(best > 0.5 * NEG_INF, 0.0, bias), bias)
            gate = jnp.where(pick, NEG_INF, gate)
        biases.append(bias)
        acc_ref[rows_of(p, h), :] = jnp.zeros((HEAD_DIM, TILE), F32)
        state.extend(_flash_step(jnp.where(causal, own_scores[c], NEG_INF), zero,
                                 vt_ref[0, rows_of(p, h), pl.ds(own, TILE)], m0, l0,
                                 acc_ref, rows_of(p, h)))

    def past_block(n, carry):
        start = pl.multiple_of(n * TILE, TILE)
        out = []
        scores = [_dot_nt(k_ref[0, pl.ds(start, TILE), lanes_of(p)], scaled[c])
                  for c, (p, h) in enumerate(heads)]
        for c, (p, h) in enumerate(heads):
            b = jnp.max(jnp.where(blk == n, biases[c], NEG_INF), axis=0, keepdims=True)
            out.extend(_flash_step(scores[c], b, vt_ref[0, rows_of(p, h), pl.ds(start, TILE)],
                                   carry[2 * c], carry[2 * c + 1], acc_ref, rows_of(p, h)))
        return tuple(out)

    state = lax.fori_loop(0, i, past_block, tuple(state))
    for p in range(pairs):
        l8s = (state[4 * p + 1], state[4 * p + 3])
        o_ref[0, :, lanes_of(p)] = _flash_finish(l8s, acc_ref.at[lanes_of(p)]).astype(BF16)


def _moba(qk, vvt, kmean_hi, kmean_lo, batch, seq):
    n_blk = seq // A_BLOCK
    gate_rows = -(-n_blk // SUBLANES) * SUBLANES
    kernel = functools.partial(_moba_kernel, topk=min(A_TOPK, n_blk), gate_rows=gate_rows)
    return pl.pallas_call(
        kernel,
        grid=(batch, n_blk),
        in_specs=[
            pl.BlockSpec((1, TILE, A_WIDTH), lambda b, i: (b, i, QK_AQ * LANES // A_WIDTH)),
            pl.BlockSpec((1, seq, A_WIDTH), lambda b, i: (b, 0, QK_AK * LANES // A_WIDTH)),
            pl.BlockSpec((1, A_WIDTH, seq), lambda b, i: (b, VV_AV * LANES // A_WIDTH, 0)),
            pl.BlockSpec((1, LANES, A_WIDTH), lambda b, i: (b, 0, 0)),
            pl.BlockSpec((1, LANES, A_WIDTH), lambda b, i: (b, 0, 0)),
        ],
        out_specs=pl.BlockSpec((1, TILE, A_WIDTH), lambda b, i: (b, i, 0)),
        out_shape=jax.ShapeDtypeStruct((batch, seq, A_WIDTH), BF16),
        scratch_shapes=[pltpu.VMEM((A_WIDTH, TILE), F32)],
        compiler_params=_params(2),
        name="moba",
    )(qk, qk, vvt, kmean_hi, kmean_lo)


def _compress_kernel(kin_ref, vin_ref, w1_ref, pos_ref, b1_ref, w2_ref, b2_ref, bd_ref, g_ref,
                     kc_ref, vc_ref):
    n_rows = kin_ref.shape[1]
    for c, (in_ref, out_ref) in enumerate(((kin_ref, kc_ref), (vin_ref, vc_ref))):
        r = in_ref[0]
        first = _dot(r, w1_ref[c, 0])
        second = _dot(r, w1_ref[c, 1])
        const = (_dot(pos_ref[c, 0], w1_ref[c, 0]) + _dot(pos_ref[c, 1], w1_ref[c, 1]))[0:1]
        hid = first + pltpu.roll(second, n_rows - 1, axis=0) + const + b1_ref[c]
        out = _dot(_gelu_tanh(hid).astype(BF16), w2_ref[c]) + b2_ref[c]
        if c == 0:
            out = _group_rmsnorm(out, bd_ref[...], g_ref[...])
        out_ref[0] = out.astype(BF16)


def _compress(kin, vin, w1, pos, b1, w2, b2, bd, gain, batch):
    n_rows = kin.shape[1]
    blk = lambda b: (b, 0, 0)
    hid2 = B_KV_HEADS * CMP_HIDDEN
    return pl.pallas_call(
        _compress_kernel,
        grid=(batch,),
        in_specs=[
            pl.BlockSpec((1, n_rows, CMP_STRIDE * LANES), blk),
            pl.BlockSpec((1, n_rows, CMP_STRIDE * LANES), blk),
            _const_spec((2, 2, CMP_STRIDE * LANES, hid2)),
            _const_spec((2, 2, SUBLANES, CMP_STRIDE * LANES)),
            _const_spec((2, 1, hid2)),
            _const_spec((2, hid2, LANES)),
            _const_spec((2, 1, LANES)),
            _const_spec((LANES, LANES)),
            _const_spec((1, LANES)),
        ],
        out_specs=[pl.BlockSpec((1, n_rows, LANES), blk), pl.BlockSpec((1, n_rows, LANES), blk)],
        out_shape=[jax.ShapeDtypeStruct((batch, n_rows, LANES), BF16)] * 2,
        compiler_params=_params(1),
        name="nsa_compress",
    )(kin, vin, w1, pos, b1, w2, b2, bd, gain)


def _nsa_cmp_kernel(q_ref, kc_ref, vc_ref, ov_ref, ocmp_ref, selb_ref, *, n_cmp, topn):
    i = pl.program_id(1)
    scale = HEAD_DIM ** -0.5
    n_pad = kc_ref.shape[1]
    lane = lax.broadcasted_iota(I32, (TILE, LANES), 1)
    pos = i * TILE + lax.broadcasted_iota(I32, (TILE, 1), 0)
    ncol = lax.broadcasted_iota(I32, (TILE, n_pad), 1)
    visible = (ncol * CMP_STRIDE + (CMP_LEN - 1) <= pos) & (ncol < n_cmp)
    kc = kc_ref[0]
    vc = vc_ref[0]
    ov_t = ov_ref[...]
    heads = [(p, h) for p in range(B_PAIRS) for h in range(2)]
    qs = []
    for p in range(B_PAIRS):
        q_pair = q_ref[0, :, p * LANES:(p + 1) * LANES]
        qs.extend(jnp.where((lane // HEAD_DIM) == h, q_pair, jnp.zeros_like(q_pair)) for h in range(2))
    scores = [_dot_nt(q, kc) for q in qs]
    probs = []
    for s in scores:
        s = jnp.where(visible, s * scale, NEG_INF)
        e = jnp.where(visible, jnp.exp(s - jnp.max(s, axis=-1, keepdims=True)), 0.0)
        tot = jnp.sum(e, axis=-1, keepdims=True)
        probs.append(jnp.where(tot > 0.0, e / tot, 0.0))
    outs = [_dot(prob.astype(BF16), vc) for prob in probs]
    parts = []
    for prob in probs:
        hi, lo = _split(prob)
        parts.append(_dot_nt(ov_t, hi) + _dot_nt(ov_t, lo))
    for p in range(B_PAIRS):
        o_pair = jnp.where(lane < HEAD_DIM, outs[2 * p], outs[2 * p + 1])
        ocmp_ref[0, :, p * LANES:(p + 1) * LANES] = o_pair.astype(BF16)

    blk = lax.broadcasted_iota(I32, (HEAD_DIM, TILE), 0)
    own = (i * TILE + lax.broadcasted_iota(I32, (1, TILE), 1)) // SLC_BLOCK
    forced = (blk == 0) | (blk == own) | (blk == own - 1)
    for g in range(B_KV_HEADS):
        imp = functools.reduce(lambda a, b: a + b, [parts[c] for c, (p, h) in enumerate(heads) if h == g])
        imp = jnp.where(forced, BIG, imp)
        imp = jnp.where(blk <= own, imp, NEG_INF)
        rank = jnp.zeros((HEAD_DIM, TILE), F32)
        for k in range(HEAD_DIM):
            other = imp[k:k + 1]
            ahead = jnp.where(other > imp, 1.0, jnp.where(other == imp, jnp.where(k < blk, 1.0, 0.0), 0.0))
            rank = rank + ahead
        keep = jnp.where(rank < float(topn), jnp.where(imp > 0.5 * NEG_INF, 0.0, NEG_INF), NEG_INF)
        selb_ref[0, g * HEAD_DIM:(g + 1) * HEAD_DIM, :] = keep


def _nsa_cmp(qk, kc, vc, ov, batch, seq):
    n_pad = kc.shape[1]
    n_cmp = (seq - CMP_LEN) // CMP_STRIDE + 1
    topn = min(SLC_TOPN, seq // SLC_BLOCK)
    kernel = functools.partial(_nsa_cmp_kernel, n_cmp=n_cmp, topn=topn)
    return pl.pallas_call(
        kernel,
        grid=(batch, seq // TILE),
        in_specs=[
            pl.BlockSpec((1, TILE, B_WIDTH), lambda b, i: (b, i, QK_BQ * LANES // B_WIDTH)),
            pl.BlockSpec((1, n_pad, LANES), lambda b, i: (b, 0, 0)),
            pl.BlockSpec((1, n_pad, LANES), lambda b, i: (b, 0, 0)),
            _const_spec((HEAD_DIM, n_pad)),
        ],
        out_specs=[
            pl.BlockSpec((1, TILE, B_WIDTH), lambda b, i: (b, i, 0)),
            pl.BlockSpec((1, LANES, TILE), lambda b, i: (b, 0, i)),
        ],
        out_shape=[
            jax.ShapeDtypeStruct((batch, seq, B_WIDTH), BF16),
            jax.ShapeDtypeStruct((batch, LANES, seq), F32),
        ],
        compiler_params=_params(2),
        name="nsa_cmp_select",
    )(qk, kc, vc, ov)


def _nsa_attn_kernel(q_ref, ks_ref, vst_ref, kw_ref, vwt_ref, selb_ref, ocmp_ref, gate_ref, egt_ref,
                     o_ref, acc_s_ref, acc_w_ref):
    i = pl.program_id(1)
    blocks_per_tile = TILE // SLC_BLOCK
    heads = [(p, h) for p in range(B_PAIRS) for h in range(2)]
    key_i = lax.broadcasted_iota(I32, (TILE, TILE), 0)
    qry_i = lax.broadcasted_iota(I32, (TILE, TILE), 1)
    causal = key_i <= qry_i
    sub = lax.broadcasted_iota(I32, (SUBLANES, TILE), 0)
    own = pl.multiple_of(i * TILE, TILE)
    lanes_of = lambda p: slice(p * LANES, (p + 1) * LANES)
    rows_of = lambda p, h: slice((2 * p + h) * HEAD_DIM, (2 * p + h + 1) * HEAD_DIM)
    kv_rows = (slice(0, HEAD_DIM), slice(HEAD_DIM, LANES))
    zero = jnp.zeros((1, TILE), F32)
    m0 = jnp.full((1, TILE), NEG_INF, F32)
    l0 = jnp.zeros((SUBLANES, TILE), F32)
    scaled = []
    for p in range(B_PAIRS):
        scaled.extend(_scaled_halves(q_ref[0, :, lanes_of(p)])[1])

    def block_biases(h, n):
        first = h * HEAD_DIM + n * blocks_per_tile
        base = pl.multiple_of((first >> 3) << 3, SUBLANES)
        rows8 = selb_ref[0, pl.ds(base, SUBLANES), :]
        off = first - base
        return [jnp.max(jnp.where(sub == off + r, rows8, NEG_INF), axis=0, keepdims=True)
                for r in range(blocks_per_tile)]

    def slc_step(p, h, n, bias, s_t, m, l8):
        parts = [s_t[r * SLC_BLOCK:(r + 1) * SLC_BLOCK] for r in range(blocks_per_tile)]
        cand = [_col_max(parts[r]) + bias[r] for r in range(blocks_per_tile)]
        m_new = functools.reduce(jnp.maximum, cand, m)
        p_t = jnp.concatenate([jnp.exp2(parts[r] - (m_new - bias[r])) for r in range(blocks_per_tile)],
                              axis=0)
        alpha = jnp.exp2(m - m_new)
        l8 = alpha * l8 + jnp.sum(p_t.reshape(-1, SUBLANES, TILE), axis=0)
        start = pl.multiple_of(n * TILE, TILE)
        v_t = vst_ref[0, kv_rows[h], pl.ds(start, TILE)]
        acc_s_ref[rows_of(p, h), :] = alpha * acc_s_ref[rows_of(p, h), :] + _dot(v_t, p_t.astype(BF16))
        return m_new, l8

    def scores_of(k_ref_, start):
        k = k_ref_[0, pl.ds(start, TILE), :]
        return [_dot_nt(k, scaled[c]) for c in range(len(heads))]

    state = []
    own_bias = [block_biases(h, i) for h in range(2)]
    own_scores = scores_of(ks_ref, own)
    for c, (p, h) in enumerate(heads):
        acc_s_ref[rows_of(p, h), :] = jnp.zeros((HEAD_DIM, TILE), F32)
        state.extend(slc_step(p, h, i, own_bias[h], jnp.where(causal, own_scores[c], NEG_INF), m0, l0))

    def past_tile(n, carry):
        scores = scores_of(ks_ref, pl.multiple_of(n * TILE, TILE))
        bias = [block_biases(h, n) for h in range(2)]
        out = []
        for c, (p, h) in enumerate(heads):
            out.extend(slc_step(p, h, n, bias[h], scores[c], carry[2 * c], carry[2 * c + 1]))
        return tuple(out)

    state = lax.fori_loop(0, i, past_tile, tuple(state))

    wstate = [(m0, l0)] * len(heads)
    for back, mask in ((0, causal), (1, None), (2, key_i > qry_i)):
        start = pl.multiple_of(jnp.maximum(i - back, 0) * TILE, TILE)
        scores = scores_of(kw_ref, start)
        exists = jnp.where(i >= back, 0.0, NEG_INF) + zero
        for c, (p, h) in enumerate(heads):
            if back == 0:
                acc_w_ref[rows_of(p, h), :] = jnp.zeros((HEAD_DIM, TILE), F32)
            s_t = scores[c] if mask is None else jnp.where(mask, scores[c], NEG_INF)
            wstate[c] = _flash_step(s_t, exists, vwt_ref[0, kv_rows[h], pl.ds(start, TILE)],
                                    wstate[c][0], wstate[c][1], acc_w_ref, rows_of(p, h))
    wstate = [l8 for (_, l8) in wstate]

    for p in range(B_PAIRS):
        o_slc = _flash_finish((state[4 * p + 1], state[4 * p + 3]), acc_s_ref.at[lanes_of(p)])
        o_win = _flash_finish((wstate[2 * p], wstate[2 * p + 1]), acc_w_ref.at[lanes_of(p)])
        g = _split_dot(gate_ref[0], egt_ref[p])
        out = (g[:, 0:LANES] * ocmp_ref[0, :, lanes_of(p)].astype(F32) + g[:, LANES:2 * LANES] * o_slc
               + g[:, 2 * LANES:3 * LANES] * o_win)
        o_ref[0, :, lanes_of(p)] = out.astype(BF16)


def _nsa_attn(qk, vvt, selb, ocmp, gate, eg, batch, seq):
    assert WINDOW == 2 * TILE
    k_spec = lambda t: pl.BlockSpec((1, seq, LANES), lambda b, i: (b, 0, t))
    vt_spec = lambda t: pl.BlockSpec((1, LANES, seq), lambda b, i: (b, t, 0))
    return pl.pallas_call(
        _nsa_attn_kernel,
        grid=(batch, seq // TILE),
        in_specs=[
            pl.BlockSpec((1, TILE, B_WIDTH), lambda b, i: (b, i, QK_BQ * LANES // B_WIDTH)),
            k_spec(QK_BKS),
            vt_spec(VV_BVS),
            k_spec(QK_BKW),
            vt_spec(VV_BVW),
            pl.BlockSpec((1, LANES, TILE), lambda b, i: (b, 0, i)),
            pl.BlockSpec((1, TILE, B_WIDTH), lambda b, i: (b, i, 0)),
            pl.BlockSpec((1, TILE, LANES), lambda b, i: (b, i, 0)),
            _const_spec((B_PAIRS, LANES, 3 * LANES)),
        ],
        out_specs=pl.BlockSpec((1, TILE, B_WIDTH), lambda b, i: (b, i, 0)),
        out_shape=jax.ShapeDtypeStruct((batch, seq, B_WIDTH), BF16),
        scratch_shapes=[pltpu.VMEM((B_WIDTH, TILE), F32), pltpu.VMEM((B_WIDTH, TILE), F32)],
        compiler_params=_params(2),
        name="nsa_select_window",
    )(qk, qk, vvt, qk, vvt, selb, ocmp, gate, eg)


def _top_rows(v, row_f, count, fill):
    vals, rows = [], []
    for _ in range(count):
        best = jnp.max(v, axis=0, keepdims=True)
        idx = jnp.min(jnp.where(v == best, row_f, fill), axis=0, keepdims=True)
        vals.append(best)
        rows.append(idx)
        v = jnp.where(row_f == idx, -jnp.inf, v)
    return vals, rows


def _merge_peer_kernel(x_ref, ya_ref, yb_ref, mg_ref, wa_ref, wb_ref, wo_ref, g2_ref, wq_ref,
                       k1h_ref, k1l_ref, k2h_ref, k2l_ref,
                       x1_ref, hn_ref, idx_ref, gt_ref, q_scr, ex_scr, gt_scr):
    ua = _dot(ya_ref[...], wa_ref[...])
    ub = _dot(yb_ref[...], wb_ref[...])
    merged = (mg_ref[:, 0:D_MODEL].astype(F32) * ua + mg_ref[:, D_MODEL:2 * D_MODEL].astype(F32) * ub)
    x1 = x_ref[...] + _dot(merged.astype(BF16), wo_ref[...])
    x1_ref[...] = x1
    hn = x1 * lax.rsqrt(jnp.mean(x1 * x1, axis=-1, keepdims=True) + NORM_EPS) * g2_ref[...]
    hn_ref[...] = hn
    q_scr[...] = _dot(hn.astype(BF16), wq_ref[...])

    key_row = lax.broadcasted_iota(I32, (PEER_NKEYS, TILE), 0).astype(F32)
    n_cand = sum(-(-(PEER_TOPK // (a + 1)) // SUBLANES) * SUBLANES for a in range(SUBLANES)) + SUBLANES
    cand_row = lax.broadcasted_iota(I32, (n_cand, TILE), 0).astype(F32)

    def head_scores(h):
        base = pl.multiple_of(h * PEER_QDIM, PEER_QDIM)
        q1h, q1l = _split(q_scr[:, pl.ds(base, PEER_HALF)])
        q2h, q2l = _split(q_scr[:, pl.ds(base + PEER_HALF, PEER_HALF)])
        k1h, k2h = k1h_ref[...], k2h_ref[...]
        s1 = _dot_nt(k1h, q1h) + _dot_nt(k1h, q1l) + _dot_nt(k1l_ref[...], q1h)
        s2 = _dot_nt(k2h, q2h) + _dot_nt(k2h, q2l) + _dot_nt(k2l_ref[...], q2h)
        return s1, s2

    def head_select(h, s1, s2):
        v1, i1 = _top_rows(s1, key_row, PEER_TOPK, float(PEER_NKEYS))
        v2, i2 = _top_rows(s2, key_row, PEER_TOPK, float(PEER_NKEYS))
        v1m, i1m = jnp.concatenate(v1, axis=0), jnp.concatenate(i1, axis=0)
        v2m, i2m = jnp.concatenate(v2, axis=0), jnp.concatenate(i2, axis=0)
        cand, cidx = [], []
        for a in range(SUBLANES):
            n_b = PEER_TOPK // (a + 1)
            rows = -(-n_b // SUBLANES) * SUBLANES
            keep = lax.broadcasted_iota(I32, (rows, TILE), 0) < n_b
            cand.append(jnp.where(keep, v1[a] + v2m[0:rows], -jnp.inf))
            cidx.append(i1[a] * float(PEER_NKEYS) + i2m[0:rows])
        cand.append(v1m[SUBLANES:] + v2[0])
        cidx.append(i1m[SUBLANES:] * float(PEER_NKEYS) + i2[0])
        cand = jnp.concatenate(cand, axis=0)
        cidx = jnp.concatenate(cidx, axis=0)
        vals, experts = [], []
        v = cand
        for _ in range(PEER_TOPK):
            best = jnp.max(v, axis=0, keepdims=True)
            at = jnp.min(jnp.where(v == best, cand_row, float(n_cand)), axis=0, keepdims=True)
            hit = cand_row == at
            experts.append(jnp.max(jnp.where(hit, cidx, -1.0), axis=0, keepdims=True))
            vals.append(best)
            v = jnp.where(hit, -jnp.inf, v)
        vals = jnp.concatenate(vals, axis=0)
        e = jnp.exp(vals - vals[0:1])
        out_row = pl.multiple_of(h * PEER_TOPK, PEER_TOPK)
        gt_scr[pl.ds(out_row, PEER_TOPK), :] = e / jnp.sum(e, axis=0, keepdims=True)
        ex_scr[pl.ds(out_row, PEER_TOPK), :] = jnp.concatenate(experts, axis=0)

    def head_group(i, carry):
        hs = [i * PEER_HEAD_UNROLL + k for k in range(PEER_HEAD_UNROLL)]
        scores = [head_scores(h) for h in hs]
        for h, (s1, s2) in zip(hs, scores):
            head_select(h, s1, s2)
        return carry

    lax.fori_loop(0, PEER_HEADS // PEER_HEAD_UNROLL, head_group, 0)
    gt_ref[...] = jnp.transpose(gt_scr[...])
    idx_ref[...] = (jnp.transpose(ex_scr[...]) * float(HALF_ROWS)).astype(I32)


def _merge_peer(x2d, ya, yb, mg, wa, wb, wo, g2, wq, k1h, k1l, k2h, k2l):
    tokens = x2d.shape[0]
    row = lambda i: (i, 0)
    return pl.pallas_call(
        _merge_peer_kernel,
        grid=(tokens // TILE,),
        in_specs=[
            pl.BlockSpec((TILE, D_MODEL), row),
            pl.BlockSpec((TILE, A_WIDTH), row),
            pl.BlockSpec((TILE, B_WIDTH), row),
            pl.BlockSpec((TILE, MG_COLS), row),
            _const_spec((A_WIDTH, D_MODEL)),
            _const_spec((B_WIDTH, D_MODEL)),
            _const_spec((D_MODEL, D_MODEL)),
            _const_spec((1, D_MODEL)),
            _const_spec((D_MODEL, PEER_HEADS * PEER_QDIM)),
            _const_spec((PEER_NKEYS, PEER_HALF)),
            _const_spec((PEER_NKEYS, PEER_HALF)),
            _const_spec((PEER_NKEYS, PEER_HALF)),
            _const_spec((PEER_NKEYS, PEER_HALF)),
        ],
        out_specs=[
            pl.BlockSpec((TILE, D_MODEL), row),
            pl.BlockSpec((TILE, D_MODEL), row),
            pl.BlockSpec((TILE, PEER_SLOTS), row),
            pl.BlockSpec((TILE, PEER_SLOTS), row),
        ],
        out_shape=[
            jax.ShapeDtypeStruct((tokens, D_MODEL), F32),
            jax.ShapeDtypeStruct((tokens, D_MODEL), F32),
            jax.ShapeDtypeStruct((tokens, PEER_SLOTS), I32),
            jax.ShapeDtypeStruct((tokens, PEER_SLOTS), F32),
        ],
        scratch_shapes=[pltpu.VMEM((TILE, PEER_HEADS * PEER_QDIM), F32),
                        pltpu.VMEM((PEER_SLOTS, TILE), F32),
                        pltpu.VMEM((PEER_SLOTS, TILE), F32)],
        compiler_params=_params(1),
        name="merge_peer_topk",
    )(x2d, ya, yb, mg, wa, wb, wo, g2, wq, k1h, k1l, k2h, k2l)


HALF_ROWS = SUBLANES // 2
HI_MASK = -65536
ROW_MASK = 65535
PEER_UNROLL = 4
PEER_OUT_UNROLL = 2


def _unpack(words):
    hi = pltpu.bitcast(words & HI_MASK, F32)
    lo = pltpu.bitcast(words << 16, F32)
    return hi, lo


def _expert_row(tab_ref, row):
    return tab_ref[pl.ds(pl.multiple_of(row, HALF_ROWS), HALF_ROWS), :]


def _fold_rows(v, shift):
    return v + pltpu.roll(v, shift, axis=0)


def _peer_u_kernel(row_ref, rowv_ref, hn_ref, gt_ref, tab_ref, route_ref):
    lane = lax.broadcasted_iota(I32, (SUBLANES, LANES), 1)
    sub = lax.broadcasted_iota(I32, (SUBLANES, LANES), 0)
    keep_hi_pair = (sub % 4) >= 2
    odd = (sub % 2) == 1
    groups = PEER_SLOTS // SUBLANES

    def one_token(t, accs):
        x = hn_ref[t]
        xa = jnp.concatenate([x[0:HALF_ROWS], x[0:HALF_ROWS]], axis=0)
        xb = jnp.concatenate([x[HALF_ROWS:], x[HALF_ROWS:]], axis=0)

        def pair(ja, jb):
            words = jnp.concatenate([_expert_row(tab_ref, row_ref[t, ja]),
                                     _expert_row(tab_ref, row_ref[t, jb])], axis=0)
            hi, lo = _unpack(words)
            return hi * xa + lo * xb

        new = []
        for g in range(groups):
            j = g * SUBLANES
            quads = []
            for (a, b, c, d) in ((j + 3, j + 7, j + 1, j + 5), (j + 2, j + 6, j, j + 4)):
                w_ab = _fold_rows(pair(a, b), 2)
                w_cd = _fold_rows(pair(c, d), 2)
                z = jnp.where(keep_hi_pair, w_ab, pltpu.roll(w_cd, 6, axis=0))
                quads.append(_fold_rows(z, 1))
            folded = jnp.where(odd, quads[0], pltpu.roll(quads[1], 7, axis=0))
            total = jnp.sum(folded, axis=1, keepdims=True)
            new.append(jnp.where(lane == t, total, accs[g]))
        return tuple(new)

    def step(i, accs):
        for k in range(PEER_UNROLL):
            accs = one_token(i * PEER_UNROLL + k, accs)
        return accs

    init = tuple(jnp.zeros((SUBLANES, LANES), F32) for _ in range(groups))
    accs = lax.fori_loop(0, PEER_TB // PEER_UNROLL, step, init)
    a = jnp.transpose(jnp.concatenate(accs, axis=0))
    w = (_gelu_tanh(a) * gt_ref[...]).astype(BF16).astype(F32)
    route_ref[...] = (pltpu.bitcast(w, I32) & HI_MASK) | rowv_ref[...]


def _peer_u(rows, hn3, gt, tab):
    tokens = hn3.shape[0]
    blk = lambda i: (i, 0)
    return pl.pallas_call(
        _peer_u_kernel,
        grid=(tokens // PEER_TB,),
        in_specs=[
            pl.BlockSpec((PEER_TB, PEER_SLOTS), blk, memory_space=pltpu.SMEM),
            pl.BlockSpec((PEER_TB, PEER_SLOTS), blk),
            pl.BlockSpec((PEER_TB, SUBLANES, LANES), lambda i: (i, 0, 0)),
            pl.BlockSpec((PEER_TB, PEER_SLOTS), blk),
            _const_spec((PEER_N * HALF_ROWS, LANES)),
        ],
        out_specs=pl.BlockSpec((PEER_TB, PEER_SLOTS), blk),
        out_shape=jax.ShapeDtypeStruct((tokens, PEER_SLOTS), I32),
        compiler_params=_params(1),
        name="peer_expert_in",
    )(rows, rows, hn3, gt, tab)


def _peer_v_kernel(route_ref, x1_ref, tab_ref, o_ref):
    def one_token(t):
        acc_hi = jnp.zeros((HALF_ROWS, LANES), F32)
        acc_lo = jnp.zeros((HALF_ROWS, LANES), F32)
        tok_ref = route_ref.at[t]
        for j in range(PEER_SLOTS):
            word = tok_ref[j]
            hi, lo = _unpack(_expert_row(tab_ref, word & ROW_MASK))
            w = pltpu.bitcast(jnp.full((HALF_ROWS, LANES), word, I32) & HI_MASK, F32)
            acc_hi = acc_hi + w * hi
            acc_lo = acc_lo + w * lo
        o_ref[t] = x1_ref[t] + jnp.concatenate([acc_hi, acc_lo], axis=0)

    def step(i, carry):
        for k in range(PEER_OUT_UNROLL):
            one_token(i * PEER_OUT_UNROLL + k)
        return carry

    lax.fori_loop(0, PEER_TB // PEER_OUT_UNROLL, step, 0)


def _peer_v(route, x13, tab):
    tokens = x13.shape[0]
    return pl.pallas_call(
        _peer_v_kernel,
        grid=(tokens // PEER_TB,),
        in_specs=[
            pl.BlockSpec((PEER_TB, PEER_SLOTS), lambda i: (i, 0), memory_space=pltpu.SMEM),
            pl.BlockSpec((PEER_TB, SUBLANES, LANES), lambda i: (i, 0, 0)),
            _const_spec((PEER_N * HALF_ROWS, LANES)),
        ],
        out_specs=pl.BlockSpec((PEER_TB, SUBLANES, LANES), lambda i: (i, 0, 0)),
        out_shape=jax.ShapeDtypeStruct((tokens, SUBLANES, LANES), F32),
        compiler_params=_params(1),
        name="peer_expert_out",
    )(route, x13, tab)


def _pack_table(t):
    bits = lax.bitcast_convert_type(t.astype(BF16), jnp.uint16).astype(jnp.uint32)
    half = D_MODEL // 2
    words = (bits[:, :half] << 16) | bits[:, half:]
    return lax.bitcast_convert_type(words, I32).reshape(t.shape[0] * HALF_ROWS, LANES)


def _rope_tables(seq):
    inv = ROPE_THETA ** (-jnp.arange(0, ROT_DIM, 2, dtype=F32) / ROT_DIM)
    ang = jnp.arange(seq, dtype=F32)[:, None] * inv[None, :]
    d = np.arange(LANES) % HEAD_DIM
    cos = jnp.where(d[None, :] < ROT_DIM, jnp.cos(ang)[:, d % ROT_HALF], 1.0)
    sin = jnp.where(d[None, :] < ROT_DIM, jnp.sin(ang)[:, d % ROT_HALF], 0.0)
    return cos.astype(F32), sin.astype(F32)


def _compress_weights(w1, pos):
    out_w, out_p = [], []
    for part in range(2):
        wpart = w1[part * CMP_STRIDE * HEAD_DIM:(part + 1) * CMP_STRIDE * HEAD_DIM]
        wpart = wpart.reshape(CMP_STRIDE, HEAD_DIM, CMP_HIDDEN)
        full = jnp.einsum("ldj,gh->lgdhj", wpart, jnp.eye(B_KV_HEADS, dtype=F32))
        out_w.append(full.reshape(CMP_STRIDE * LANES, B_KV_HEADS * CMP_HIDDEN))
        ppart = pos[part * CMP_STRIDE:(part + 1) * CMP_STRIDE]
        prow = jnp.tile(ppart[:, None, :], (1, B_KV_HEADS, 1)).reshape(1, CMP_STRIDE * LANES)
        out_p.append(jnp.tile(prow, (SUBLANES, 1)))
    return jnp.stack(out_w).astype(BF16), jnp.stack(out_p).astype(BF16)


def kernel(x, norm1_g, w_in, b_merge, a_q_g, a_k_g, b_q_g, b_kc_g, b_ks_g, b_kw_g, cmp_pos_k,
           cmp_k_w1, cmp_k_b1, cmp_k_w2, cmp_k_b2, cmp_pos_v, cmp_v_w1, cmp_v_b1, cmp_v_w2,
           cmp_v_b2, w_up_a, w_up_b, w_out, norm2_g, peer_wq, peer_k1, peer_k2, peer_u, peer_v):
    batch, seq, _ = x.shape
    assert seq % TILE == 0 and seq // SLC_BLOCK <= HEAD_DIM
    tokens = batch * seq
    l = 0
    x2d = x.reshape(tokens, D_MODEL)

    w = w_in[l]
    sizes = (A_WIDTH, A_WIDTH, A_WIDTH, B_WIDTH) + (B_KV_WIDTH,) * 6 + (3 * B_HEADS, MG_COLS)
    offs = np.concatenate([[0], np.cumsum(sizes)])
    seg = lambda k: w[:, offs[k]:offs[k + 1]]
    aq, ak, av, bq, bkc, bvc, bks, bvs, bkw, bvw, bgate, mgate = (seg(k) for k in range(12))
    bq_perm = bq.reshape(D_MODEL, B_HEADS, HEAD_DIM)[:, np.array(B_HEAD_ORDER)].reshape(D_MODEL, B_WIDTH)
    w_rope = jnp.concatenate([aq, ak, bq_perm, bks, bkw], axis=1).astype(BF16)
    w_plain = jnp.concatenate([av, bvs, bvw], axis=1).T.astype(BF16)
    w_small = jnp.concatenate([bkc, bvc, jnp.pad(bgate, ((0, 0), (0, LANES - 3 * B_HEADS)))],
                              axis=1).astype(BF16)
    hg = jnp.concatenate([jnp.tile(a_q_g[l], A_HEADS), jnp.tile(a_k_g[l], A_HEADS),
                          jnp.tile(b_q_g[l], B_HEADS), jnp.tile(b_ks_g[l], B_KV_HEADS),
                          jnp.tile(b_kw_g[l], B_KV_HEADS)])[None, :]
    cos_t, sin_t = _rope_tables(seq)
    bd = jnp.asarray(np.kron(np.eye(LANES // HEAD_DIM), np.full((HEAD_DIM, HEAD_DIM), 1.0 / HEAD_DIM)),
                     BF16)

    qk, vvt, kcin, vcin, gate, mg, kmean = _inproj(
        x2d, norm1_g[l][None, :], w_rope, w_plain, w_small, mgate.astype(BF16), hg, cos_t, sin_t,
        jnp.kron(jnp.eye(2, dtype=BF16), bd), b_merge[l].reshape(1, MG_COLS), seq)
    qk = qk.reshape(batch, seq, ROPE_COLS)

    n_blk = seq // A_BLOCK
    kmean = jnp.pad(kmean.reshape(batch, n_blk, A_WIDTH), ((0, 0), (0, LANES - n_blk), (0, 0)))
    km_hi = kmean.astype(BF16)
    km_lo = (kmean - km_hi.astype(F32)).astype(BF16)
    ya = _moba(qk, vvt, km_hi, km_lo, batch, seq)

    n_rows = seq // CMP_STRIDE
    wk1, pk = _compress_weights(cmp_k_w1[l], cmp_pos_k[l])
    wv1, pv = _compress_weights(cmp_v_w1[l], cmp_pos_v[l])
    blockdiag = lambda m: jnp.kron(jnp.eye(B_KV_HEADS, dtype=F32), m)
    w2 = jnp.stack([blockdiag(cmp_k_w2[l]), blockdiag(cmp_v_w2[l])]).astype(BF16)
    b1 = jnp.stack([jnp.tile(cmp_k_b1[l], B_KV_HEADS), jnp.tile(cmp_v_b1[l], B_KV_HEADS)])[:, None, :]
    b2 = jnp.stack([jnp.tile(cmp_k_b2[l], B_KV_HEADS), jnp.tile(cmp_v_b2[l], B_KV_HEADS)])[:, None, :]
    kc, vc = _compress(kcin.reshape(batch, n_rows, CMP_STRIDE * LANES),
                       vcin.reshape(batch, n_rows, CMP_STRIDE * LANES),
                       jnp.stack([wk1, wv1]), jnp.stack([pk, pv]), b1, w2, b2, bd,
                       jnp.tile(b_kc_g[l], B_KV_HEADS)[None, :], batch)

    n_cmp = (seq - CMP_LEN) // CMP_STRIDE + 1
    n_slc = seq // SLC_BLOCK
    ci = np.arange(n_rows)[:, None]
    sj = np.arange(HEAD_DIM)[None, :]
    ov = ((ci * CMP_STRIDE < (sj + 1) * SLC_BLOCK) & (ci * CMP_STRIDE + CMP_LEN > sj * SLC_BLOCK)
          & (ci < n_cmp) & (sj < n_slc)).astype(np.float32)
    ov = jnp.asarray(ov.T, BF16)
    ocmp, selb = _nsa_cmp(qk, kc, vc, ov, batch, seq)

    eg = np.zeros((B_PAIRS, LANES, 3 * LANES), np.float32)
    for p in range(B_PAIRS):
        for half in range(2):
            head = B_HEAD_ORDER[2 * p + half]
            for c in range(3):
                eg[p, head * 3 + c, c * LANES + half * HEAD_DIM:c * LANES + (half + 1) * HEAD_DIM] = 1.0
    yb = _nsa_attn(qk, vvt, selb, ocmp, gate.reshape(batch, seq, LANES), jnp.asarray(eg, BF16),
                   batch, seq)

    wb_perm = w_up_b[l].reshape(B_HEADS, HEAD_DIM, D_MODEL)[np.array(B_HEAD_ORDER)].reshape(
        B_WIDTH, D_MODEL)
    k1h, k1l = _split(peer_k1[l])
    k2h, k2l = _split(peer_k2[l])
    x1, hn, rows, gt = _merge_peer(
        x2d, ya.reshape(tokens, A_WIDTH), yb.reshape(tokens, B_WIDTH), mg,
        w_up_a[l].astype(BF16), wb_perm.astype(BF16), w_out[l].astype(BF16),
        norm2_g[l][None, :], peer_wq[l].astype(BF16), k1h, k1l, k2h, k2l)

    route = _peer_u(rows, hn.reshape(tokens, SUBLANES, LANES), gt, _pack_table(peer_u[l]))
    out = _peer_v(route, x1.reshape(tokens, SUBLANES, LANES), _pack_table(peer_v[l]))
    return out.reshape(batch, seq, D_MODEL)
```

```python
import functools
import math

import jax
import jax.numpy as jnp
import numpy as np
from jax import lax
from jax.experimental import pallas as pl
from jax.experimental.pallas import tpu as pltpu

D_MODEL = 1024
HEAD_DIM = 64
ROT_DIM = HEAD_DIM // 4
ROT_HALF = ROT_DIM // 2
ROPE_THETA = 500000.0
NORM_EPS = 1e-6
NEG_INF = -1e30
BIG = 1e30

A_HEADS = 8
A_BLOCK = 256
A_TOPK = 3

B_HEADS = 8
B_KV_HEADS = 2
B_GROUP = B_HEADS // B_KV_HEADS
CMP_LEN = 32
CMP_STRIDE = 16
CMP_HIDDEN = 256
SLC_BLOCK = 64
SLC_TOPN = 16
WINDOW = 512

PEER_HEADS = 8
PEER_NKEYS = 128
PEER_N = PEER_NKEYS * PEER_NKEYS
PEER_QDIM = 256
PEER_HALF = PEER_QDIM // 2
PEER_TOPK = 16
PEER_SLOTS = PEER_HEADS * PEER_TOPK
PEER_HEAD_UNROLL = 2

A_WIDTH = A_HEADS * HEAD_DIM
B_WIDTH = B_HEADS * HEAD_DIM
B_KV_WIDTH = B_KV_HEADS * HEAD_DIM
N_BRANCH = 2

LANES = 128
SUBLANES = 8
VMEM_LIMIT_BYTES = 56 * 1024 * 1024

ROPE_COLS = 2 * A_WIDTH + B_WIDTH + 2 * B_KV_WIDTH
QK_AQ, QK_AK, QK_BQ, QK_BKS, QK_BKW = 0, 4, 8, 12, 13
PLAIN_COLS = A_WIDTH + 2 * B_KV_WIDTH
VV_AV, VV_BVS, VV_BVW = 0, 4, 5
MG_COLS = N_BRANCH * D_MODEL
B_PAIRS = B_WIDTH // LANES
B_HEAD_ORDER = tuple(h for p in range(B_PAIRS) for h in (p, B_GROUP + p))

TILE = 256
PEER_TB = 128
F32 = jnp.float32
BF16 = jnp.bfloat16
I32 = jnp.int32


def _const_spec(shape):
    n = len(shape)
    return pl.BlockSpec(shape, lambda *_: (0,) * n, pipeline_mode=pl.Buffered(1))


def _params(n_axes):
    return pltpu.CompilerParams(dimension_semantics=("arbitrary",) * n_axes,
                                vmem_limit_bytes=VMEM_LIMIT_BYTES)


def _dot(a, b):
    return jnp.dot(a, b, preferred_element_type=F32)


def _dot_nt(a, b):
    return lax.dot_general(a, b, (((1,), (1,)), ((), ())), preferred_element_type=F32)


def _split(a_f32):
    hi = a_f32.astype(BF16)
    lo = (a_f32 - hi.astype(F32)).astype(BF16)
    return hi, lo


def _split_dot(a_f32, b_bf16):
    hi, lo = _split(a_f32)
    return _dot(hi, b_bf16) + _dot(lo, b_bf16)


def _gelu_tanh(x):
    return 0.5 * x * (1.0 + jnp.tanh(math.sqrt(2.0 / math.pi) * (x + 0.044715 * (x * x * x))))


def _group_rmsnorm(y, bd, gain):
    ms = _split_dot(y * y, bd)
    return y * lax.rsqrt(ms + NORM_EPS) * gain


def _inproj_kernel(x_ref, g1_ref, w_rope_ref, w_plain_ref, w_small_ref, w_mg_ref, hg_ref,
                   cos_ref, sin_ref, bd_ref, bm_ref,
                   qk_ref, vvt_ref, kcin_ref, vcin_ref, gate_ref, mg_ref, kmean_ref):
    x = x_ref[...]
    h = x * lax.rsqrt(jnp.mean(x * x, axis=-1, keepdims=True) + NORM_EPS) * g1_ref[...]
    hb = h.astype(BF16)

    y_all = _dot(hb, w_rope_ref[...])
    vvt_ref[0] = _dot_nt(w_plain_ref[...], hb).astype(BF16)
    small = _dot(hb, w_small_ref[...])
    kcin_ref[...] = small[:, 0:LANES].astype(BF16)
    vcin_ref[...] = small[:, LANES:2 * LANES].astype(BF16)
    gate_ref[...] = jax.nn.sigmoid(small[:, 2 * LANES:3 * LANES])
    mg_ref[...] = jax.nn.sigmoid(_dot(hb, w_mg_ref[...]) + bm_ref[...]).astype(BF16)

    cos = cos_ref[...]
    sin = sin_ref[...]
    bd = bd_ref[...]
    lane = lax.broadcasted_iota(I32, (TILE, LANES), 1)
    first = (lane % HEAD_DIM) < ROT_HALF
    wide = 2 * LANES
    for t2 in range(ROPE_COLS // wide):
        y2 = y_all[:, t2 * wide:(t2 + 1) * wide]
        ms2 = _split_dot(y2 * y2, bd)
        for half in range(2):
            t = 2 * t2 + half
            sl = slice(t * LANES, (t + 1) * LANES)
            hs = slice(half * LANES, (half + 1) * LANES)
            yn = y2[:, hs] * lax.rsqrt(ms2[:, hs] + NORM_EPS) * hg_ref[:, sl]
            up = pltpu.roll(yn, LANES - ROT_HALF, axis=1)
            dn = pltpu.roll(yn, ROT_HALF, axis=1)
            out = yn * cos + jnp.where(first, -up, dn) * sin
            qk_ref[:, sl] = out.astype(BF16)
            if QK_AK <= t < QK_AK + A_WIDTH // LANES:
                c = t - QK_AK
                kmean_ref[0, :, c * LANES:(c + 1) * LANES] = jnp.mean(out, axis=0, keepdims=True)


def _inproj(x2d, g1, w_rope, w_plain, w_small, w_mg, hg, cos_t, sin_t, bd2, bm, seq):
    tokens = x2d.shape[0]
    nt = tokens // TILE
    tiles_per_seq = seq // TILE
    row = lambda i: (i, 0)
    pos = lambda i: (i % tiles_per_seq, 0)
    return pl.pallas_call(
        _inproj_kernel,
        grid=(nt,),
        in_specs=[
            pl.BlockSpec((TILE, D_MODEL), row),
            _const_spec((1, D_MODEL)),
            _const_spec((D_MODEL, ROPE_COLS)),
            _const_spec((PLAIN_COLS, D_MODEL)),
            _const_spec((D_MODEL, 3 * LANES)),
            _const_spec((D_MODEL, MG_COLS)),
            _const_spec((1, ROPE_COLS)),
            pl.BlockSpec((TILE, LANES), pos),
            pl.BlockSpec((TILE, LANES), pos),
            _const_spec((2 * LANES, 2 * LANES)),
            _const_spec((1, MG_COLS)),
        ],
        out_specs=[
            pl.BlockSpec((TILE, ROPE_COLS), row),
            pl.BlockSpec((1, PLAIN_COLS, TILE), lambda i: (i // tiles_per_seq, 0, i % tiles_per_seq)),
            pl.BlockSpec((TILE, LANES), row),
            pl.BlockSpec((TILE, LANES), row),
            pl.BlockSpec((TILE, LANES), row),
            pl.BlockSpec((TILE, MG_COLS), row),
            pl.BlockSpec((1, 1, A_WIDTH), lambda i: (i, 0, 0)),
        ],
        out_shape=[
            jax.ShapeDtypeStruct((tokens, ROPE_COLS), BF16),
            jax.ShapeDtypeStruct((tokens // seq, PLAIN_COLS, seq), BF16),
            jax.ShapeDtypeStruct((tokens, LANES), BF16),
            jax.ShapeDtypeStruct((tokens, LANES), BF16),
            jax.ShapeDtypeStruct((tokens, LANES), F32),
            jax.ShapeDtypeStruct((tokens, MG_COLS), BF16),
            jax.ShapeDtypeStruct((nt, 1, A_WIDTH), F32),
        ],
        compiler_params=_params(1),
        name="inproj",
    )(x2d, g1, w_rope, w_plain, w_small, w_mg, hg, cos_t, sin_t, bd2, bm)


LOG2E = math.log2(math.e)


def _scaled_halves(q_pair):
    lane = lax.broadcasted_iota(I32, (TILE, LANES), 1)
    q = q_pair.astype(F32)
    c = HEAD_DIM ** -0.5 * LOG2E
    plain = [jnp.where((lane // HEAD_DIM) == h, q, 0.0).astype(BF16) for h in range(2)]
    scaled = [jnp.where((lane // HEAD_DIM) == h, q * c, 0.0).astype(BF16) for h in range(2)]
    return plain, scaled


def _col_max(s_t):
    part = jnp.max(s_t.reshape(-1, SUBLANES, s_t.shape[-1]), axis=0)
    return jnp.max(part, axis=0, keepdims=True)


def _values_and_sum(v_t, p_t):
    ones = jnp.ones((2 * SUBLANES, v_t.shape[1]), BF16)
    res = _dot(jnp.concatenate([v_t, ones], axis=0), p_t.astype(BF16))
    return res[0:HEAD_DIM], res[HEAD_DIM:HEAD_DIM + SUBLANES]


def _flash_step(s_t, shift_bias, v_t, m, l8, acc_ref, rows):
    m_new = jnp.maximum(m, _col_max(s_t) + shift_bias)
    p_t = jnp.exp2(s_t - (m_new - shift_bias))
    alpha = jnp.exp2(m - m_new)
    pv, psum = _values_and_sum(v_t, p_t)
    l8 = alpha * l8 + psum
    acc_ref[rows, :] = alpha * acc_ref[rows, :] + pv
    return m_new, l8


def _flash_finish(l8s, acc_ref):
    inv = [1.0 / l8[0:1] for l8 in l8s]
    out_t = jnp.concatenate([acc_ref[0:HEAD_DIM, :] * inv[0], acc_ref[HEAD_DIM:, :] * inv[1]], axis=0)
    return jnp.transpose(out_t)


def _moba_kernel(q_ref, k_ref, vt_ref, kmh_ref, kml_ref, o_ref, acc_ref, *, topk, gate_rows):
    i = pl.program_id(1)
    pairs = A_WIDTH // LANES
    heads = [(p, h) for p in range(pairs) for h in range(2)]
    key_i = lax.broadcasted_iota(I32, (TILE, TILE), 0)
    qry_i = lax.broadcasted_iota(I32, (TILE, TILE), 1)
    causal = key_i <= qry_i
    blk = lax.broadcasted_iota(I32, (gate_rows, TILE), 0)
    blk_f = blk.astype(F32)
    own = pl.multiple_of(i * TILE, TILE)
    lanes_of = lambda p: slice(p * LANES, (p + 1) * LANES)
    rows_of = lambda p, h: slice((2 * p + h) * HEAD_DIM, (2 * p + h + 1) * HEAD_DIM)
    zero = jnp.zeros((1, TILE), F32)
    m0 = jnp.full((1, TILE), NEG_INF, F32)
    l0 = jnp.zeros((SUBLANES, TILE), F32)

    plain, scaled = [], []
    for p in range(pairs):
        plain_p, scaled_p = _scaled_halves(q_ref[0, :, lanes_of(p)])
        plain.extend(plain_p)
        scaled.extend(scaled_p)
    gates = [(_dot_nt(kmh_ref[0, :, lanes_of(p)], plain[c])
              + _dot_nt(kml_ref[0, :, lanes_of(p)], plain[c]))[0:gate_rows]
             for c, (p, h) in enumerate(heads)]
    own_scores = [_dot_nt(k_ref[0, pl.ds(own, TILE), lanes_of(p)], scaled[c])
                  for c, (p, h) in enumerate(heads)]

    biases, state = [], []
    for c, (p, h) in enumerate(heads):
        gate = jnp.where(blk < i, gates[c], NEG_INF)
        bias = jnp.full((gate_rows, TILE), NEG_INF, F32)
        for _ in range(topk):
            best = jnp.max(gate, axis=0, keepdims=True)
            idx = jnp.min(jnp.where(gate == best, blk_f, float(gate_rows)), axis=0, keepdims=True)
            pick = blk_f == idx
            bias = jnp.where(pick, jnp.where(best > 0.5 * NEG_INF, 0.0, bias), bias)
            gate = jnp.where(pick, NEG_INF, gate)
        biases.append(bias)
        acc_ref[rows_of(p, h), :] = jnp.zeros((HEAD_DIM, TILE), F32)
        state.extend(_flash_step(jnp.where(causal, own_scores[c], NEG_INF), zero,
                                 vt_ref[0, rows_of(p, h), pl.ds(own, TILE)], m0, l0,
                                 acc_ref, rows_of(p, h)))

    def past_block(n, carry):
        start = pl.multiple_of(n * TILE, TILE)
        out = []
        scores = [_dot_nt(k_ref[0, pl.ds(start, TILE), lanes_of(p)], scaled[c])
                  for c, (p, h) in enumerate(heads)]
        for c, (p, h) in enumerate(heads):
            b = jnp.max(jnp.where(blk == n, biases[c], NEG_INF), axis=0, keepdims=True)
            out.extend(_flash_step(scores[c], b, vt_ref[0, rows_of(p, h), pl.ds(start, TILE)],
                                   carry[2 * c], carry[2 * c + 1], acc_ref, rows_of(p, h)))
        return tuple(out)

    state = lax.fori_loop(0, i, past_block, tuple(state))
    for p in range(pairs):
        l8s = (state[4 * p + 1], state[4 * p + 3])
        o_ref[0, :, lanes_of(p)] = _flash_finish(l8s, acc_ref.at[lanes_of(p)]).astype(BF16)


def _moba(qk, vvt, kmean_hi, kmean_lo, batch, seq):
    n_blk = seq // A_BLOCK
    gate_rows = -(-n_blk // SUBLANES) * SUBLANES
    kernel = functools.partial(_moba_kernel, topk=min(A_TOPK, n_blk), gate_rows=gate_rows)
    return pl.pallas_call(
        kernel,
        grid=(batch, n_blk),
        in_specs=[
            pl.BlockSpec((1, TILE, A_WIDTH), lambda b, i: (b, i, QK_AQ * LANES // A_WIDTH)),
            pl.BlockSpec((1, seq, A_WIDTH), lambda b, i: (b, 0, QK_AK * LANES // A_WIDTH)),
            pl.BlockSpec((1, A_WIDTH, seq), lambda b, i: (b, VV_AV * LANES // A_WIDTH, 0)),
            pl.BlockSpec((1, LANES, A_WIDTH), lambda b, i: (b, 0, 0)),
            pl.BlockSpec((1, LANES, A_WIDTH), lambda b, i: (b, 0, 0)),
        ],
        out_specs=pl.BlockSpec((1, TILE, A_WIDTH), lambda b, i: (b, i, 0)),
        out_shape=jax.ShapeDtypeStruct((batch, seq, A_WIDTH), BF16),
        scratch_shapes=[pltpu.VMEM((A_WIDTH, TILE), F32)],
        compiler_params=_params(2),
        name="moba",
    )(qk, qk, vvt, kmean_hi, kmean_lo)


def _compress_kernel(kin_ref, vin_ref, w1_ref, pos_ref, b1_ref, w2_ref, b2_ref, bd_ref, g_ref,
                     kc_ref, vc_ref):
    n_rows = kin_ref.shape[1]
    for c, (in_ref, out_ref) in enumerate(((kin_ref, kc_ref), (vin_ref, vc_ref))):
        r = in_ref[0]
        first = _dot(r, w1_ref[c, 0])
        second = _dot(r, w1_ref[c, 1])
        const = (_dot(pos_ref[c, 0], w1_ref[c, 0]) + _dot(pos_ref[c, 1], w1_ref[c, 1]))[0:1]
        hid = first + pltpu.roll(second, n_rows - 1, axis=0) + const + b1_ref[c]
        out = _dot(_gelu_tanh(hid).astype(BF16), w2_ref[c]) + b2_ref[c]
        if c == 0:
            out = _group_rmsnorm(out, bd_ref[...], g_ref[...])
        out_ref[0] = out.astype(BF16)


def _compress(kin, vin, w1, pos, b1, w2, b2, bd, gain, batch):
    n_rows = kin.shape[1]
    blk = lambda b: (b, 0, 0)
    hid2 = B_KV_HEADS * CMP_HIDDEN
    return pl.pallas_call(
        _compress_kernel,
        grid=(batch,),
        in_specs=[
            pl.BlockSpec((1, n_rows, CMP_STRIDE * LANES), blk),
            pl.BlockSpec((1, n_rows, CMP_STRIDE * LANES), blk),
            _const_spec((2, 2, CMP_STRIDE * LANES, hid2)),
            _const_spec((2, 2, SUBLANES, CMP_STRIDE * LANES)),
            _const_spec((2, 1, hid2)),
            _const_spec((2, hid2, LANES)),
            _const_spec((2, 1, LANES)),
            _const_spec((LANES, LANES)),
            _const_spec((1, LANES)),
        ],
        out_specs=[pl.BlockSpec((1, n_rows, LANES), blk), pl.BlockSpec((1, n_rows, LANES), blk)],
        out_shape=[jax.ShapeDtypeStruct((batch, n_rows, LANES), BF16)] * 2,
        compiler_params=_params(1),
        name="nsa_compress",
    )(kin, vin, w1, pos, b1, w2, b2, bd, gain)


def _nsa_cmp_kernel(q_ref, kc_ref, vc_ref, ov_ref, ocmp_ref, selb_ref, *, n_cmp, topn):
    i = pl.program_id(1)
    scale = HEAD_DIM ** -0.5
    n_pad = kc_ref.shape[1]
    lane = lax.broadcasted_iota(I32, (TILE, LANES), 1)
    pos = i * TILE + lax.broadcasted_iota(I32, (TILE, 1), 0)
    ncol = lax.broadcasted_iota(I32, (TILE, n_pad), 1)
    visible = (ncol * CMP_STRIDE + (CMP_LEN - 1) <= pos) & (ncol < n_cmp)
    kc = kc_ref[0]
    vc = vc_ref[0]
    ov_t = ov_ref[...]
    heads = [(p, h) for p in range(B_PAIRS) for h in range(2)]
    qs = []
    for p in range(B_PAIRS):
        q_pair = q_ref[0, :, p * LANES:(p + 1) * LANES]
        qs.extend(jnp.where((lane // HEAD_DIM) == h, q_pair, jnp.zeros_like(q_pair)) for h in range(2))
    scores = [_dot_nt(q, kc) for q in qs]
    probs = []
    for s in scores:
        s = jnp.where(visible, s * scale, NEG_INF)
        e = jnp.where(visible, jnp.exp(s - jnp.max(s, axis=-1, keepdims=True)), 0.0)
        tot = jnp.sum(e, axis=-1, keepdims=True)
        probs.append(jnp.where(tot > 0.0, e / tot, 0.0))
    outs = [_dot(prob.astype(BF16), vc) for prob in probs]
    parts = []
    for prob in probs:
        hi, lo = _split(prob)
        parts.append(_dot_nt(ov_t, hi) + _dot_nt(ov_t, lo))
    for p in range(B_PAIRS):
        o_pair = jnp.where(lane < HEAD_DIM, outs[2 * p], outs[2 * p + 1])
        ocmp_ref[0, :, p * LANES:(p + 1) * LANES] = o_pair.astype(BF16)

    blk = lax.broadcasted_iota(I32, (HEAD_DIM, TILE), 0)
    own = (i * TILE + lax.broadcasted_iota(I32, (1, TILE), 1)) // SLC_BLOCK
    forced = (blk == 0) | (blk == own) | (blk == own - 1)
    for g in range(B_KV_HEADS):
        imp = functools.reduce(lambda a, b: a + b, [parts[c] for c, (p, h) in enumerate(heads) if h == g])
        imp = jnp.where(forced, BIG, imp)
        imp = jnp.where(blk <= own, imp, NEG_INF)
        rank = jnp.zeros((HEAD_DIM, TILE), F32)
        for k in range(HEAD_DIM):
            other = imp[k:k + 1]
            ahead = jnp.where(other > imp, 1.0, jnp.where(other == imp, jnp.where(k < blk, 1.0, 0.0), 0.0))
            rank = rank + ahead
        keep = jnp.where(rank < float(topn), jnp.where(imp > 0.5 * NEG_INF, 0.0, NEG_INF), NEG_INF)
        selb_ref[0, g * HEAD_DIM:(g + 1) * HEAD_DIM, :] = keep


def _nsa_cmp(qk, kc, vc, ov, batch, seq):
    n_pad = kc.shape[1]
    n_cmp = (seq - CMP_LEN) // CMP_STRIDE + 1
    topn = min(SLC_TOPN, seq // SLC_BLOCK)
    kernel = functools.partial(_nsa_cmp_kernel, n_cmp=n_cmp, topn=topn)
    return pl.pallas_call(
        kernel,
        grid=(batch, seq // TILE),
        in_specs=[
            pl.BlockSpec((1, TILE, B_WIDTH), lambda b, i: (b, i, QK_BQ * LANES // B_WIDTH)),
            pl.BlockSpec((1, n_pad, LANES), lambda b, i: (b, 0, 0)),
            pl.BlockSpec((1, n_pad, LANES), lambda b, i: (b, 0, 0)),
            _const_spec((HEAD_DIM, n_pad)),
        ],
        out_specs=[
            pl.BlockSpec((1, TILE, B_WIDTH), lambda b, i: (b, i, 0)),
            pl.BlockSpec((1, LANES, TILE), lambda b, i: (b, 0, i)),
        ],
        out_shape=[
            jax.ShapeDtypeStruct((batch, seq, B_WIDTH), BF16),
            jax.ShapeDtypeStruct((batch, LANES, seq), F32),
        ],
        compiler_params=_params(2),
        name="nsa_cmp_select",
    )(qk, kc, vc, ov)


def _nsa_attn_kernel(q_ref, ks_ref, vst_ref, kw_ref, vwt_ref, selb_ref, ocmp_ref, gate_ref, egt_ref,
                     o_ref, acc_s_ref, acc_w_ref):
    i = pl.program_id(1)
    blocks_per_tile = TILE // SLC_BLOCK
    heads = [(p, h) for p in range(B_PAIRS) for h in range(2)]
    key_i = lax.broadcasted_iota(I32, (TILE, TILE), 0)
    qry_i = lax.broadcasted_iota(I32, (TILE, TILE), 1)
    causal = key_i <= qry_i
    sub = lax.broadcasted_iota(I32, (SUBLANES, TILE), 0)
    own = pl.multiple_of(i * TILE, TILE)
    lanes_of = lambda p: slice(p * LANES, (p + 1) * LANES)
    rows_of = lambda p, h: slice((2 * p + h) * HEAD_DIM, (2 * p + h + 1) * HEAD_DIM)
    kv_rows = (slice(0, HEAD_DIM), slice(HEAD_DIM, LANES))
    zero = jnp.zeros((1, TILE), F32)
    m0 = jnp.full((1, TILE), NEG_INF, F32)
    l0 = jnp.zeros((SUBLANES, TILE), F32)
    scaled = []
    for p in range(B_PAIRS):
        scaled.extend(_scaled_halves(q_ref[0, :, lanes_of(p)])[1])

    def block_biases(h, n):
        first = h * HEAD_DIM + n * blocks_per_tile
        base = pl.multiple_of((first >> 3) << 3, SUBLANES)
        rows8 = selb_ref[0, pl.ds(base, SUBLANES), :]
        off = first - base
        return [jnp.max(jnp.where(sub == off + r, rows8, NEG_INF), axis=0, keepdims=True)
                for r in range(blocks_per_tile)]

    def slc_step(p, h, n, bias, s_t, m, l8):
        parts = [s_t[r * SLC_BLOCK:(r + 1) * SLC_BLOCK] for r in range(blocks_per_tile)]
        cand = [_col_max(parts[r]) + bias[r] for r in range(blocks_per_tile)]
        m_new = functools.reduce(jnp.maximum, cand, m)
        p_t = jnp.concatenate([jnp.exp2(parts[r] - (m_new - bias[r])) for r in range(blocks_per_tile)],
                              axis=0)
        alpha = jnp.exp2(m - m_new)
        start = pl.multiple_of(n * TILE, TILE)
        v_t = vst_ref[0, kv_rows[h], pl.ds(start, TILE)]
        pv, psum = _values_and_sum(v_t, p_t)
        acc_s_ref[rows_of(p, h), :] = alpha * acc_s_ref[rows_of(p, h), :] + pv
        return m_new, alpha * l8 + psum

    def scores_of(k_ref_, start):
        k = k_ref_[0, pl.ds(start, TILE), :]
        return [_dot_nt(k, scaled[c]) for c in range(len(heads))]

    state = []
    own_bias = [block_biases(h, i) for h in range(2)]
    own_scores = scores_of(ks_ref, own)
    for c, (p, h) in enumerate(heads):
        acc_s_ref[rows_of(p, h), :] = jnp.zeros((HEAD_DIM, TILE), F32)
        state.extend(slc_step(p, h, i, own_bias[h], jnp.where(causal, own_scores[c], NEG_INF), m0, l0))

    def past_tile(n, carry):
        scores = scores_of(ks_ref, pl.multiple_of(n * TILE, TILE))
        bias = [block_biases(h, n) for h in range(2)]
        out = []
        for c, (p, h) in enumerate(heads):
            out.extend(slc_step(p, h, n, bias[h], scores[c], carry[2 * c], carry[2 * c + 1]))
        return tuple(out)

    state = lax.fori_loop(0, i, past_tile, tuple(state))

    wstate = [(m0, l0)] * len(heads)
    for back, mask in ((0, causal), (1, None), (2, key_i > qry_i)):
        start = pl.multiple_of(jnp.maximum(i - back, 0) * TILE, TILE)
        scores = scores_of(kw_ref, start)
        exists = jnp.where(i >= back, 0.0, NEG_INF) + zero
        for c, (p, h) in enumerate(heads):
            if back == 0:
                acc_w_ref[rows_of(p, h), :] = jnp.zeros((HEAD_DIM, TILE), F32)
            s_t = scores[c] if mask is None else jnp.where(mask, scores[c], NEG_INF)
            wstate[c] = _flash_step(s_t, exists, vwt_ref[0, kv_rows[h], pl.ds(start, TILE)],
                                    wstate[c][0], wstate[c][1], acc_w_ref, rows_of(p, h))
    wstate = [l8 for (_, l8) in wstate]

    for p in range(B_PAIRS):
        o_slc = _flash_finish((state[4 * p + 1], state[4 * p + 3]), acc_s_ref.at[lanes_of(p)])
        o_win = _flash_finish((wstate[2 * p], wstate[2 * p + 1]), acc_w_ref.at[lanes_of(p)])
        g = _split_dot(gate_ref[0], egt_ref[p])
        out = (g[:, 0:LANES] * ocmp_ref[0, :, lanes_of(p)].astype(F32) + g[:, LANES:2 * LANES] * o_slc
               + g[:, 2 * LANES:3 * LANES] * o_win)
        o_ref[0, :, lanes_of(p)] = out.astype(BF16)


def _nsa_attn(qk, vvt, selb, ocmp, gate, eg, batch, seq):
    assert WINDOW == 2 * TILE
    k_spec = lambda t: pl.BlockSpec((1, seq, LANES), lambda b, i: (b, 0, t))
    vt_spec = lambda t: pl.BlockSpec((1, LANES, seq), lambda b, i: (b, t, 0))
    return pl.pallas_call(
        _nsa_attn_kernel,
        grid=(batch, seq // TILE),
        in_specs=[
            pl.BlockSpec((1, TILE, B_WIDTH), lambda b, i: (b, i, QK_BQ * LANES // B_WIDTH)),
            k_spec(QK_BKS),
            vt_spec(VV_BVS),
            k_spec(QK_BKW),
            vt_spec(VV_BVW),
            pl.BlockSpec((1, LANES, TILE), lambda b, i: (b, 0, i)),
            pl.BlockSpec((1, TILE, B_WIDTH), lambda b, i: (b, i, 0)),
            pl.BlockSpec((1, TILE, LANES), lambda b, i: (b, i, 0)),
            _const_spec((B_PAIRS, LANES, 3 * LANES)),
        ],
        out_specs=pl.BlockSpec((1, TILE, B_WIDTH), lambda b, i: (b, i, 0)),
        out_shape=jax.ShapeDtypeStruct((batch, seq, B_WIDTH), BF16),
        scratch_shapes=[pltpu.VMEM((B_WIDTH, TILE), F32), pltpu.VMEM((B_WIDTH, TILE), F32)],
        compiler_params=_params(2),
        name="nsa_select_window",
    )(qk, qk, vvt, qk, vvt, selb, ocmp, gate, eg)


def _top_rows(v, row_f, count, fill):
    vals, rows = [], []
    for _ in range(count):
        best = jnp.max(v, axis=0, keepdims=True)
        idx = jnp.min(jnp.where(v == best, row_f, fill), axis=0, keepdims=True)
        vals.append(best)
        rows.append(idx)
        v = jnp.where(row_f == idx, -jnp.inf, v)
    return vals, rows


def _store_token_tiles(ref, val):
    for c in range(D_MODEL // LANES):
        ref[pl.ds(c, TILE, stride=D_MODEL // LANES), :] = val[:, c * LANES:(c + 1) * LANES]


def _merge_peer_kernel(x_ref, ya_ref, yb_ref, mg_ref, wa_ref, wb_ref, wo_ref, g2_ref, wq_ref,
                       k1h_ref, k1l_ref, k2h_ref, k2l_ref,
                       x1_ref, hn_ref, idx_ref, gt_ref, q_scr, ex_scr, gt_scr):
    ua = _dot(ya_ref[...], wa_ref[...])
    ub = _dot(yb_ref[...], wb_ref[...])
    merged = (mg_ref[:, 0:D_MODEL].astype(F32) * ua + mg_ref[:, D_MODEL:2 * D_MODEL].astype(F32) * ub)
    x1 = x_ref[...] + _dot(merged.astype(BF16), wo_ref[...])
    _store_token_tiles(x1_ref, x1)
    hn = x1 * lax.rsqrt(jnp.mean(x1 * x1, axis=-1, keepdims=True) + NORM_EPS) * g2_ref[...]
    _store_token_tiles(hn_ref, hn)
    q_scr[...] = _dot(hn.astype(BF16), wq_ref[...])

    key_row = lax.broadcasted_iota(I32, (PEER_NKEYS, TILE), 0).astype(F32)
    n_cand = sum(-(-(PEER_TOPK // (a + 1)) // SUBLANES) * SUBLANES for a in range(SUBLANES)) + SUBLANES
    cand_row = lax.broadcasted_iota(I32, (n_cand, TILE), 0).astype(F32)

    def head_scores(h):
        base = pl.multiple_of(h * PEER_QDIM, PEER_QDIM)
        q1h, q1l = _split(q_scr[:, pl.ds(base, PEER_HALF)])
        q2h, q2l = _split(q_scr[:, pl.ds(base + PEER_HALF, PEER_HALF)])
        k1h, k2h = k1h_ref[...], k2h_ref[...]
        s1 = _dot_nt(k1h, q1h) + _dot_nt(k1h, q1l) + _dot_nt(k1l_ref[...], q1h)
        s2 = _dot_nt(k2h, q2h) + _dot_nt(k2h, q2l) + _dot_nt(k2l_ref[...], q2h)
        return s1, s2

    def head_select(h, s1, s2):
        v1, i1 = _top_rows(s1, key_row, PEER_TOPK, float(PEER_NKEYS))
        v2, i2 = _top_rows(s2, key_row, PEER_TOPK, float(PEER_NKEYS))
        v1m, i1m = jnp.concatenate(v1, axis=0), jnp.concatenate(i1, axis=0)
        v2m, i2m = jnp.concatenate(v2, axis=0), jnp.concatenate(i2, axis=0)
        cand, cidx = [], []
        for a in range(SUBLANES):
            n_b = PEER_TOPK // (a + 1)
            rows = -(-n_b // SUBLANES) * SUBLANES
            keep = lax.broadcasted_iota(I32, (rows, TILE), 0) < n_b
            cand.append(jnp.where(keep, v1[a] + v2m[0:rows], -jnp.inf))
            cidx.append(i1[a] * float(PEER_NKEYS) + i2m[0:rows])
        cand.append(v1m[SUBLANES:] + v2[0])
        cidx.append(i1m[SUBLANES:] * float(PEER_NKEYS) + i2[0])
        cand = jnp.concatenate(cand, axis=0)
        cidx = jnp.concatenate(cidx, axis=0)
        vals, experts = [], []
        v = cand
        for _ in range(PEER_TOPK):
            best = jnp.max(v, axis=0, keepdims=True)
            at = jnp.min(jnp.where(v == best, cand_row, float(n_cand)), axis=0, keepdims=True)
            hit = cand_row == at
            experts.append(jnp.max(jnp.where(hit, cidx, -1.0), axis=0, keepdims=True))
            vals.append(best)
            v = jnp.where(hit, -jnp.inf, v)
        vals = jnp.concatenate(vals, axis=0)
        e = jnp.exp(vals - vals[0:1])
        out_row = pl.multiple_of(h * PEER_TOPK, PEER_TOPK)
        gt_scr[pl.ds(out_row, PEER_TOPK), :] = e / jnp.sum(e, axis=0, keepdims=True)
        ex_scr[pl.ds(out_row, PEER_TOPK), :] = jnp.concatenate(experts, axis=0)

    def head_group(i, carry):
        hs = [i * PEER_HEAD_UNROLL + k for k in range(PEER_HEAD_UNROLL)]
        scores = [head_scores(h) for h in hs]
        for h, (s1, s2) in zip(hs, scores):
            head_select(h, s1, s2)
        return carry

    lax.fori_loop(0, PEER_HEADS // PEER_HEAD_UNROLL, head_group, 0)
    gt_ref[...] = jnp.transpose(gt_scr[...])
    idx_ref[...] = (jnp.transpose(ex_scr[...]) * float(HALF_ROWS)).astype(I32)


def _merge_peer(x2d, ya, yb, mg, wa, wb, wo, g2, wq, k1h, k1l, k2h, k2l):
    tokens = x2d.shape[0]
    row = lambda i: (i, 0)
    return pl.pallas_call(
        _merge_peer_kernel,
        grid=(tokens // TILE,),
        in_specs=[
            pl.BlockSpec((TILE, D_MODEL), row),
            pl.BlockSpec((TILE, A_WIDTH), row),
            pl.BlockSpec((TILE, B_WIDTH), row),
            pl.BlockSpec((TILE, MG_COLS), row),
            _const_spec((A_WIDTH, D_MODEL)),
            _const_spec((B_WIDTH, D_MODEL)),
            _const_spec((D_MODEL, D_MODEL)),
            _const_spec((1, D_MODEL)),
            _const_spec((D_MODEL, PEER_HEADS * PEER_QDIM)),
            _const_spec((PEER_NKEYS, PEER_HALF)),
            _const_spec((PEER_NKEYS, PEER_HALF)),
            _const_spec((PEER_NKEYS, PEER_HALF)),
            _const_spec((PEER_NKEYS, PEER_HALF)),
        ],
        out_specs=[
            pl.BlockSpec((TILE * SUBLANES, LANES), row),
            pl.BlockSpec((TILE * SUBLANES, LANES), row),
            pl.BlockSpec((TILE, PEER_SLOTS), row),
            pl.BlockSpec((TILE, PEER_SLOTS), row),
        ],
        out_shape=[
            jax.ShapeDtypeStruct((tokens * SUBLANES, LANES), F32),
            jax.ShapeDtypeStruct((tokens * SUBLANES, LANES), F32),
            jax.ShapeDtypeStruct((tokens, PEER_SLOTS), I32),
            jax.ShapeDtypeStruct((tokens, PEER_SLOTS), F32),
        ],
        scratch_shapes=[pltpu.VMEM((TILE, PEER_HEADS * PEER_QDIM), F32),
                        pltpu.VMEM((PEER_SLOTS, TILE), F32),
                        pltpu.VMEM((PEER_SLOTS, TILE), F32)],
        compiler_params=_params(1),
        name="merge_peer_topk",
    )(x2d, ya, yb, mg, wa, wb, wo, g2, wq, k1h, k1l, k2h, k2l)


HALF_ROWS = SUBLANES // 2
HI_MASK = -65536
ROW_MASK = 65535
PEER_UNROLL = 8
PEER_OUT_UNROLL = 2


def _unpack(words):
    hi = pltpu.bitcast(words & HI_MASK, F32)
    lo = pltpu.bitcast(words << 16, F32)
    return hi, lo


def _expert_row(tab_ref, row):
    return tab_ref[pl.ds(pl.multiple_of(row, HALF_ROWS), HALF_ROWS), :]


def _fold_rows(v, shift):
    return v + pltpu.roll(v, shift, axis=0)


def _peer_u_kernel(row_ref, rowv_ref, hn_ref, gt_ref, tab_ref, route_ref):
    lane = lax.broadcasted_iota(I32, (SUBLANES, LANES), 1)
    sub = lax.broadcasted_iota(I32, (SUBLANES, LANES), 0)
    keep_hi_pair = (sub % 4) >= 2
    odd = (sub % 2) == 1
    groups = PEER_SLOTS // SUBLANES

    def one_token(t, accs):
        x = hn_ref[t]
        xa = jnp.concatenate([x[0:HALF_ROWS], x[0:HALF_ROWS]], axis=0)
        xb = jnp.concatenate([x[HALF_ROWS:], x[HALF_ROWS:]], axis=0)

        def pair(ja, jb):
            words = jnp.concatenate([_expert_row(tab_ref, row_ref[t, ja]),
                                     _expert_row(tab_ref, row_ref[t, jb])], axis=0)
            hi, lo = _unpack(words)
            return hi * xa + lo * xb

        new = []
        for g in range(groups):
            j = g * SUBLANES
            quads = []
            for (a, b, c, d) in ((j + 3, j + 7, j + 1, j + 5), (j + 2, j + 6, j, j + 4)):
                w_ab = _fold_rows(pair(a, b), 2)
                w_cd = _fold_rows(pair(c, d), 2)
                z = jnp.where(keep_hi_pair, w_ab, pltpu.roll(w_cd, 6, axis=0))
                quads.append(_fold_rows(z, 1))
            folded = jnp.where(odd, quads[0], pltpu.roll(quads[1], 7, axis=0))
            total = jnp.sum(folded, axis=1, keepdims=True)
            new.append(jnp.where(lane == t, total, accs[g]))
        return tuple(new)

    def step(i, accs):
        for k in range(PEER_UNROLL):
            accs = one_token(i * PEER_UNROLL + k, accs)
        return accs

    init = tuple(jnp.zeros((SUBLANES, LANES), F32) for _ in range(groups))
    accs = lax.fori_loop(0, PEER_TB // PEER_UNROLL, step, init)
    a = jnp.transpose(jnp.concatenate(accs, axis=0))
    w = (_gelu_tanh(a) * gt_ref[...]).astype(BF16).astype(F32)
    route_ref[...] = (pltpu.bitcast(w, I32) & HI_MASK) | rowv_ref[...]


def _peer_u(rows, hn3, gt, tab):
    tokens = hn3.shape[0]
    blk = lambda i: (i, 0)
    return pl.pallas_call(
        _peer_u_kernel,
        grid=(tokens // PEER_TB,),
        in_specs=[
            pl.BlockSpec((PEER_TB, PEER_SLOTS), blk, memory_space=pltpu.SMEM),
            pl.BlockSpec((PEER_TB, PEER_SLOTS), blk),
            pl.BlockSpec((PEER_TB, SUBLANES, LANES), lambda i: (i, 0, 0)),
            pl.BlockSpec((PEER_TB, PEER_SLOTS), blk),
            _const_spec((PEER_N * HALF_ROWS, LANES)),
        ],
        out_specs=pl.BlockSpec((PEER_TB, PEER_SLOTS), blk),
        out_shape=jax.ShapeDtypeStruct((tokens, PEER_SLOTS), I32),
        compiler_params=_params(1),
        name="peer_expert_in",
    )(rows, rows, hn3, gt, tab)


def _peer_v_kernel(route_ref, x1_ref, tab_ref, o_ref):
    def one_token(t):
        acc_hi = jnp.zeros((HALF_ROWS, LANES), F32)
        acc_lo = jnp.zeros((HALF_ROWS, LANES), F32)
        tok_ref = route_ref.at[t]
        for j in range(PEER_SLOTS):
            word = tok_ref[j]
            hi, lo = _unpack(_expert_row(tab_ref, word & ROW_MASK))
            w = pltpu.bitcast(jnp.full((HALF_ROWS, LANES), word, I32) & HI_MASK, F32)
            acc_hi = acc_hi + w * hi
            acc_lo = acc_lo + w * lo
        o_ref[t] = x1_ref[t] + jnp.concatenate([acc_hi, acc_lo], axis=0)

    def step(i, carry):
        for k in range(PEER_OUT_UNROLL):
            one_token(i * PEER_OUT_UNROLL + k)
        return carry

    lax.fori_loop(0, PEER_TB // PEER_OUT_UNROLL, step, 0)


def _peer_v(route, x13, tab):
    tokens = x13.shape[0]
    return pl.pallas_call(
        _peer_v_kernel,
        grid=(tokens // PEER_TB,),
        in_specs=[
            pl.BlockSpec((PEER_TB, PEER_SLOTS), lambda i: (i, 0), memory_space=pltpu.SMEM),
            pl.BlockSpec((PEER_TB, SUBLANES, LANES), lambda i: (i, 0, 0)),
            _const_spec((PEER_N * HALF_ROWS, LANES)),
        ],
        out_specs=pl.BlockSpec((PEER_TB, SUBLANES, LANES), lambda i: (i, 0, 0)),
        out_shape=jax.ShapeDtypeStruct((tokens, SUBLANES, LANES), F32),
        compiler_params=_params(1),
        name="peer_expert_out",
    )(route, x13, tab)


def _pack_table(t):
    bits = lax.bitcast_convert_type(t.astype(BF16), jnp.uint16).astype(jnp.uint32)
    half = D_MODEL // 2
    words = (bits[:, :half] << 16) | bits[:, half:]
    return lax.bitcast_convert_type(words, I32).reshape(t.shape[0] * HALF_ROWS, LANES)


def _rope_tables(seq):
    inv = ROPE_THETA ** (-jnp.arange(0, ROT_DIM, 2, dtype=F32) / ROT_DIM)
    ang = jnp.arange(seq, dtype=F32)[:, None] * inv[None, :]
    d = np.arange(LANES) % HEAD_DIM
    cos = jnp.where(d[None, :] < ROT_DIM, jnp.cos(ang)[:, d % ROT_HALF], 1.0)
    sin = jnp.where(d[None, :] < ROT_DIM, jnp.sin(ang)[:, d % ROT_HALF], 0.0)
    return cos.astype(F32), sin.astype(F32)


def _compress_weights(w1, pos):
    out_w, out_p = [], []
    for part in range(2):
        wpart = w1[part * CMP_STRIDE * HEAD_DIM:(part + 1) * CMP_STRIDE * HEAD_DIM]
        wpart = wpart.reshape(CMP_STRIDE, HEAD_DIM, CMP_HIDDEN)
        full = jnp.einsum("ldj,gh->lgdhj", wpart, jnp.eye(B_KV_HEADS, dtype=F32))
        out_w.append(full.reshape(CMP_STRIDE * LANES, B_KV_HEADS * CMP_HIDDEN))
        ppart = pos[part * CMP_STRIDE:(part + 1) * CMP_STRIDE]
        prow = jnp.tile(ppart[:, None, :], (1, B_KV_HEADS, 1)).reshape(1, CMP_STRIDE * LANES)
        out_p.append(jnp.tile(prow, (SUBLANES, 1)))
    return jnp.stack(out_w).astype(BF16), jnp.stack(out_p).astype(BF16)


def kernel(x, norm1_g, w_in, b_merge, a_q_g, a_k_g, b_q_g, b_kc_g, b_ks_g, b_kw_g, cmp_pos_k,
           cmp_k_w1, cmp_k_b1, cmp_k_w2, cmp_k_b2, cmp_pos_v, cmp_v_w1, cmp_v_b1, cmp_v_w2,
           cmp_v_b2, w_up_a, w_up_b, w_out, norm2_g, peer_wq, peer_k1, peer_k2, peer_u, peer_v):
    batch, seq, _ = x.shape
    assert seq % TILE == 0 and seq // SLC_BLOCK <= HEAD_DIM
    tokens = batch * seq
    l = 0
    x2d = x.reshape(tokens, D_MODEL)

    w = w_in[l]
    sizes = (A_WIDTH, A_WIDTH, A_WIDTH, B_WIDTH) + (B_KV_WIDTH,) * 6 + (3 * B_HEADS, MG_COLS)
    offs = np.concatenate([[0], np.cumsum(sizes)])
    seg = lambda k: w[:, offs[k]:offs[k + 1]]
    aq, ak, av, bq, bkc, bvc, bks, bvs, bkw, bvw, bgate, mgate = (seg(k) for k in range(12))
    bq_perm = bq.reshape(D_MODEL, B_HEADS, HEAD_DIM)[:, np.array(B_HEAD_ORDER)].reshape(D_MODEL, B_WIDTH)
    w_rope = jnp.concatenate([aq, ak, bq_perm, bks, bkw], axis=1).astype(BF16)
    w_plain = jnp.concatenate([av, bvs, bvw], axis=1).T.astype(BF16)
    w_small = jnp.concatenate([bkc, bvc, jnp.pad(bgate, ((0, 0), (0, LANES - 3 * B_HEADS)))],
                              axis=1).astype(BF16)
    hg = jnp.concatenate([jnp.tile(a_q_g[l], A_HEADS), jnp.tile(a_k_g[l], A_HEADS),
                          jnp.tile(b_q_g[l], B_HEADS), jnp.tile(b_ks_g[l], B_KV_HEADS),
                          jnp.tile(b_kw_g[l], B_KV_HEADS)])[None, :]
    cos_t, sin_t = _rope_tables(seq)
    bd = jnp.asarray(np.kron(np.eye(LANES // HEAD_DIM), np.full((HEAD_DIM, HEAD_DIM), 1.0 / HEAD_DIM)),
                     BF16)

    qk, vvt, kcin, vcin, gate, mg, kmean = _inproj(
        x2d, norm1_g[l][None, :], w_rope, w_plain, w_small, mgate.astype(BF16), hg, cos_t, sin_t,
        jnp.kron(jnp.eye(2, dtype=BF16), bd), b_merge[l].reshape(1, MG_COLS), seq)
    qk = qk.reshape(batch, seq, ROPE_COLS)

    n_blk = seq // A_BLOCK
    kmean = jnp.pad(kmean.reshape(batch, n_blk, A_WIDTH), ((0, 0), (0, LANES - n_blk), (0, 0)))
    km_hi = kmean.astype(BF16)
    km_lo = (kmean - km_hi.astype(F32)).astype(BF16)
    ya = _moba(qk, vvt, km_hi, km_lo, batch, seq)

    n_rows = seq // CMP_STRIDE
    wk1, pk = _compress_weights(cmp_k_w1[l], cmp_pos_k[l])
    wv1, pv = _compress_weights(cmp_v_w1[l], cmp_pos_v[l])
    blockdiag = lambda m: jnp.kron(jnp.eye(B_KV_HEADS, dtype=F32), m)
    w2 = jnp.stack([blockdiag(cmp_k_w2[l]), blockdiag(cmp_v_w2[l])]).astype(BF16)
    b1 = jnp.stack([jnp.tile(cmp_k_b1[l], B_KV_HEADS), jnp.tile(cmp_v_b1[l], B_KV_HEADS)])[:, None, :]
    b2 = jnp.stack([jnp.tile(cmp_k_b2[l], B_KV_HEADS), jnp.tile(cmp_v_b2[l], B_KV_HEADS)])[:, None, :]
    kc, vc = _compress(kcin.reshape(batch, n_rows, CMP_STRIDE * LANES),
                       vcin.reshape(batch, n_rows, CMP_STRIDE * LANES),
                       jnp.stack([wk1, wv1]), jnp.stack([pk, pv]), b1, w2, b2, bd,
                       jnp.tile(b_kc_g[l], B_KV_HEADS)[None, :], batch)

    n_cmp = (seq - CMP_LEN) // CMP_STRIDE + 1
    n_slc = seq // SLC_BLOCK
    ci = np.arange(n_rows)[:, None]
    sj = np.arange(HEAD_DIM)[None, :]
    ov = ((ci * CMP_STRIDE < (sj + 1) * SLC_BLOCK) & (ci * CMP_STRIDE + CMP_LEN > sj * SLC_BLOCK)
          & (ci < n_cmp) & (sj < n_slc)).astype(np.float32)
    ov = jnp.asarray(ov.T, BF16)
    ocmp, selb = _nsa_cmp(qk, kc, vc, ov, batch, seq)

    eg = np.zeros((B_PAIRS, LANES, 3 * LANES), np.float32)
    for p in range(B_PAIRS):
        for half in range(2):
            head = B_HEAD_ORDER[2 * p + half]
            for c in range(3):
                eg[p, head * 3 + c, c * LANES + half * HEAD_DIM:c * LANES + (half + 1) * HEAD_DIM] = 1.0
    yb = _nsa_attn(qk, vvt, selb, ocmp, gate.reshape(batch, seq, LANES), jnp.asarray(eg, BF16),
                   batch, seq)

    wb_perm = w_up_b[l].reshape(B_HEADS, HEAD_DIM, D_MODEL)[np.array(B_HEAD_ORDER)].reshape(
        B_WIDTH, D_MODEL)
    k1h, k1l = _split(peer_k1[l])
    k2h, k2l = _split(peer_k2[l])
    x1, hn, rows, gt = _merge_peer(
        x2d, ya.reshape(tokens, A_WIDTH), yb.reshape(tokens, B_WIDTH), mg,
        w_up_a[l].astype(BF16), wb_perm.astype(BF16), w_out[l].astype(BF16),
        norm2_g[l][None, :], peer_wq[l].astype(BF16), k1h, k1l, k2h, k2l)

    route = _peer_u(rows, hn.reshape(tokens, SUBLANES, LANES), gt, _pack_table(peer_u[l]))
    out = _peer_v(route, x1.reshape(tokens, SUBLANES, LANES), _pack_table(peer_v[l]))
    return out.reshape(batch, seq, D_MODEL)
```

```python
import functools
import math

import jax
import jax.numpy as jnp
import numpy as np
from jax import lax
from jax.experimental import pallas as pl
from jax.experimental.pallas import tpu as pltpu

D_MODEL = 1024
HEAD_DIM = 64
ROT_DIM = HEAD_DIM // 4
ROT_HALF = ROT_DIM // 2
ROPE_THETA = 500000.0
NORM_EPS = 1e-6
NEG_INF = -1e30
BIG = 1e30

A_HEADS = 8
A_BLOCK = 256
A_TOPK = 3

B_HEADS = 8
B_KV_HEADS = 2
B_GROUP = B_HEADS // B_KV_HEADS
CMP_LEN = 32
CMP_STRIDE = 16
CMP_HIDDEN = 256
SLC_BLOCK = 64
SLC_TOPN = 16
WINDOW = 512

PEER_HEADS = 8
PEER_NKEYS = 128
PEER_N = PEER_NKEYS * PEER_NKEYS
PEER_QDIM = 256
PEER_HALF = PEER_QDIM // 2
PEER_TOPK = 16
PEER_SLOTS = PEER_HEADS * PEER_TOPK
PEER_HEAD_UNROLL = 4

A_WIDTH = A_HEADS * HEAD_DIM
B_WIDTH = B_HEADS * HEAD_DIM
B_KV_WIDTH = B_KV_HEADS * HEAD_DIM
N_BRANCH = 2

LANES = 128
SUBLANES = 8
VMEM_LIMIT_BYTES = 56 * 1024 * 1024

ROPE_COLS = 2 * A_WIDTH + B_WIDTH + 2 * B_KV_WIDTH
QK_AQ, QK_AK, QK_BQ, QK_BKS, QK_BKW = 0, 4, 8, 12, 13
PLAIN_COLS = A_WIDTH + 2 * B_KV_WIDTH
VV_AV, VV_BVS, VV_BVW = 0, 4, 5
MG_COLS = N_BRANCH * D_MODEL
B_PAIRS = B_WIDTH // LANES
B_HEAD_ORDER = tuple(h for p in range(B_PAIRS) for h in (p, B_GROUP + p))

TILE = 256
PEER_TB = 128
F32 = jnp.float32
BF16 = jnp.bfloat16
I32 = jnp.int32


def _const_spec(shape):
    n = len(shape)
    return pl.BlockSpec(shape, lambda *_: (0,) * n, pipeline_mode=pl.Buffered(1))


def _params(n_axes):
    return pltpu.CompilerParams(dimension_semantics=("arbitrary",) * n_axes,
                                vmem_limit_bytes=VMEM_LIMIT_BYTES)


def _dot(a, b):
    return jnp.dot(a, b, preferred_element_type=F32)


def _dot_nt(a, b):
    return lax.dot_general(a, b, (((1,), (1,)), ((), ())), preferred_element_type=F32)


def _split(a_f32):
    hi = a_f32.astype(BF16)
    lo = (a_f32 - hi.astype(F32)).astype(BF16)
    return hi, lo


def _split_dot(a_f32, b_bf16):
    hi, lo = _split(a_f32)
    return _dot(hi, b_bf16) + _dot(lo, b_bf16)


def _gelu_tanh(x):
    return 0.5 * x * (1.0 + jnp.tanh(math.sqrt(2.0 / math.pi) * (x + 0.044715 * (x * x * x))))


def _group_rmsnorm(y, bd, gain):
    ms = _split_dot(y * y, bd)
    return y * lax.rsqrt(ms + NORM_EPS) * gain


def _inproj_kernel(x_ref, g1_ref, w_rope_ref, w_plain_ref, w_small_ref, w_mg_ref, hg_ref,
                   cos_ref, sin_ref, bd_ref, bm_ref,
                   qk_ref, vvt_ref, kcin_ref, vcin_ref, gate_ref, mg_ref, kmean_ref):
    x = x_ref[...]
    h = x * lax.rsqrt(jnp.mean(x * x, axis=-1, keepdims=True) + NORM_EPS) * g1_ref[...]
    hb = h.astype(BF16)

    y_all = _dot(hb, w_rope_ref[...])
    vvt_ref[0] = _dot_nt(w_plain_ref[...], hb).astype(BF16)
    small = _dot(hb, w_small_ref[...])
    kcin_ref[...] = small[:, 0:LANES].astype(BF16)
    vcin_ref[...] = small[:, LANES:2 * LANES].astype(BF16)
    gate_ref[...] = jax.nn.sigmoid(small[:, 2 * LANES:3 * LANES])
    mg_ref[...] = jax.nn.sigmoid(_dot(hb, w_mg_ref[...]) + bm_ref[...]).astype(BF16)

    cos = cos_ref[...]
    sin = sin_ref[...]
    bd = bd_ref[...]
    lane = lax.broadcasted_iota(I32, (TILE, LANES), 1)
    first = (lane % HEAD_DIM) < ROT_HALF
    wide = 2 * LANES
    for t2 in range(ROPE_COLS // wide):
        y2 = y_all[:, t2 * wide:(t2 + 1) * wide]
        ms2 = _split_dot(y2 * y2, bd)
        for half in range(2):
            t = 2 * t2 + half
            sl = slice(t * LANES, (t + 1) * LANES)
            hs = slice(half * LANES, (half + 1) * LANES)
            yn = y2[:, hs] * lax.rsqrt(ms2[:, hs] + NORM_EPS) * hg_ref[:, sl]
            up = pltpu.roll(yn, LANES - ROT_HALF, axis=1)
            dn = pltpu.roll(yn, ROT_HALF, axis=1)
            out = yn * cos + jnp.where(first, -up, dn) * sin
            qk_ref[:, sl] = out.astype(BF16)
            if QK_AK <= t < QK_AK + A_WIDTH // LANES:
                c = t - QK_AK
                kmean_ref[0, :, c * LANES:(c + 1) * LANES] = jnp.mean(out, axis=0, keepdims=True)


def _inproj(x2d, g1, w_rope, w_plain, w_small, w_mg, hg, cos_t, sin_t, bd2, bm, seq):
    tokens = x2d.shape[0]
    nt = tokens // TILE
    tiles_per_seq = seq // TILE
    row = lambda i: (i, 0)
    pos = lambda i: (i % tiles_per_seq, 0)
    return pl.pallas_call(
        _inproj_kernel,
        grid=(nt,),
        in_specs=[
            pl.BlockSpec((TILE, D_MODEL), row),
            _const_spec((1, D_MODEL)),
            _const_spec((D_MODEL, ROPE_COLS)),
            _const_spec((PLAIN_COLS, D_MODEL)),
            _const_spec((D_MODEL, 3 * LANES)),
            _const_spec((D_MODEL, MG_COLS)),
            _const_spec((1, ROPE_COLS)),
            pl.BlockSpec((TILE, LANES), pos),
            pl.BlockSpec((TILE, LANES), pos),
            _const_spec((2 * LANES, 2 * LANES)),
            _const_spec((1, MG_COLS)),
        ],
        out_specs=[
            pl.BlockSpec((TILE, ROPE_COLS), row),
            pl.BlockSpec((1, PLAIN_COLS, TILE), lambda i: (i // tiles_per_seq, 0, i % tiles_per_seq)),
            pl.BlockSpec((TILE, LANES), row),
            pl.BlockSpec((TILE, LANES), row),
            pl.BlockSpec((TILE, LANES), row),
            pl.BlockSpec((TILE, MG_COLS), row),
            pl.BlockSpec((1, 1, A_WIDTH), lambda i: (i, 0, 0)),
        ],
        out_shape=[
            jax.ShapeDtypeStruct((tokens, ROPE_COLS), BF16),
            jax.ShapeDtypeStruct((tokens // seq, PLAIN_COLS, seq), BF16),
            jax.ShapeDtypeStruct((tokens, LANES), BF16),
            jax.ShapeDtypeStruct((tokens, LANES), BF16),
            jax.ShapeDtypeStruct((tokens, LANES), F32),
            jax.ShapeDtypeStruct((tokens, MG_COLS), BF16),
            jax.ShapeDtypeStruct((nt, 1, A_WIDTH), F32),
        ],
        compiler_params=_params(1),
        name="inproj",
    )(x2d, g1, w_rope, w_plain, w_small, w_mg, hg, cos_t, sin_t, bd2, bm)


LOG2E = math.log2(math.e)


def _scaled_halves(q_pair):
    lane = lax.broadcasted_iota(I32, (TILE, LANES), 1)
    q = q_pair.astype(F32)
    c = HEAD_DIM ** -0.5 * LOG2E
    plain = [jnp.where((lane // HEAD_DIM) == h, q, 0.0).astype(BF16) for h in range(2)]
    scaled = [jnp.where((lane // HEAD_DIM) == h, q * c, 0.0).astype(BF16) for h in range(2)]
    return plain, scaled


def _col_max(s_t):
    part = jnp.max(s_t.reshape(-1, SUBLANES, s_t.shape[-1]), axis=0)
    return jnp.max(part, axis=0, keepdims=True)


def _values_and_sum(v_t, p_t):
    ones = jnp.ones((2 * SUBLANES, v_t.shape[1]), BF16)
    res = _dot(jnp.concatenate([v_t, ones], axis=0), p_t.astype(BF16))
    return res[0:HEAD_DIM], res[HEAD_DIM:HEAD_DIM + SUBLANES]


def _flash_step(s_t, shift_bias, v_t, m, l8, acc_ref, rows):
    m_new = jnp.maximum(m, _col_max(s_t) + shift_bias)
    p_t = jnp.exp2(s_t - (m_new - shift_bias))
    alpha = jnp.exp2(m - m_new)
    pv, psum = _values_and_sum(v_t, p_t)
    l8 = alpha * l8 + psum
    acc_ref[rows, :] = alpha * acc_ref[rows, :] + pv
    return m_new, l8


def _flash_finish(l8s, acc_ref):
    inv = [1.0 / l8[0:1] for l8 in l8s]
    out_t = jnp.concatenate([acc_ref[0:HEAD_DIM, :] * inv[0], acc_ref[HEAD_DIM:, :] * inv[1]], axis=0)
    return jnp.transpose(out_t)


def _moba_kernel(q_ref, k_ref, vt_ref, kmh_ref, kml_ref, o_ref, acc_ref, *, topk, gate_rows):
    i = pl.program_id(1)
    pairs = A_WIDTH // LANES
    heads = [(p, h) for p in range(pairs) for h in range(2)]
    key_i = lax.broadcasted_iota(I32, (TILE, TILE), 0)
    qry_i = lax.broadcasted_iota(I32, (TILE, TILE), 1)
    causal = key_i <= qry_i
    blk = lax.broadcasted_iota(I32, (gate_rows, TILE), 0)
    blk_f = blk.astype(F32)
    own = pl.multiple_of(i * TILE, TILE)
    lanes_of = lambda p: slice(p * LANES, (p + 1) * LANES)
    rows_of = lambda p, h: slice((2 * p + h) * HEAD_DIM, (2 * p + h + 1) * HEAD_DIM)
    zero = jnp.zeros((1, TILE), F32)
    m0 = jnp.full((1, TILE), NEG_INF, F32)
    l0 = jnp.zeros((SUBLANES, TILE), F32)

    plain, scaled = [], []
    for p in range(pairs):
        plain_p, scaled_p = _scaled_halves(q_ref[0, :, lanes_of(p)])
        plain.extend(plain_p)
        scaled.extend(scaled_p)
    gates = [(_dot_nt(kmh_ref[0, :, lanes_of(p)], plain[c])
              + _dot_nt(kml_ref[0, :, lanes_of(p)], plain[c]))[0:gate_rows]
             for c, (p, h) in enumerate(heads)]
    own_scores = [_dot_nt(k_ref[0, pl.ds(own, TILE), lanes_of(p)], scaled[c])
                  for c, (p, h) in enumerate(heads)]

    biases, state = [], []
    for c, (p, h) in enumerate(heads):
        gate = jnp.where(blk < i, gates[c], NEG_INF)
        bias = jnp.full((gate_rows, TILE), NEG_INF, F32)
        for _ in range(topk):
            best = jnp.max(gate, axis=0, keepdims=True)
            idx = jnp.min(jnp.where(gate == best, blk_f, float(gate_rows)), axis=0, keepdims=True)
            pick = blk_f == idx
            bias = jnp.where(pick, jnp.where(best > 0.5 * NEG_INF, 0.0, bias), bias)
            gate = jnp.where(pick, NEG_INF, gate)
        biases.append(bias)
        acc_ref[rows_of(p, h), :] = jnp.zeros((HEAD_DIM, TILE), F32)
        state.extend(_flash_step(jnp.where(causal, own_scores[c], NEG_INF), zero,
                                 vt_ref[0, rows_of(p, h), pl.ds(own, TILE)], m0, l0,
                                 acc_ref, rows_of(p, h)))

    def past_block(n, carry):
        start = pl.multiple_of(n * TILE, TILE)
        out = []
        scores = [_dot_nt(k_ref[0, pl.ds(start, TILE), lanes_of(p)], scaled[c])
                  for c, (p, h) in enumerate(heads)]
        for c, (p, h) in enumerate(heads):
            b = jnp.max(jnp.where(blk == n, biases[c], NEG_INF), axis=0, keepdims=True)
            out.extend(_flash_step(scores[c], b, vt_ref[0, rows_of(p, h), pl.ds(start, TILE)],
                                   carry[2 * c], carry[2 * c + 1], acc_ref, rows_of(p, h)))
        return tuple(out)

    state = lax.fori_loop(0, i, past_block, tuple(state))
    for p in range(pairs):
        l8s = (state[4 * p + 1], state[4 * p + 3])
        o_ref[0, :, lanes_of(p)] = _flash_finish(l8s, acc_ref.at[lanes_of(p)]).astype(BF16)


def _moba(qk, vvt, kmean_hi, kmean_lo, batch, seq):
    n_blk = seq // A_BLOCK
    gate_rows = -(-n_blk // SUBLANES) * SUBLANES
    kernel = functools.partial(_moba_kernel, topk=min(A_TOPK, n_blk), gate_rows=gate_rows)
    return pl.pallas_call(
        kernel,
        grid=(batch, n_blk),
        in_specs=[
            pl.BlockSpec((1, TILE, A_WIDTH), lambda b, i: (b, i, QK_AQ * LANES // A_WIDTH)),
            pl.BlockSpec((1, seq, A_WIDTH), lambda b, i: (b, 0, QK_AK * LANES // A_WIDTH)),
            pl.BlockSpec((1, A_WIDTH, seq), lambda b, i: (b, VV_AV * LANES // A_WIDTH, 0)),
            pl.BlockSpec((1, LANES, A_WIDTH), lambda b, i: (b, 0, 0)),
            pl.BlockSpec((1, LANES, A_WIDTH), lambda b, i: (b, 0, 0)),
        ],
        out_specs=pl.BlockSpec((1, TILE, A_WIDTH), lambda b, i: (b, i, 0)),
        out_shape=jax.ShapeDtypeStruct((batch, seq, A_WIDTH), BF16),
        scratch_shapes=[pltpu.VMEM((A_WIDTH, TILE), F32)],
        compiler_params=_params(2),
        name="moba",
    )(qk, qk, vvt, kmean_hi, kmean_lo)


def _compress_kernel(kin_ref, vin_ref, w1_ref, pos_ref, b1_ref, w2_ref, b2_ref, bd_ref, g_ref,
                     kc_ref, vc_ref):
    n_rows = kin_ref.shape[1]
    for c, (in_ref, out_ref) in enumerate(((kin_ref, kc_ref), (vin_ref, vc_ref))):
        r = in_ref[0]
        first = _dot(r, w1_ref[c, 0])
        second = _dot(r, w1_ref[c, 1])
        const = (_dot(pos_ref[c, 0], w1_ref[c, 0]) + _dot(pos_ref[c, 1], w1_ref[c, 1]))[0:1]
        hid = first + pltpu.roll(second, n_rows - 1, axis=0) + const + b1_ref[c]
        out = _dot(_gelu_tanh(hid).astype(BF16), w2_ref[c]) + b2_ref[c]
        if c == 0:
            out = _group_rmsnorm(out, bd_ref[...], g_ref[...])
        out_ref[0] = out.astype(BF16)


def _compress(kin, vin, w1, pos, b1, w2, b2, bd, gain, batch):
    n_rows = kin.shape[1]
    blk = lambda b: (b, 0, 0)
    hid2 = B_KV_HEADS * CMP_HIDDEN
    return pl.pallas_call(
        _compress_kernel,
        grid=(batch,),
        in_specs=[
            pl.BlockSpec((1, n_rows, CMP_STRIDE * LANES), blk),
            pl.BlockSpec((1, n_rows, CMP_STRIDE * LANES), blk),
            _const_spec((2, 2, CMP_STRIDE * LANES, hid2)),
            _const_spec((2, 2, SUBLANES, CMP_STRIDE * LANES)),
            _const_spec((2, 1, hid2)),
            _const_spec((2, hid2, LANES)),
            _const_spec((2, 1, LANES)),
            _const_spec((LANES, LANES)),
            _const_spec((1, LANES)),
        ],
        out_specs=[pl.BlockSpec((1, n_rows, LANES), blk), pl.BlockSpec((1, n_rows, LANES), blk)],
        out_shape=[jax.ShapeDtypeStruct((batch, n_rows, LANES), BF16)] * 2,
        compiler_params=_params(1),
        name="nsa_compress",
    )(kin, vin, w1, pos, b1, w2, b2, bd, gain)


def _nsa_cmp_kernel(q_ref, kc_ref, vc_ref, ov_ref, ocmp_ref, selb_ref, *, n_cmp, topn):
    i = pl.program_id(1)
    scale = HEAD_DIM ** -0.5
    n_pad = kc_ref.shape[1]
    lane = lax.broadcasted_iota(I32, (TILE, LANES), 1)
    pos = i * TILE + lax.broadcasted_iota(I32, (TILE, 1), 0)
    ncol = lax.broadcasted_iota(I32, (TILE, n_pad), 1)
    visible = (ncol * CMP_STRIDE + (CMP_LEN - 1) <= pos) & (ncol < n_cmp)
    kc = kc_ref[0]
    vc = vc_ref[0]
    ov_t = ov_ref[...]
    heads = [(p, h) for p in range(B_PAIRS) for h in range(2)]
    qs = []
    for p in range(B_PAIRS):
        q_pair = q_ref[0, :, p * LANES:(p + 1) * LANES]
        qs.extend(jnp.where((lane // HEAD_DIM) == h, q_pair, jnp.zeros_like(q_pair)) for h in range(2))
    scores = [_dot_nt(q, kc) for q in qs]
    probs = []
    for s in scores:
        s = jnp.where(visible, s * scale, NEG_INF)
        e = jnp.where(visible, jnp.exp(s - jnp.max(s, axis=-1, keepdims=True)), 0.0)
        tot = jnp.sum(e, axis=-1, keepdims=True)
        probs.append(jnp.where(tot > 0.0, e / tot, 0.0))
    outs = [_dot(prob.astype(BF16), vc) for prob in probs]
    parts = []
    for prob in probs:
        hi, lo = _split(prob)
        parts.append(_dot_nt(ov_t, hi) + _dot_nt(ov_t, lo))
    for p in range(B_PAIRS):
        o_pair = jnp.where(lane < HEAD_DIM, outs[2 * p], outs[2 * p + 1])
        ocmp_ref[0, :, p * LANES:(p + 1) * LANES] = o_pair.astype(BF16)

    blk = lax.broadcasted_iota(I32, (HEAD_DIM, TILE), 0)
    own = (i * TILE + lax.broadcasted_iota(I32, (1, TILE), 1)) // SLC_BLOCK
    forced = (blk == 0) | (blk == own) | (blk == own - 1)
    for g in range(B_KV_HEADS):
        imp = functools.reduce(lambda a, b: a + b, [parts[c] for c, (p, h) in enumerate(heads) if h == g])
        imp = jnp.where(forced, BIG, imp)
        imp = jnp.where(blk <= own, imp, NEG_INF)
        rank = jnp.zeros((HEAD_DIM, TILE), F32)
        for k in range(HEAD_DIM):
            other = imp[k:k + 1]
            ahead = jnp.where(other > imp, 1.0, jnp.where(other == imp, jnp.where(k < blk, 1.0, 0.0), 0.0))
            rank = rank + ahead
        keep = jnp.where(rank < float(topn), jnp.where(imp > 0.5 * NEG_INF, 0.0, NEG_INF), NEG_INF)
        selb_ref[0, g * HEAD_DIM:(g + 1) * HEAD_DIM, :] = keep


def _nsa_cmp(qk, kc, vc, ov, batch, seq):
    n_pad = kc.shape[1]
    n_cmp = (seq - CMP_LEN) // CMP_STRIDE + 1
    topn = min(SLC_TOPN, seq // SLC_BLOCK)
    kernel = functools.partial(_nsa_cmp_kernel, n_cmp=n_cmp, topn=topn)
    return pl.pallas_call(
        kernel,
        grid=(batch, seq // TILE),
        in_specs=[
            pl.BlockSpec((1, TILE, B_WIDTH), lambda b, i: (b, i, QK_BQ * LANES // B_WIDTH)),
            pl.BlockSpec((1, n_pad, LANES), lambda b, i: (b, 0, 0)),
            pl.BlockSpec((1, n_pad, LANES), lambda b, i: (b, 0, 0)),
            _const_spec((HEAD_DIM, n_pad)),
        ],
        out_specs=[
            pl.BlockSpec((1, TILE, B_WIDTH), lambda b, i: (b, i, 0)),
            pl.BlockSpec((1, LANES, TILE), lambda b, i: (b, 0, i)),
        ],
        out_shape=[
            jax.ShapeDtypeStruct((batch, seq, B_WIDTH), BF16),
            jax.ShapeDtypeStruct((batch, LANES, seq), F32),
        ],
        compiler_params=_params(2),
        name="nsa_cmp_select",
    )(qk, kc, vc, ov)


def _nsa_attn_kernel(q_ref, ks_ref, vst_ref, kw_ref, vwt_ref, selb_ref, ocmp_ref, gate_ref, egt_ref,
                     o_ref, acc_s_ref, acc_w_ref):
    i = pl.program_id(1)
    blocks_per_tile = TILE // SLC_BLOCK
    heads = [(p, h) for p in range(B_PAIRS) for h in range(2)]
    key_i = lax.broadcasted_iota(I32, (TILE, TILE), 0)
    qry_i = lax.broadcasted_iota(I32, (TILE, TILE), 1)
    causal = key_i <= qry_i
    sub = lax.broadcasted_iota(I32, (SUBLANES, TILE), 0)
    own = pl.multiple_of(i * TILE, TILE)
    lanes_of = lambda p: slice(p * LANES, (p + 1) * LANES)
    rows_of = lambda p, h: slice((2 * p + h) * HEAD_DIM, (2 * p + h + 1) * HEAD_DIM)
    kv_rows = (slice(0, HEAD_DIM), slice(HEAD_DIM, LANES))
    zero = jnp.zeros((1, TILE), F32)
    m0 = jnp.full((1, TILE), NEG_INF, F32)
    l0 = jnp.zeros((SUBLANES, TILE), F32)
    scaled = []
    for p in range(B_PAIRS):
        scaled.extend(_scaled_halves(q_ref[0, :, lanes_of(p)])[1])

    def block_biases(h, n):
        first = h * HEAD_DIM + n * blocks_per_tile
        base = pl.multiple_of((first >> 3) << 3, SUBLANES)
        rows8 = selb_ref[0, pl.ds(base, SUBLANES), :]
        off = first - base
        return [jnp.max(jnp.where(sub == off + r, rows8, NEG_INF), axis=0, keepdims=True)
                for r in range(blocks_per_tile)]

    def slc_step(p, h, n, bias, s_t, m, l8):
        parts = [s_t[r * SLC_BLOCK:(r + 1) * SLC_BLOCK] for r in range(blocks_per_tile)]
        cand = [_col_max(parts[r]) + bias[r] for r in range(blocks_per_tile)]
        m_new = functools.reduce(jnp.maximum, cand, m)
        p_t = jnp.concatenate([jnp.exp2(parts[r] - (m_new - bias[r])) for r in range(blocks_per_tile)],
                              axis=0)
        alpha = jnp.exp2(m - m_new)
        start = pl.multiple_of(n * TILE, TILE)
        v_t = vst_ref[0, kv_rows[h], pl.ds(start, TILE)]
        pv, psum = _values_and_sum(v_t, p_t)
        acc_s_ref[rows_of(p, h), :] = alpha * acc_s_ref[rows_of(p, h), :] + pv
        return m_new, alpha * l8 + psum

    def scores_of(k_ref_, start):
        k = k_ref_[0, pl.ds(start, TILE), :]
        return [_dot_nt(k, scaled[c]) for c in range(len(heads))]

    state = []
    own_bias = [block_biases(h, i) for h in range(2)]
    own_scores = scores_of(ks_ref, own)
    for c, (p, h) in enumerate(heads):
        acc_s_ref[rows_of(p, h), :] = jnp.zeros((HEAD_DIM, TILE), F32)
        state.extend(slc_step(p, h, i, own_bias[h], jnp.where(causal, own_scores[c], NEG_INF), m0, l0))

    def past_tile(n, carry):
        scores = scores_of(ks_ref, pl.multiple_of(n * TILE, TILE))
        bias = [block_biases(h, n) for h in range(2)]
        out = []
        for c, (p, h) in enumerate(heads):
            out.extend(slc_step(p, h, n, bias[h], scores[c], carry[2 * c], carry[2 * c + 1]))
        return tuple(out)

    state = lax.fori_loop(0, i, past_tile, tuple(state))

    wstate = [(m0, l0)] * len(heads)
    for back, mask in ((0, causal), (1, None), (2, key_i > qry_i)):
        start = pl.multiple_of(jnp.maximum(i - back, 0) * TILE, TILE)
        scores = scores_of(kw_ref, start)
        exists = jnp.where(i >= back, 0.0, NEG_INF) + zero
        for c, (p, h) in enumerate(heads):
            if back == 0:
                acc_w_ref[rows_of(p, h), :] = jnp.zeros((HEAD_DIM, TILE), F32)
            s_t = scores[c] if mask is None else jnp.where(mask, scores[c], NEG_INF)
            wstate[c] = _flash_step(s_t, exists, vwt_ref[0, kv_rows[h], pl.ds(start, TILE)],
                                    wstate[c][0], wstate[c][1], acc_w_ref, rows_of(p, h))
    wstate = [l8 for (_, l8) in wstate]

    for p in range(B_PAIRS):
        o_slc = _flash_finish((state[4 * p + 1], state[4 * p + 3]), acc_s_ref.at[lanes_of(p)])
        o_win = _flash_finish((wstate[2 * p], wstate[2 * p + 1]), acc_w_ref.at[lanes_of(p)])
        g = _split_dot(gate_ref[0], egt_ref[p])
        out = (g[:, 0:LANES] * ocmp_ref[0, :, lanes_of(p)].astype(F32) + g[:, LANES:2 * LANES] * o_slc
               + g[:, 2 * LANES:3 * LANES] * o_win)
        o_ref[0, :, lanes_of(p)] = out.astype(BF16)


def _nsa_attn(qk, vvt, selb, ocmp, gate, eg, batch, seq):
    assert WINDOW == 2 * TILE
    k_spec = lambda t: pl.BlockSpec((1, seq, LANES), lambda b, i: (b, 0, t))
    vt_spec = lambda t: pl.BlockSpec((1, LANES, seq), lambda b, i: (b, t, 0))
    return pl.pallas_call(
        _nsa_attn_kernel,
        grid=(batch, seq // TILE),
        in_specs=[
            pl.BlockSpec((1, TILE, B_WIDTH), lambda b, i: (b, i, QK_BQ * LANES // B_WIDTH)),
            k_spec(QK_BKS),
            vt_spec(VV_BVS),
            k_spec(QK_BKW),
            vt_spec(VV_BVW),
            pl.BlockSpec((1, LANES, TILE), lambda b, i: (b, 0, i)),
            pl.BlockSpec((1, TILE, B_WIDTH), lambda b, i: (b, i, 0)),
            pl.BlockSpec((1, TILE, LANES), lambda b, i: (b, i, 0)),
            _const_spec((B_PAIRS, LANES, 3 * LANES)),
        ],
        out_specs=pl.BlockSpec((1, TILE, B_WIDTH), lambda b, i: (b, i, 0)),
        out_shape=jax.ShapeDtypeStruct((batch, seq, B_WIDTH), BF16),
        scratch_shapes=[pltpu.VMEM((B_WIDTH, TILE), F32), pltpu.VMEM((B_WIDTH, TILE), F32)],
        compiler_params=_params(2),
        name="nsa_select_window",
    )(qk, qk, vvt, qk, vvt, selb, ocmp, gate, eg)


def _top_rows(v, row_f, count, fill):
    vals, rows = [], []
    for _ in range(count):
        best = jnp.max(v, axis=0, keepdims=True)
        idx = jnp.min(jnp.where(v == best, row_f, fill), axis=0, keepdims=True)
        vals.append(best)
        rows.append(idx)
        v = jnp.where(row_f == idx, -jnp.inf, v)
    return vals, rows


def _store_token_tiles(ref, val):
    for c in range(D_MODEL // LANES):
        ref[pl.ds(c, TILE, stride=D_MODEL // LANES), :] = val[:, c * LANES:(c + 1) * LANES]


def _merge_peer_kernel(x_ref, ya_ref, yb_ref, mg_ref, wa_ref, wb_ref, wo_ref, g2_ref, wq_ref,
                       k1h_ref, k1l_ref, k2h_ref, k2l_ref,
                       x1_ref, hn_ref, idx_ref, gt_ref, q_scr, ex_scr, gt_scr):
    ua = _dot(ya_ref[...], wa_ref[...])
    ub = _dot(yb_ref[...], wb_ref[...])
    merged = (mg_ref[:, 0:D_MODEL].astype(F32) * ua + mg_ref[:, D_MODEL:2 * D_MODEL].astype(F32) * ub)
    x1 = x_ref[...] + _dot(merged.astype(BF16), wo_ref[...])
    _store_token_tiles(x1_ref, x1)
    hn = x1 * lax.rsqrt(jnp.mean(x1 * x1, axis=-1, keepdims=True) + NORM_EPS) * g2_ref[...]
    _store_token_tiles(hn_ref, hn)
    q_scr[...] = _dot(hn.astype(BF16), wq_ref[...])

    key_row = lax.broadcasted_iota(I32, (PEER_NKEYS, TILE), 0).astype(F32)
    n_cand = sum(-(-(PEER_TOPK // (a + 1)) // SUBLANES) * SUBLANES for a in range(SUBLANES)) + SUBLANES
    cand_row = lax.broadcasted_iota(I32, (n_cand, TILE), 0).astype(F32)

    def head_scores(h):
        base = pl.multiple_of(h * PEER_QDIM, PEER_QDIM)
        q1h, q1l = _split(q_scr[:, pl.ds(base, PEER_HALF)])
        q2h, q2l = _split(q_scr[:, pl.ds(base + PEER_HALF, PEER_HALF)])
        k1h, k2h = k1h_ref[...], k2h_ref[...]
        s1 = _dot_nt(k1h, q1h) + _dot_nt(k1h, q1l) + _dot_nt(k1l_ref[...], q1h)
        s2 = _dot_nt(k2h, q2h) + _dot_nt(k2h, q2l) + _dot_nt(k2l_ref[...], q2h)
        return s1, s2

    def head_select(h, s1, s2):
        v1, i1 = _top_rows(s1, key_row, PEER_TOPK, float(PEER_NKEYS))
        v2, i2 = _top_rows(s2, key_row, PEER_TOPK, float(PEER_NKEYS))
        v1m, i1m = jnp.concatenate(v1, axis=0), jnp.concatenate(i1, axis=0)
        v2m, i2m = jnp.concatenate(v2, axis=0), jnp.concatenate(i2, axis=0)
        cand, cidx = [], []
        for a in range(SUBLANES):
            n_b = PEER_TOPK // (a + 1)
            rows = -(-n_b // SUBLANES) * SUBLANES
            keep = lax.broadcasted_iota(I32, (rows, TILE), 0) < n_b
            cand.append(jnp.where(keep, v1[a] + v2m[0:rows], -jnp.inf))
            cidx.append(i1[a] * float(PEER_NKEYS) + i2m[0:rows])
        cand.append(v1m[SUBLANES:] + v2[0])
        cidx.append(i1m[SUBLANES:] * float(PEER_NKEYS) + i2[0])
        cand = jnp.concatenate(cand, axis=0)
        cidx = jnp.concatenate(cidx, axis=0)
        vals, experts = [], []
        v = cand
        for _ in range(PEER_TOPK):
            best = jnp.max(v, axis=0, keepdims=True)
            at = jnp.min(jnp.where(v == best, cand_row, float(n_cand)), axis=0, keepdims=True)
            hit = cand_row == at
            experts.append(jnp.max(jnp.where(hit, cidx, -1.0), axis=0, keepdims=True))
            vals.append(best)
            v = jnp.where(hit, -jnp.inf, v)
        vals = jnp.concatenate(vals, axis=0)
        e = jnp.exp(vals - vals[0:1])
        out_row = pl.multiple_of(h * PEER_TOPK, PEER_TOPK)
        gt_scr[pl.ds(out_row, PEER_TOPK), :] = e / jnp.sum(e, axis=0, keepdims=True)
        ex_scr[pl.ds(out_row, PEER_TOPK), :] = jnp.concatenate(experts, axis=0)

    def head_group(i, carry):
        hs = [i * PEER_HEAD_UNROLL + k for k in range(PEER_HEAD_UNROLL)]
        scores = [head_scores(h) for h in hs]
        for h, (s1, s2) in zip(hs, scores):
            head_select(h, s1, s2)
        return carry

    lax.fori_loop(0, PEER_HEADS // PEER_HEAD_UNROLL, head_group, 0)
    gt_ref[...] = jnp.transpose(gt_scr[...])
    idx_ref[...] = (jnp.transpose(ex_scr[...]) * float(HALF_ROWS)).astype(I32)


def _merge_peer(x2d, ya, yb, mg, wa, wb, wo, g2, wq, k1h, k1l, k2h, k2l):
    tokens = x2d.shape[0]
    row = lambda i: (i, 0)
    return pl.pallas_call(
        _merge_peer_kernel,
        grid=(tokens // TILE,),
        in_specs=[
            pl.BlockSpec((TILE, D_MODEL), row),
            pl.BlockSpec((TILE, A_WIDTH), row),
            pl.BlockSpec((TILE, B_WIDTH), row),
            pl.BlockSpec((TILE, MG_COLS), row),
            _const_spec((A_WIDTH, D_MODEL)),
            _const_spec((B_WIDTH, D_MODEL)),
            _const_spec((D_MODEL, D_MODEL)),
            _const_spec((1, D_MODEL)),
            _const_spec((D_MODEL, PEER_HEADS * PEER_QDIM)),
            _const_spec((PEER_NKEYS, PEER_HALF)),
            _const_spec((PEER_NKEYS, PEER_HALF)),
            _const_spec((PEER_NKEYS, PEER_HALF)),
            _const_spec((PEER_NKEYS, PEER_HALF)),
        ],
        out_specs=[
            pl.BlockSpec((TILE * SUBLANES, LANES), row),
            pl.BlockSpec((TILE * SUBLANES, LANES), row),
            pl.BlockSpec((TILE, PEER_SLOTS), row),
            pl.BlockSpec((TILE, PEER_SLOTS), row),
        ],
        out_shape=[
            jax.ShapeDtypeStruct((tokens * SUBLANES, LANES), F32),
            jax.ShapeDtypeStruct((tokens * SUBLANES, LANES), F32),
            jax.ShapeDtypeStruct((tokens, PEER_SLOTS), I32),
            jax.ShapeDtypeStruct((tokens, PEER_SLOTS), F32),
        ],
        scratch_shapes=[pltpu.VMEM((TILE, PEER_HEADS * PEER_QDIM), F32),
                        pltpu.VMEM((PEER_SLOTS, TILE), F32),
                        pltpu.VMEM((PEER_SLOTS, TILE), F32)],
        compiler_params=_params(1),
        name="merge_peer_topk",
    )(x2d, ya, yb, mg, wa, wb, wo, g2, wq, k1h, k1l, k2h, k2l)


HALF_ROWS = SUBLANES // 2
HI_MASK = -65536
ROW_MASK = 65535
PEER_UNROLL = 16
PEER_OUT_UNROLL = 2


def _unpack(words):
    hi = pltpu.bitcast(words & HI_MASK, F32)
    lo = pltpu.bitcast(words << 16, F32)
    return hi, lo


def _expert_row(tab_ref, row):
    return tab_ref[pl.ds(pl.multiple_of(row, HALF_ROWS), HALF_ROWS), :]


def _slot_reader(head_refs, t):
    base = t * PEER_TOPK
    index = {}

    def read(j):
        k = j % PEER_TOPK
        if k not in index:
            index[k] = base + k
        return head_refs[j // PEER_TOPK][index[k]]

    return read


def _split_heads(words):
    return [words[:, h * PEER_TOPK:(h + 1) * PEER_TOPK].reshape(-1) for h in range(PEER_HEADS)]


def _fold_rows(v, shift):
    return v + pltpu.roll(v, shift, axis=0)


def _peer_u_kernel(*refs):
    row_refs = refs[:PEER_HEADS]
    rowv_ref, hn_ref, gt_ref, tab_ref, route_ref = refs[PEER_HEADS:]
    lane = lax.broadcasted_iota(I32, (SUBLANES, LANES), 1)
    sub = lax.broadcasted_iota(I32, (SUBLANES, LANES), 0)
    keep_hi_pair = (sub % 4) >= 2
    odd = (sub % 2) == 1
    groups = PEER_SLOTS // SUBLANES

    def one_token(t, accs):
        x = hn_ref[t]
        xa = jnp.concatenate([x[0:HALF_ROWS], x[0:HALF_ROWS]], axis=0)
        xb = jnp.concatenate([x[HALF_ROWS:], x[HALF_ROWS:]], axis=0)

        slot_row = _slot_reader(row_refs, t)

        def pair(ja, jb):
            words = jnp.concatenate([_expert_row(tab_ref, slot_row(ja)),
                                     _expert_row(tab_ref, slot_row(jb))], axis=0)
            hi, lo = _unpack(words)
            return hi * xa + lo * xb

        new = []
        for g in range(groups):
            j = g * SUBLANES
            quads = []
            for (a, b, c, d) in ((j + 3, j + 7, j + 1, j + 5), (j + 2, j + 6, j, j + 4)):
                w_ab = _fold_rows(pair(a, b), 2)
                w_cd = _fold_rows(pair(c, d), 2)
                z = jnp.where(keep_hi_pair, w_ab, pltpu.roll(w_cd, 6, axis=0))
                quads.append(_fold_rows(z, 1))
            folded = jnp.where(odd, quads[0], pltpu.roll(quads[1], 7, axis=0))
            total = jnp.sum(folded, axis=1, keepdims=True)
            new.append(jnp.where(lane == t, total, accs[g]))
        return tuple(new)

    def step(i, accs):
        for k in range(PEER_UNROLL):
            accs = one_token(i * PEER_UNROLL + k, accs)
        return accs

    init = tuple(jnp.zeros((SUBLANES, LANES), F32) for _ in range(groups))
    accs = lax.fori_loop(0, PEER_TB // PEER_UNROLL, step, init)
    a = jnp.transpose(jnp.concatenate(accs, axis=0))
    w = (_gelu_tanh(a) * gt_ref[...]).astype(BF16).astype(F32)
    route_ref[...] = (pltpu.bitcast(w, I32) & HI_MASK) | rowv_ref[...]


def _head_smem_spec():
    return pl.BlockSpec((PEER_TB * PEER_TOPK,), lambda i: (i,), memory_space=pltpu.SMEM)


def _peer_u(rows, hn3, gt, tab):
    tokens = hn3.shape[0]
    blk = lambda i: (i, 0)
    return pl.pallas_call(
        _peer_u_kernel,
        grid=(tokens // PEER_TB,),
        in_specs=[_head_smem_spec()] * PEER_HEADS + [
            pl.BlockSpec((PEER_TB, PEER_SLOTS), blk),
            pl.BlockSpec((PEER_TB, SUBLANES, LANES), lambda i: (i, 0, 0)),
            pl.BlockSpec((PEER_TB, PEER_SLOTS), blk),
            _const_spec((PEER_N * HALF_ROWS, LANES)),
        ],
        out_specs=pl.BlockSpec((PEER_TB, PEER_SLOTS), blk),
        out_shape=jax.ShapeDtypeStruct((tokens, PEER_SLOTS), I32),
        compiler_params=_params(1),
        name="peer_expert_in",
    )(*_split_heads(rows), rows, hn3, gt, tab)


def _peer_v_kernel(*refs):
    route_refs = refs[:PEER_HEADS]
    x1_ref, tab_ref, o_ref = refs[PEER_HEADS:]

    upper = lax.broadcasted_iota(I32, (SUBLANES, LANES), 0) >= HALF_ROWS

    def one_token(t):
        acc_hi = jnp.zeros((SUBLANES, LANES), F32)
        acc_lo = jnp.zeros((SUBLANES, LANES), F32)
        slot_word = _slot_reader(route_refs, t)
        for j in range(0, PEER_SLOTS, 2):
            wa, wb = slot_word(j), slot_word(j + 1)
            words = jnp.concatenate([_expert_row(tab_ref, wa & ROW_MASK),
                                     _expert_row(tab_ref, wb & ROW_MASK)], axis=0)
            hi, lo = _unpack(words)
            both = jnp.where(upper, jnp.full((SUBLANES, LANES), wb, I32),
                             jnp.full((SUBLANES, LANES), wa, I32))
            w = pltpu.bitcast(both & HI_MASK, F32)
            acc_hi = acc_hi + w * hi
            acc_lo = acc_lo + w * lo
        out = jnp.concatenate([acc_hi[0:HALF_ROWS] + acc_hi[HALF_ROWS:],
                               acc_lo[0:HALF_ROWS] + acc_lo[HALF_ROWS:]], axis=0)
        o_ref[t] = x1_ref[t] + out

    def step(i, carry):
        for k in range(PEER_OUT_UNROLL):
            one_token(i * PEER_OUT_UNROLL + k)
        return carry

    lax.fori_loop(0, PEER_TB // PEER_OUT_UNROLL, step, 0)


def _peer_v(route, x13, tab):
    tokens = x13.shape[0]
    return pl.pallas_call(
        _peer_v_kernel,
        grid=(tokens // PEER_TB,),
        in_specs=[_head_smem_spec()] * PEER_HEADS + [
            pl.BlockSpec((PEER_TB, SUBLANES, LANES), lambda i: (i, 0, 0)),
            _const_spec((PEER_N * HALF_ROWS, LANES)),
        ],
        out_specs=pl.BlockSpec((PEER_TB, SUBLANES, LANES), lambda i: (i, 0, 0)),
        out_shape=jax.ShapeDtypeStruct((tokens, SUBLANES, LANES), F32),
        compiler_params=_params(1),
        name="peer_expert_out",
    )(*_split_heads(route), x13, tab)


def _pack_table(t):
    bits = lax.bitcast_convert_type(t.astype(BF16), jnp.uint16).astype(jnp.uint32)
    half = D_MODEL // 2
    words = (bits[:, :half] << 16) | bits[:, half:]
    return lax.bitcast_convert_type(words, I32).reshape(t.shape[0] * HALF_ROWS, LANES)


def _rope_tables(seq):
    inv = ROPE_THETA ** (-jnp.arange(0, ROT_DIM, 2, dtype=F32) / ROT_DIM)
    ang = jnp.arange(seq, dtype=F32)[:, None] * inv[None, :]
    d = np.arange(LANES) % HEAD_DIM
    cos = jnp.where(d[None, :] < ROT_DIM, jnp.cos(ang)[:, d % ROT_HALF], 1.0)
    sin = jnp.where(d[None, :] < ROT_DIM, jnp.sin(ang)[:, d % ROT_HALF], 0.0)
    return cos.astype(F32), sin.astype(F32)


def _compress_weights(w1, pos):
    out_w, out_p = [], []
    for part in range(2):
        wpart = w1[part * CMP_STRIDE * HEAD_DIM:(part + 1) * CMP_STRIDE * HEAD_DIM]
        wpart = wpart.reshape(CMP_STRIDE, HEAD_DIM, CMP_HIDDEN)
        full = jnp.einsum("ldj,gh->lgdhj", wpart, jnp.eye(B_KV_HEADS, dtype=F32))
        out_w.append(full.reshape(CMP_STRIDE * LANES, B_KV_HEADS * CMP_HIDDEN))
        ppart = pos[part * CMP_STRIDE:(part + 1) * CMP_STRIDE]
        prow = jnp.tile(ppart[:, None, :], (1, B_KV_HEADS, 1)).reshape(1, CMP_STRIDE * LANES)
        out_p.append(jnp.tile(prow, (SUBLANES, 1)))
    return jnp.stack(out_w).astype(BF16), jnp.stack(out_p).astype(BF16)


def kernel(x, norm1_g, w_in, b_merge, a_q_g, a_k_g, b_q_g, b_kc_g, b_ks_g, b_kw_g, cmp_pos_k,
           cmp_k_w1, cmp_k_b1, cmp_k_w2, cmp_k_b2, cmp_pos_v, cmp_v_w1, cmp_v_b1, cmp_v_w2,
           cmp_v_b2, w_up_a, w_up_b, w_out, norm2_g, peer_wq, peer_k1, peer_k2, peer_u, peer_v):
    batch, seq, _ = x.shape
    assert seq % TILE == 0 and seq // SLC_BLOCK <= HEAD_DIM
    tokens = batch * seq
    l = 0
    x2d = x.reshape(tokens, D_MODEL)

    w = w_in[l]
    sizes = (A_WIDTH, A_WIDTH, A_WIDTH, B_WIDTH) + (B_KV_WIDTH,) * 6 + (3 * B_HEADS, MG_COLS)
    offs = np.concatenate([[0], np.cumsum(sizes)])
    seg = lambda k: w[:, offs[k]:offs[k + 1]]
    aq, ak, av, bq, bkc, bvc, bks, bvs, bkw, bvw, bgate, mgate = (seg(k) for k in range(12))
    bq_perm = bq.reshape(D_MODEL, B_HEADS, HEAD_DIM)[:, np.array(B_HEAD_ORDER)].reshape(D_MODEL, B_WIDTH)
    w_rope = jnp.concatenate([aq, ak, bq_perm, bks, bkw], axis=1).astype(BF16)
    w_plain = jnp.concatenate([av, bvs, bvw], axis=1).T.astype(BF16)
    w_small = jnp.concatenate([bkc, bvc, jnp.pad(bgate, ((0, 0), (0, LANES - 3 * B_HEADS)))],
                              axis=1).astype(BF16)
    hg = jnp.concatenate([jnp.tile(a_q_g[l], A_HEADS), jnp.tile(a_k_g[l], A_HEADS),
                          jnp.tile(b_q_g[l], B_HEADS), jnp.tile(b_ks_g[l], B_KV_HEADS),
                          jnp.tile(b_kw_g[l], B_KV_HEADS)])[None, :]
    cos_t, sin_t = _rope_tables(seq)
    bd = jnp.asarray(np.kron(np.eye(LANES // HEAD_DIM), np.full((HEAD_DIM, HEAD_DIM), 1.0 / HEAD_DIM)),
                     BF16)

    qk, vvt, kcin, vcin, gate, mg, kmean = _inproj(
        x2d, norm1_g[l][None, :], w_rope, w_plain, w_small, mgate.astype(BF16), hg, cos_t, sin_t,
        jnp.kron(jnp.eye(2, dtype=BF16), bd), b_merge[l].reshape(1, MG_COLS), seq)
    qk = qk.reshape(batch, seq, ROPE_COLS)

    n_blk = seq // A_BLOCK
    kmean = jnp.pad(kmean.reshape(batch, n_blk, A_WIDTH), ((0, 0), (0, LANES - n_blk), (0, 0)))
    km_hi = kmean.astype(BF16)
    km_lo = (kmean - km_hi.astype(F32)).astype(BF16)
    ya = _moba(qk, vvt, km_hi, km_lo, batch, seq)

    n_rows = seq // CMP_STRIDE
    wk1, pk = _compress_weights(cmp_k_w1[l], cmp_pos_k[l])
    wv1, pv = _compress_weights(cmp_v_w1[l], cmp_pos_v[l])
    blockdiag = lambda m: jnp.kron(jnp.eye(B_KV_HEADS, dtype=F32), m)
    w2 = jnp.stack([blockdiag(cmp_k_w2[l]), blockdiag(cmp_v_w2[l])]).astype(BF16)
    b1 = jnp.stack([jnp.tile(cmp_k_b1[l], B_KV_HEADS), jnp.tile(cmp_v_b1[l], B_KV_HEADS)])[:, None, :]
    b2 = jnp.stack([jnp.tile(cmp_k_b2[l], B_KV_HEADS), jnp.tile(cmp_v_b2[l], B_KV_HEADS)])[:, None, :]
    kc, vc = _compress(kcin.reshape(batch, n_rows, CMP_STRIDE * LANES),
                       vcin.reshape(batch, n_rows, CMP_STRIDE * LANES),
                       jnp.stack([wk1, wv1]), jnp.stack([pk, pv]), b1, w2, b2, bd,
                       jnp.tile(b_kc_g[l], B_KV_HEADS)[None, :], batch)

    n_cmp = (seq - CMP_LEN) // CMP_STRIDE + 1
    n_slc = seq // SLC_BLOCK
    ci = np.arange(n_rows)[:, None]
    sj = np.arange(HEAD_DIM)[None, :]
    ov = ((ci * CMP_STRIDE < (sj + 1) * SLC_BLOCK) & (ci * CMP_STRIDE + CMP_LEN > sj * SLC_BLOCK)
          & (ci < n_cmp) & (sj < n_slc)).astype(np.float32)
    ov = jnp.asarray(ov.T, BF16)
    ocmp, selb = _nsa_cmp(qk, kc, vc, ov, batch, seq)

    eg = np.zeros((B_PAIRS, LANES, 3 * LANES), np.float32)
    for p in range(B_PAIRS):
        for half in range(2):
            head = B_HEAD_ORDER[2 * p + half]
            for c in range(3):
                eg[p, head * 3 + c, c * LANES + half * HEAD_DIM:c * LANES + (half + 1) * HEAD_DIM] = 1.0
    yb = _nsa_attn(qk, vvt, selb, ocmp, gate.reshape(batch, seq, LANES), jnp.asarray(eg, BF16),
                   batch, seq)

    wb_perm = w_up_b[l].reshape(B_HEADS, HEAD_DIM, D_MODEL)[np.array(B_HEAD_ORDER)].reshape(
        B_WIDTH, D_MODEL)
    k1h, k1l = _split(peer_k1[l])
    k2h, k2l = _split(peer_k2[l])
    x1, hn, rows, gt = _merge_peer(
        x2d, ya.reshape(tokens, A_WIDTH), yb.reshape(tokens, B_WIDTH), mg,
        w_up_a[l].astype(BF16), wb_perm.astype(BF16), w_out[l].astype(BF16),
        norm2_g[l][None, :], peer_wq[l].astype(BF16), k1h, k1l, k2h, k2l)

    route = _peer_u(rows, hn.reshape(tokens, SUBLANES, LANES), gt, _pack_table(peer_u[l]))
    out = _peer_v(route, x1.reshape(tokens, SUBLANES, LANES), _pack_table(peer_v[l]))
    return out.reshape(batch, seq, D_MODEL)
```

```python
import functools
import math

import jax
import jax.numpy as jnp
import numpy as np
from jax import lax
from jax.experimental import pallas as pl
from jax.experimental.pallas import tpu as pltpu

D_MODEL = 1024
HEAD_DIM = 64
ROT_DIM = HEAD_DIM // 4
ROT_HALF = ROT_DIM // 2
ROPE_THETA = 500000.0
NORM_EPS = 1e-6
NEG_INF = -1e30
BIG = 1e30

A_HEADS = 8
A_BLOCK = 256
A_TOPK = 3

B_HEADS = 8
B_KV_HEADS = 2
B_GROUP = B_HEADS // B_KV_HEADS
CMP_LEN = 32
CMP_STRIDE = 16
CMP_HIDDEN = 256
SLC_BLOCK = 64
SLC_TOPN = 16
WINDOW = 512

PEER_HEADS = 8
PEER_NKEYS = 128
PEER_N = PEER_NKEYS * PEER_NKEYS
PEER_QDIM = 256
PEER_HALF = PEER_QDIM // 2
PEER_TOPK = 16
PEER_SLOTS = PEER_HEADS * PEER_TOPK
PEER_HEAD_UNROLL = 4

A_WIDTH = A_HEADS * HEAD_DIM
B_WIDTH = B_HEADS * HEAD_DIM
B_KV_WIDTH = B_KV_HEADS * HEAD_DIM
N_BRANCH = 2

LANES = 128
SUBLANES = 8
VMEM_LIMIT_BYTES = 56 * 1024 * 1024

ROPE_COLS = 2 * A_WIDTH + B_WIDTH + 2 * B_KV_WIDTH
QK_AQ, QK_AK, QK_BQ, QK_BKS, QK_BKW = 0, 4, 8, 12, 13
PLAIN_COLS = A_WIDTH + 2 * B_KV_WIDTH
VV_AV, VV_BVS, VV_BVW = 0, 4, 5
MG_COLS = N_BRANCH * D_MODEL
B_PAIRS = B_WIDTH // LANES
B_HEAD_ORDER = tuple(h for p in range(B_PAIRS) for h in (p, B_GROUP + p))

TILE = 256
PEER_TB = 128
F32 = jnp.float32
BF16 = jnp.bfloat16
I32 = jnp.int32


def _const_spec(shape):
    n = len(shape)
    return pl.BlockSpec(shape, lambda *_: (0,) * n, pipeline_mode=pl.Buffered(1))


def _params(n_axes):
    return pltpu.CompilerParams(dimension_semantics=("arbitrary",) * n_axes,
                                vmem_limit_bytes=VMEM_LIMIT_BYTES)


def _dot(a, b):
    return jnp.dot(a, b, preferred_element_type=F32)


def _dot_nt(a, b):
    return lax.dot_general(a, b, (((1,), (1,)), ((), ())), preferred_element_type=F32)


def _split(a_f32):
    hi = a_f32.astype(BF16)
    lo = (a_f32 - hi.astype(F32)).astype(BF16)
    return hi, lo


def _split_dot(a_f32, b_bf16):
    hi, lo = _split(a_f32)
    return _dot(hi, b_bf16) + _dot(lo, b_bf16)


def _gelu_tanh(x):
    return 0.5 * x * (1.0 + jnp.tanh(math.sqrt(2.0 / math.pi) * (x + 0.044715 * (x * x * x))))


def _group_rmsnorm(y, bd, gain):
    ms = _split_dot(y * y, bd)
    return y * lax.rsqrt(ms + NORM_EPS) * gain


def _inproj_kernel(x_ref, g1_ref, w_rope_ref, w_plain_ref, w_small_ref, w_mg_ref, hg_ref,
                   cos_ref, sin_ref, bd_ref, bm_ref,
                   qk_ref, vvt_ref, kcin_ref, vcin_ref, gate_ref, mg_ref, kmean_ref):
    x = x_ref[...]
    h = x * lax.rsqrt(jnp.mean(x * x, axis=-1, keepdims=True) + NORM_EPS) * g1_ref[...]
    hb = h.astype(BF16)

    y_all = _dot(hb, w_rope_ref[...])
    vvt_ref[0] = _dot_nt(w_plain_ref[...], hb).astype(BF16)
    small = _dot(hb, w_small_ref[...])
    kcin_ref[...] = small[:, 0:LANES].astype(BF16)
    vcin_ref[...] = small[:, LANES:2 * LANES].astype(BF16)
    gate_ref[...] = jax.nn.sigmoid(small[:, 2 * LANES:3 * LANES])
    mg_ref[...] = jax.nn.sigmoid(_dot(hb, w_mg_ref[...]) + bm_ref[...]).astype(BF16)

    cos = cos_ref[...]
    sin = sin_ref[...]
    bd = bd_ref[...]
    lane = lax.broadcasted_iota(I32, (TILE, LANES), 1)
    first = (lane % HEAD_DIM) < ROT_HALF
    wide = 2 * LANES
    for t2 in range(ROPE_COLS // wide):
        y2 = y_all[:, t2 * wide:(t2 + 1) * wide]
        ms2 = _split_dot(y2 * y2, bd)
        for half in range(2):
            t = 2 * t2 + half
            sl = slice(t * LANES, (t + 1) * LANES)
            hs = slice(half * LANES, (half + 1) * LANES)
            yn = y2[:, hs] * lax.rsqrt(ms2[:, hs] + NORM_EPS) * hg_ref[:, sl]
            up = pltpu.roll(yn, LANES - ROT_HALF, axis=1)
            dn = pltpu.roll(yn, ROT_HALF, axis=1)
            out = yn * cos + jnp.where(first, -up, dn) * sin
            qk_ref[:, sl] = out.astype(BF16)
            if QK_AK <= t < QK_AK + A_WIDTH // LANES:
                c = t - QK_AK
                kmean_ref[0, :, c * LANES:(c + 1) * LANES] = jnp.mean(out, axis=0, keepdims=True)


def _inproj(x2d, g1, w_rope, w_plain, w_small, w_mg, hg, cos_t, sin_t, bd2, bm, seq):
    tokens = x2d.shape[0]
    nt = tokens // TILE
    tiles_per_seq = seq // TILE
    row = lambda i: (i, 0)
    pos = lambda i: (i % tiles_per_seq, 0)
    return pl.pallas_call(
        _inproj_kernel,
        grid=(nt,),
        in_specs=[
            pl.BlockSpec((TILE, D_MODEL), row),
            _const_spec((1, D_MODEL)),
            _const_spec((D_MODEL, ROPE_COLS)),
            _const_spec((PLAIN_COLS, D_MODEL)),
            _const_spec((D_MODEL, 3 * LANES)),
            _const_spec((D_MODEL, MG_COLS)),
            _const_spec((1, ROPE_COLS)),
            pl.BlockSpec((TILE, LANES), pos),
            pl.BlockSpec((TILE, LANES), pos),
            _const_spec((2 * LANES, 2 * LANES)),
            _const_spec((1, MG_COLS)),
        ],
        out_specs=[
            pl.BlockSpec((TILE, ROPE_COLS), row),
            pl.BlockSpec((1, PLAIN_COLS, TILE), lambda i: (i // tiles_per_seq, 0, i % tiles_per_seq)),
            pl.BlockSpec((TILE, LANES), row),
            pl.BlockSpec((TILE, LANES), row),
            pl.BlockSpec((TILE, LANES), row),
            pl.BlockSpec((TILE, MG_COLS), row),
            pl.BlockSpec((1, 1, A_WIDTH), lambda i: (i, 0, 0)),
        ],
        out_shape=[
            jax.ShapeDtypeStruct((tokens, ROPE_COLS), BF16),
            jax.ShapeDtypeStruct((tokens // seq, PLAIN_COLS, seq), BF16),
            jax.ShapeDtypeStruct((tokens, LANES), BF16),
            jax.ShapeDtypeStruct((tokens, LANES), BF16),
            jax.ShapeDtypeStruct((tokens, LANES), F32),
            jax.ShapeDtypeStruct((tokens, MG_COLS), BF16),
            jax.ShapeDtypeStruct((nt, 1, A_WIDTH), F32),
        ],
        compiler_params=_params(1),
        name="inproj",
    )(x2d, g1, w_rope, w_plain, w_small, w_mg, hg, cos_t, sin_t, bd2, bm)


LOG2E = math.log2(math.e)


def _scaled_halves(q_pair):
    lane = lax.broadcasted_iota(I32, (TILE, LANES), 1)
    q = q_pair.astype(F32)
    c = HEAD_DIM ** -0.5 * LOG2E
    plain = [jnp.where((lane // HEAD_DIM) == h, q, 0.0).astype(BF16) for h in range(2)]
    scaled = [jnp.where((lane // HEAD_DIM) == h, q * c, 0.0).astype(BF16) for h in range(2)]
    return plain, scaled


def _col_max(s_t):
    part = jnp.max(s_t.reshape(-1, SUBLANES, s_t.shape[-1]), axis=0)
    return jnp.max(part, axis=0, keepdims=True)


def _values_and_sum(v_t, p_t):
    ones = jnp.ones((2 * SUBLANES, v_t.shape[1]), BF16)
    res = _dot(jnp.concatenate([v_t, ones], axis=0), p_t.astype(BF16))
    return res[0:HEAD_DIM], res[HEAD_DIM:HEAD_DIM + SUBLANES]


def _flash_step(s_t, shift_bias, v_t, m, l8, acc_ref, rows):
    m_new = jnp.maximum(m, _col_max(s_t) + shift_bias)
    p_t = jnp.exp2(s_t - (m_new - shift_bias))
    alpha = jnp.exp2(m - m_new)
    pv, psum = _values_and_sum(v_t, p_t)
    l8 = alpha * l8 + psum
    acc_ref[rows, :] = alpha * acc_ref[rows, :] + pv
    return m_new, l8


def _flash_finish(l8s, acc_ref):
    inv = [1.0 / l8[0:1] for l8 in l8s]
    out_t = jnp.concatenate([acc_ref[0:HEAD_DIM, :] * inv[0], acc_ref[HEAD_DIM:, :] * inv[1]], axis=0)
    return jnp.transpose(out_t)


def _moba_kernel(q_ref, k_ref, vt_ref, kmh_ref, kml_ref, o_ref, acc_ref, *, topk, gate_rows):
    i = pl.program_id(1)
    pairs = A_WIDTH // LANES
    heads = [(p, h) for p in range(pairs) for h in range(2)]
    key_i = lax.broadcasted_iota(I32, (TILE, TILE), 0)
    qry_i = lax.broadcasted_iota(I32, (TILE, TILE), 1)
    causal = key_i <= qry_i
    blk = lax.broadcasted_iota(I32, (gate_rows, TILE), 0)
    blk_f = blk.astype(F32)
    own = pl.multiple_of(i * TILE, TILE)
    lanes_of = lambda p: slice(p * LANES, (p + 1) * LANES)
    rows_of = lambda p, h: slice((2 * p + h) * HEAD_DIM, (2 * p + h + 1) * HEAD_DIM)
    zero = jnp.zeros((1, TILE), F32)
    m0 = jnp.full((1, TILE), NEG_INF, F32)
    l0 = jnp.zeros((SUBLANES, TILE), F32)

    plain, scaled = [], []
    for p in range(pairs):
        plain_p, scaled_p = _scaled_halves(q_ref[0, :, lanes_of(p)])
        plain.extend(plain_p)
        scaled.extend(scaled_p)
    gates = [(_dot_nt(kmh_ref[0, :, lanes_of(p)], plain[c])
              + _dot_nt(kml_ref[0, :, lanes_of(p)], plain[c]))[0:gate_rows]
             for c, (p, h) in enumerate(heads)]
    own_scores = [_dot_nt(k_ref[0, pl.ds(own, TILE), lanes_of(p)], scaled[c])
                  for c, (p, h) in enumerate(heads)]

    biases, state = [], []
    for c, (p, h) in enumerate(heads):
        gate = jnp.where(blk < i, gates[c], NEG_INF)
        bias = jnp.full((gate_rows, TILE), NEG_INF, F32)
        for _ in range(topk):
            best = jnp.max(gate, axis=0, keepdims=True)
            idx = jnp.min(jnp.where(gate == best, blk_f, float(gate_rows)), axis=0, keepdims=True)
            pick = blk_f == idx
            bias = jnp.where(pick, jnp.where(best > 0.5 * NEG_INF, 0.0, bias), bias)
            gate = jnp.where(pick, NEG_INF, gate)
        biases.append(bias)
        acc_ref[rows_of(p, h), :] = jnp.zeros((HEAD_DIM, TILE), F32)
        state.extend(_flash_step(jnp.where(causal, own_scores[c], NEG_INF), zero,
                                 vt_ref[0, rows_of(p, h), pl.ds(own, TILE)], m0, l0,
                                 acc_ref, rows_of(p, h)))

    def past_block(n, carry):
        start = pl.multiple_of(n * TILE, TILE)
        out = []
        scores = [_dot_nt(k_ref[0, pl.ds(start, TILE), lanes_of(p)], scaled[c])
                  for c, (p, h) in enumerate(heads)]
        for c, (p, h) in enumerate(heads):
            b = jnp.max(jnp.where(blk == n, biases[c], NEG_INF), axis=0, keepdims=True)
            out.extend(_flash_step(scores[c], b, vt_ref[0, rows_of(p, h), pl.ds(start, TILE)],
                                   carry[2 * c], carry[2 * c + 1], acc_ref, rows_of(p, h)))
        return tuple(out)

    state = lax.fori_loop(0, i, past_block, tuple(state))
    for p in range(pairs):
        l8s = (state[4 * p + 1], state[4 * p + 3])
        o_ref[0, :, lanes_of(p)] = _flash_finish(l8s, acc_ref.at[lanes_of(p)]).astype(BF16)


def _moba(qk, vvt, kmean_hi, kmean_lo, batch, seq):
    n_blk = seq // A_BLOCK
    gate_rows = -(-n_blk // SUBLANES) * SUBLANES
    kernel = functools.partial(_moba_kernel, topk=min(A_TOPK, n_blk), gate_rows=gate_rows)
    return pl.pallas_call(
        kernel,
        grid=(batch, n_blk),
        in_specs=[
            pl.BlockSpec((1, TILE, A_WIDTH), lambda b, i: (b, i, QK_AQ * LANES // A_WIDTH)),
            pl.BlockSpec((1, seq, A_WIDTH), lambda b, i: (b, 0, QK_AK * LANES // A_WIDTH)),
            pl.BlockSpec((1, A_WIDTH, seq), lambda b, i: (b, VV_AV * LANES // A_WIDTH, 0)),
            pl.BlockSpec((1, LANES, A_WIDTH), lambda b, i: (b, 0, 0)),
            pl.BlockSpec((1, LANES, A_WIDTH), lambda b, i: (b, 0, 0)),
        ],
        out_specs=pl.BlockSpec((1, TILE, A_WIDTH), lambda b, i: (b, i, 0)),
        out_shape=jax.ShapeDtypeStruct((batch, seq, A_WIDTH), BF16),
        scratch_shapes=[pltpu.VMEM((A_WIDTH, TILE), F32)],
        compiler_params=_params(2),
        name="moba",
    )(qk, qk, vvt, kmean_hi, kmean_lo)


def _compress_kernel(kin_ref, vin_ref, w1_ref, pos_ref, b1_ref, w2_ref, b2_ref, bd_ref, g_ref,
                     kc_ref, vc_ref):
    n_rows = kin_ref.shape[1]
    for c, (in_ref, out_ref) in enumerate(((kin_ref, kc_ref), (vin_ref, vc_ref))):
        r = in_ref[0]
        first = _dot(r, w1_ref[c, 0])
        second = _dot(r, w1_ref[c, 1])
        const = (_dot(pos_ref[c, 0], w1_ref[c, 0]) + _dot(pos_ref[c, 1], w1_ref[c, 1]))[0:1]
        hid = first + pltpu.roll(second, n_rows - 1, axis=0) + const + b1_ref[c]
        out = _dot(_gelu_tanh(hid).astype(BF16), w2_ref[c]) + b2_ref[c]
        if c == 0:
            out = _group_rmsnorm(out, bd_ref[...], g_ref[...])
        out_ref[0] = out.astype(BF16)


def _compress(kin, vin, w1, pos, b1, w2, b2, bd, gain, batch):
    n_rows = kin.shape[1]
    blk = lambda b: (b, 0, 0)
    hid2 = B_KV_HEADS * CMP_HIDDEN
    return pl.pallas_call(
        _compress_kernel,
        grid=(batch,),
        in_specs=[
            pl.BlockSpec((1, n_rows, CMP_STRIDE * LANES), blk),
            pl.BlockSpec((1, n_rows, CMP_STRIDE * LANES), blk),
            _const_spec((2, 2, CMP_STRIDE * LANES, hid2)),
            _const_spec((2, 2, SUBLANES, CMP_STRIDE * LANES)),
            _const_spec((2, 1, hid2)),
            _const_spec((2, hid2, LANES)),
            _const_spec((2, 1, LANES)),
            _const_spec((LANES, LANES)),
            _const_spec((1, LANES)),
        ],
        out_specs=[pl.BlockSpec((1, n_rows, LANES), blk), pl.BlockSpec((1, n_rows, LANES), blk)],
        out_shape=[jax.ShapeDtypeStruct((batch, n_rows, LANES), BF16)] * 2,
        compiler_params=_params(1),
        name="nsa_compress",
    )(kin, vin, w1, pos, b1, w2, b2, bd, gain)


def _nsa_cmp_kernel(q_ref, kc_ref, vc_ref, ov_ref, ocmp_ref, selb_ref, *, n_cmp, topn):
    i = pl.program_id(1)
    scale = HEAD_DIM ** -0.5
    n_pad = kc_ref.shape[1]
    lane = lax.broadcasted_iota(I32, (TILE, LANES), 1)
    pos = i * TILE + lax.broadcasted_iota(I32, (TILE, 1), 0)
    ncol = lax.broadcasted_iota(I32, (TILE, n_pad), 1)
    visible = (ncol * CMP_STRIDE + (CMP_LEN - 1) <= pos) & (ncol < n_cmp)
    kc = kc_ref[0]
    vc = vc_ref[0]
    ov_t = ov_ref[...]
    heads = [(p, h) for p in range(B_PAIRS) for h in range(2)]
    qs = []
    for p in range(B_PAIRS):
        q_pair = q_ref[0, :, p * LANES:(p + 1) * LANES]
        qs.extend(jnp.where((lane // HEAD_DIM) == h, q_pair, jnp.zeros_like(q_pair)) for h in range(2))
    scores = [_dot_nt(q, kc) for q in qs]
    probs = []
    for s in scores:
        s = jnp.where(visible, s * scale, NEG_INF)
        e = jnp.where(visible, jnp.exp(s - jnp.max(s, axis=-1, keepdims=True)), 0.0)
        tot = jnp.sum(e, axis=-1, keepdims=True)
        probs.append(jnp.where(tot > 0.0, e / tot, 0.0))
    outs = [_dot(prob.astype(BF16), vc) for prob in probs]
    parts = []
    for prob in probs:
        hi, lo = _split(prob)
        parts.append(_dot_nt(ov_t, hi) + _dot_nt(ov_t, lo))
    for p in range(B_PAIRS):
        o_pair = jnp.where(lane < HEAD_DIM, outs[2 * p], outs[2 * p + 1])
        ocmp_ref[0, :, p * LANES:(p + 1) * LANES] = o_pair.astype(BF16)

    blk = lax.broadcasted_iota(I32, (HEAD_DIM, TILE), 0)
    own = (i * TILE + lax.broadcasted_iota(I32, (1, TILE), 1)) // SLC_BLOCK
    forced = (blk == 0) | (blk == own) | (blk == own - 1)
    for g in range(B_KV_HEADS):
        imp = functools.reduce(lambda a, b: a + b, [parts[c] for c, (p, h) in enumerate(heads) if h == g])
        imp = jnp.where(forced, BIG, imp)
        imp = jnp.where(blk <= own, imp, NEG_INF)
        rank = jnp.zeros((HEAD_DIM, TILE), F32)
        for k in range(HEAD_DIM):
            other = imp[k:k + 1]
            ahead = jnp.where(other > imp, 1.0, jnp.where(other == imp, jnp.where(k < blk, 1.0, 0.0), 0.0))
            rank = rank + ahead
        keep = jnp.where(rank < float(topn), jnp.where(imp > 0.5 * NEG_INF, 0.0, NEG_INF), NEG_INF)
        selb_ref[0, g * HEAD_DIM:(g + 1) * HEAD_DIM, :] = keep


def _nsa_cmp(qk, kc, vc, ov, batch, seq):
    n_pad = kc.shape[1]
    n_cmp = (seq - CMP_LEN) // CMP_STRIDE + 1
    topn = min(SLC_TOPN, seq // SLC_BLOCK)
    kernel = functools.partial(_nsa_cmp_kernel, n_cmp=n_cmp, topn=topn)
    return pl.pallas_call(
        kernel,
        grid=(batch, seq // TILE),
        in_specs=[
            pl.BlockSpec((1, TILE, B_WIDTH), lambda b, i: (b, i, QK_BQ * LANES // B_WIDTH)),
            pl.BlockSpec((1, n_pad, LANES), lambda b, i: (b, 0, 0)),
            pl.BlockSpec((1, n_pad, LANES), lambda b, i: (b, 0, 0)),
            _const_spec((HEAD_DIM, n_pad)),
        ],
        out_specs=[
            pl.BlockSpec((1, TILE, B_WIDTH), lambda b, i: (b, i, 0)),
            pl.BlockSpec((1, LANES, TILE), lambda b, i: (b, 0, i)),
        ],
        out_shape=[
            jax.ShapeDtypeStruct((batch, seq, B_WIDTH), BF16),
            jax.ShapeDtypeStruct((batch, LANES, seq), F32),
        ],
        compiler_params=_params(2),
        name="nsa_cmp_select",
    )(qk, kc, vc, ov)


def _nsa_attn_kernel(q_ref, ks_ref, vst_ref, kw_ref, vwt_ref, selb_ref, ocmp_ref, gate_ref, egt_ref,
                     o_ref, acc_s_ref, acc_w_ref):
    i = pl.program_id(1)
    blocks_per_tile = TILE // SLC_BLOCK
    heads = [(p, h) for p in range(B_PAIRS) for h in range(2)]
    key_i = lax.broadcasted_iota(I32, (TILE, TILE), 0)
    qry_i = lax.broadcasted_iota(I32, (TILE, TILE), 1)
    causal = key_i <= qry_i
    sub = lax.broadcasted_iota(I32, (SUBLANES, TILE), 0)
    own = pl.multiple_of(i * TILE, TILE)
    lanes_of = lambda p: slice(p * LANES, (p + 1) * LANES)
    rows_of = lambda p, h: slice((2 * p + h) * HEAD_DIM, (2 * p + h + 1) * HEAD_DIM)
    kv_rows = (slice(0, HEAD_DIM), slice(HEAD_DIM, LANES))
    zero = jnp.zeros((1, TILE), F32)
    m0 = jnp.full((1, TILE), NEG_INF, F32)
    l0 = jnp.zeros((SUBLANES, TILE), F32)
    scaled = []
    for p in range(B_PAIRS):
        scaled.extend(_scaled_halves(q_ref[0, :, lanes_of(p)])[1])

    def block_biases(h, n):
        first = h * HEAD_DIM + n * blocks_per_tile
        base = pl.multiple_of((first >> 3) << 3, SUBLANES)
        rows8 = selb_ref[0, pl.ds(base, SUBLANES), :]
        off = first - base
        return [jnp.max(jnp.where(sub == off + r, rows8, NEG_INF), axis=0, keepdims=True)
                for r in range(blocks_per_tile)]

    def slc_step(p, h, n, bias, s_t, m, l8):
        parts = [s_t[r * SLC_BLOCK:(r + 1) * SLC_BLOCK] for r in range(blocks_per_tile)]
        cand = [_col_max(parts[r]) + bias[r] for r in range(blocks_per_tile)]
        m_new = functools.reduce(jnp.maximum, cand, m)
        p_t = jnp.concatenate([jnp.exp2(parts[r] - (m_new - bias[r])) for r in range(blocks_per_tile)],
                              axis=0)
        alpha = jnp.exp2(m - m_new)
        start = pl.multiple_of(n * TILE, TILE)
        v_t = vst_ref[0, kv_rows[h], pl.ds(start, TILE)]
        pv, psum = _values_and_sum(v_t, p_t)
        acc_s_ref[rows_of(p, h), :] = alpha * acc_s_ref[rows_of(p, h), :] + pv
        return m_new, alpha * l8 + psum

    def scores_of(k_ref_, start):
        k = k_ref_[0, pl.ds(start, TILE), :]
        return [_dot_nt(k, scaled[c]) for c in range(len(heads))]

    state = []
    own_bias = [block_biases(h, i) for h in range(2)]
    own_scores = scores_of(ks_ref, own)
    for c, (p, h) in enumerate(heads):
        acc_s_ref[rows_of(p, h), :] = jnp.zeros((HEAD_DIM, TILE), F32)
        state.extend(slc_step(p, h, i, own_bias[h], jnp.where(causal, own_scores[c], NEG_INF), m0, l0))

    def past_tile(n, carry):
        scores = scores_of(ks_ref, pl.multiple_of(n * TILE, TILE))
        bias = [block_biases(h, n) for h in range(2)]
        out = []
        for c, (p, h) in enumerate(heads):
            out.extend(slc_step(p, h, n, bias[h], scores[c], carry[2 * c], carry[2 * c + 1]))
        return tuple(out)

    state = lax.fori_loop(0, i, past_tile, tuple(state))

    wstate = [(m0, l0)] * len(heads)
    for back, mask in ((0, causal), (1, None), (2, key_i > qry_i)):
        start = pl.multiple_of(jnp.maximum(i - back, 0) * TILE, TILE)
        scores = scores_of(kw_ref, start)
        exists = jnp.where(i >= back, 0.0, NEG_INF) + zero
        for c, (p, h) in enumerate(heads):
            if back == 0:
                acc_w_ref[rows_of(p, h), :] = jnp.zeros((HEAD_DIM, TILE), F32)
            s_t = scores[c] if mask is None else jnp.where(mask, scores[c], NEG_INF)
            wstate[c] = _flash_step(s_t, exists, vwt_ref[0, kv_rows[h], pl.ds(start, TILE)],
                                    wstate[c][0], wstate[c][1], acc_w_ref, rows_of(p, h))
    wstate = [l8 for (_, l8) in wstate]

    for p in range(B_PAIRS):
        o_slc = _flash_finish((state[4 * p + 1], state[4 * p + 3]), acc_s_ref.at[lanes_of(p)])
        o_win = _flash_finish((wstate[2 * p], wstate[2 * p + 1]), acc_w_ref.at[lanes_of(p)])
        g = _split_dot(gate_ref[0], egt_ref[p])
        out = (g[:, 0:LANES] * ocmp_ref[0, :, lanes_of(p)].astype(F32) + g[:, LANES:2 * LANES] * o_slc
               + g[:, 2 * LANES:3 * LANES] * o_win)
        o_ref[0, :, lanes_of(p)] = out.astype(BF16)


def _nsa_attn(qk, vvt, selb, ocmp, gate, eg, batch, seq):
    assert WINDOW == 2 * TILE
    k_spec = lambda t: pl.BlockSpec((1, seq, LANES), lambda b, i: (b, 0, t))
    vt_spec = lambda t: pl.BlockSpec((1, LANES, seq), lambda b, i: (b, t, 0))
    return pl.pallas_call(
        _nsa_attn_kernel,
        grid=(batch, seq // TILE),
        in_specs=[
            pl.BlockSpec((1, TILE, B_WIDTH), lambda b, i: (b, i, QK_BQ * LANES // B_WIDTH)),
            k_spec(QK_BKS),
            vt_spec(VV_BVS),
            k_spec(QK_BKW),
            vt_spec(VV_BVW),
            pl.BlockSpec((1, LANES, TILE), lambda b, i: (b, 0, i)),
            pl.BlockSpec((1, TILE, B_WIDTH), lambda b, i: (b, i, 0)),
            pl.BlockSpec((1, TILE, LANES), lambda b, i: (b, i, 0)),
            _const_spec((B_PAIRS, LANES, 3 * LANES)),
        ],
        out_specs=pl.BlockSpec((1, TILE, B_WIDTH), lambda b, i: (b, i, 0)),
        out_shape=jax.ShapeDtypeStruct((batch, seq, B_WIDTH), BF16),
        scratch_shapes=[pltpu.VMEM((B_WIDTH, TILE), F32), pltpu.VMEM((B_WIDTH, TILE), F32)],
        compiler_params=_params(2),
        name="nsa_select_window",
    )(qk, qk, vvt, qk, vvt, selb, ocmp, gate, eg)


def _top_rows(v, row_f, count, fill):
    vals, rows = [], []
    for _ in range(count):
        best = jnp.max(v, axis=0, keepdims=True)
        idx = jnp.min(jnp.where(v == best, row_f, fill), axis=0, keepdims=True)
        vals.append(best)
        rows.append(idx)
        v = jnp.where(row_f == idx, -jnp.inf, v)
    return vals, rows


def _store_token_tiles(ref, val):
    for c in range(D_MODEL // LANES):
        ref[pl.ds(c, TILE, stride=D_MODEL // LANES), :] = val[:, c * LANES:(c + 1) * LANES]


def _merge_peer_kernel(x_ref, ya_ref, yb_ref, mg_ref, wa_ref, wb_ref, wo_ref, g2_ref, wq_ref,
                       k1h_ref, k1l_ref, k2h_ref, k2l_ref,
                       x1_ref, hn_ref, idx_ref, gt_ref, q_scr):
    ua = _dot(ya_ref[...], wa_ref[...])
    ub = _dot(yb_ref[...], wb_ref[...])
    merged = (mg_ref[:, 0:D_MODEL].astype(F32) * ua + mg_ref[:, D_MODEL:2 * D_MODEL].astype(F32) * ub)
    x1 = x_ref[...] + _dot(merged.astype(BF16), wo_ref[...])
    _store_token_tiles(x1_ref, x1)
    hn = x1 * lax.rsqrt(jnp.mean(x1 * x1, axis=-1, keepdims=True) + NORM_EPS) * g2_ref[...]
    _store_token_tiles(hn_ref, hn)
    q_scr[...] = _dot(hn.astype(BF16), wq_ref[...])

    key_row = lax.broadcasted_iota(I32, (PEER_NKEYS, TILE), 0).astype(F32)
    n_cand = sum(-(-(PEER_TOPK // (a + 1)) // SUBLANES) * SUBLANES for a in range(SUBLANES)) + SUBLANES
    cand_row = lax.broadcasted_iota(I32, (n_cand, TILE), 0).astype(F32)

    def head_scores(h):
        base = pl.multiple_of(h * PEER_QDIM, PEER_QDIM)
        q1h, q1l = _split(q_scr[:, pl.ds(base, PEER_HALF)])
        q2h, q2l = _split(q_scr[:, pl.ds(base + PEER_HALF, PEER_HALF)])
        k1h, k2h = k1h_ref[...], k2h_ref[...]
        s1 = _dot_nt(k1h, q1h) + _dot_nt(k1h, q1l) + _dot_nt(k1l_ref[...], q1h)
        s2 = _dot_nt(k2h, q2h) + _dot_nt(k2h, q2l) + _dot_nt(k2l_ref[...], q2h)
        return s1, s2

    def head_select(h, s1, s2):
        v1, i1 = _top_rows(s1, key_row, PEER_TOPK, float(PEER_NKEYS))
        v2, i2 = _top_rows(s2, key_row, PEER_TOPK, float(PEER_NKEYS))
        v1m, i1m = jnp.concatenate(v1, axis=0), jnp.concatenate(i1, axis=0)
        v2m, i2m = jnp.concatenate(v2, axis=0), jnp.concatenate(i2, axis=0)
        cand, cidx = [], []
        for a in range(SUBLANES):
            n_b = PEER_TOPK // (a + 1)
            rows = -(-n_b // SUBLANES) * SUBLANES
            keep = lax.broadcasted_iota(I32, (rows, TILE), 0) < n_b
            cand.append(jnp.where(keep, v1[a] + v2m[0:rows], -jnp.inf))
            cidx.append(i1[a] * float(PEER_NKEYS) + i2m[0:rows])
        cand.append(v1m[SUBLANES:] + v2[0])
        cidx.append(i1m[SUBLANES:] * float(PEER_NKEYS) + i2[0])
        cand = jnp.concatenate(cand, axis=0)
        cidx = jnp.concatenate(cidx, axis=0)
        vals, experts = [], []
        v = cand
        for _ in range(PEER_TOPK):
            best = jnp.max(v, axis=0, keepdims=True)
            at = jnp.min(jnp.where(v == best, cand_row, float(n_cand)), axis=0, keepdims=True)
            hit = cand_row == at
            experts.append(jnp.max(jnp.where(hit, cidx, -1.0), axis=0, keepdims=True))
            vals.append(best)
            v = jnp.where(hit, -jnp.inf, v)
        vals = jnp.concatenate(vals, axis=0)
        e = jnp.exp(vals - vals[0:1])
        out_row = pl.multiple_of(h * PEER_TOPK, PEER_TOPK)
        gt_ref[pl.ds(out_row, PEER_TOPK), :] = e / jnp.sum(e, axis=0, keepdims=True)
        first_row = jnp.concatenate(experts, axis=0) * float(HALF_ROWS)
        idx_ref[pl.ds(out_row, PEER_TOPK), :] = first_row.astype(I32)

    def head_group(i, carry):
        hs = [i * PEER_HEAD_UNROLL + k for k in range(PEER_HEAD_UNROLL)]
        scores = [head_scores(h) for h in hs]
        for h, (s1, s2) in zip(hs, scores):
            head_select(h, s1, s2)
        return carry

    lax.fori_loop(0, PEER_HEADS // PEER_HEAD_UNROLL, head_group, 0)


def _merge_peer(x2d, ya, yb, mg, wa, wb, wo, g2, wq, k1h, k1l, k2h, k2l):
    tokens = x2d.shape[0]
    row = lambda i: (i, 0)
    return pl.pallas_call(
        _merge_peer_kernel,
        grid=(tokens // TILE,),
        in_specs=[
            pl.BlockSpec((TILE, D_MODEL), row),
            pl.BlockSpec((TILE, A_WIDTH), row),
            pl.BlockSpec((TILE, B_WIDTH), row),
            pl.BlockSpec((TILE, MG_COLS), row),
            _const_spec((A_WIDTH, D_MODEL)),
            _const_spec((B_WIDTH, D_MODEL)),
            _const_spec((D_MODEL, D_MODEL)),
            _const_spec((1, D_MODEL)),
            _const_spec((D_MODEL, PEER_HEADS * PEER_QDIM)),
            _const_spec((PEER_NKEYS, PEER_HALF)),
            _const_spec((PEER_NKEYS, PEER_HALF)),
            _const_spec((PEER_NKEYS, PEER_HALF)),
            _const_spec((PEER_NKEYS, PEER_HALF)),
        ],
        out_specs=[
            pl.BlockSpec((TILE * SUBLANES, LANES), row),
            pl.BlockSpec((TILE * SUBLANES, LANES), row),
            pl.BlockSpec((PEER_SLOTS, TILE), lambda i: (0, i)),
            pl.BlockSpec((PEER_SLOTS, TILE), lambda i: (0, i)),
        ],
        out_shape=[
            jax.ShapeDtypeStruct((tokens * SUBLANES, LANES), F32),
            jax.ShapeDtypeStruct((tokens * SUBLANES, LANES), F32),
            jax.ShapeDtypeStruct((PEER_SLOTS, tokens), I32),
            jax.ShapeDtypeStruct((PEER_SLOTS, tokens), F32),
        ],
        scratch_shapes=[pltpu.VMEM((TILE, PEER_HEADS * PEER_QDIM), F32)],
        compiler_params=_params(1),
        name="merge_peer_topk",
    )(x2d, ya, yb, mg, wa, wb, wo, g2, wq, k1h, k1l, k2h, k2l)


HALF_ROWS = SUBLANES // 2
HI_MASK = -65536
ROW_MASK = 65535
PEER_UNROLL = 16
PEER_OUT_UNROLL = 2


def _unpack(words):
    hi = pltpu.bitcast(words & HI_MASK, F32)
    lo = pltpu.bitcast(words << 16, F32)
    return hi, lo


def _expert_row(tab_ref, row):
    return tab_ref[pl.ds(pl.multiple_of(row, HALF_ROWS), HALF_ROWS), :]


def _slot_reader(head_refs, t):
    return lambda j: head_refs[j // PEER_TOPK][j % PEER_TOPK, t]


def _head_smem_specs():
    return [pl.BlockSpec((PEER_TOPK, PEER_TB), lambda i, h=h: (h, i), memory_space=pltpu.SMEM)
            for h in range(PEER_HEADS)]


def _fold_rows(v, shift):
    return v + pltpu.roll(v, shift, axis=0)


def _peer_u_kernel(*refs):
    row_refs = refs[:PEER_HEADS]
    rowv_ref, hn_ref, gt_ref, tab_ref, route_ref = refs[PEER_HEADS:]
    lane = lax.broadcasted_iota(I32, (SUBLANES, LANES), 1)
    sub = lax.broadcasted_iota(I32, (SUBLANES, LANES), 0)
    keep_hi_pair = (sub % 4) >= 2
    odd = (sub % 2) == 1
    groups = PEER_SLOTS // SUBLANES

    def one_token(t, accs):
        x = hn_ref[t]
        xa = jnp.concatenate([x[0:HALF_ROWS], x[0:HALF_ROWS]], axis=0)
        xb = jnp.concatenate([x[HALF_ROWS:], x[HALF_ROWS:]], axis=0)

        slot_row = _slot_reader(row_refs, t)

        def pair(ja, jb):
            words = jnp.concatenate([_expert_row(tab_ref, slot_row(ja)),
                                     _expert_row(tab_ref, slot_row(jb))], axis=0)
            hi, lo = _unpack(words)
            return hi * xa + lo * xb

        new = list(accs)
        for g in [2 * h + half for half in range(2) for h in range(PEER_HEADS)]:
            j = g * SUBLANES
            quads = []
            for (a, b, c, d) in ((j + 3, j + 7, j + 1, j + 5), (j + 2, j + 6, j, j + 4)):
                w_ab = _fold_rows(pair(a, b), 2)
                w_cd = _fold_rows(pair(c, d), 2)
                z = jnp.where(keep_hi_pair, w_ab, pltpu.roll(w_cd, 6, axis=0))
                quads.append(_fold_rows(z, 1))
            folded = jnp.where(odd, quads[0], pltpu.roll(quads[1], 7, axis=0))
            total = jnp.sum(folded, axis=1, keepdims=True)
            new[g] = jnp.where(lane == t, total, accs[g])
        return tuple(new)

    def step(i, accs):
        for k in range(PEER_UNROLL):
            accs = one_token(i * PEER_UNROLL + k, accs)
        return accs

    init = tuple(jnp.zeros((SUBLANES, LANES), F32) for _ in range(groups))
    accs = lax.fori_loop(0, PEER_TB // PEER_UNROLL, step, init)
    a = jnp.concatenate(accs, axis=0)
    w = (_gelu_tanh(a) * gt_ref[...]).astype(BF16).astype(F32)
    route_ref[...] = (pltpu.bitcast(w, I32) & HI_MASK) | rowv_ref[...]


def _peer_u(rows, hn3, gt, tab):
    tokens = hn3.shape[0]
    blk = lambda i: (0, i)
    return pl.pallas_call(
        _peer_u_kernel,
        grid=(tokens // PEER_TB,),
        in_specs=_head_smem_specs() + [
            pl.BlockSpec((PEER_SLOTS, PEER_TB), blk),
            pl.BlockSpec((PEER_TB, SUBLANES, LANES), lambda i: (i, 0, 0)),
            pl.BlockSpec((PEER_SLOTS, PEER_TB), blk),
            _const_spec((PEER_N * HALF_ROWS, LANES)),
        ],
        out_specs=pl.BlockSpec((PEER_SLOTS, PEER_TB), blk),
        out_shape=jax.ShapeDtypeStruct((PEER_SLOTS, tokens), I32),
        compiler_params=_params(1),
        name="peer_expert_in",
    )(*([rows] * PEER_HEADS), rows, hn3, gt, tab)


def _peer_v_kernel(*refs):
    route_refs = refs[:PEER_HEADS]
    x1_ref, tab_ref, o_ref = refs[PEER_HEADS:]

    upper = lax.broadcasted_iota(I32, (SUBLANES, LANES), 0) >= HALF_ROWS

    def one_token(t):
        acc_hi = jnp.zeros((SUBLANES, LANES), F32)
        acc_lo = jnp.zeros((SUBLANES, LANES), F32)
        slot_word = _slot_reader(route_refs, t)
        for j in [h * PEER_TOPK + k for k in range(0, PEER_TOPK, 2) for h in range(PEER_HEADS)]:
            wa, wb = slot_word(j), slot_word(j + 1)
            words = jnp.concatenate([_expert_row(tab_ref, wa & ROW_MASK),
                                     _expert_row(tab_ref, wb & ROW_MASK)], axis=0)
            hi, lo = _unpack(words)
            both = jnp.where(upper, jnp.full((SUBLANES, LANES), wb, I32),
                             jnp.full((SUBLANES, LANES), wa, I32))
            w = pltpu.bitcast(both & HI_MASK, F32)
            acc_hi = acc_hi + w * hi
            acc_lo = acc_lo + w * lo
        out = jnp.concatenate([acc_hi[0:HALF_ROWS] + acc_hi[HALF_ROWS:],
                               acc_lo[0:HALF_ROWS] + acc_lo[HALF_ROWS:]], axis=0)
        o_ref[t] = x1_ref[t] + out

    def step(i, carry):
        for k in range(PEER_OUT_UNROLL):
            one_token(i * PEER_OUT_UNROLL + k)
        return carry

    lax.fori_loop(0, PEER_TB // PEER_OUT_UNROLL, step, 0)


def _peer_v(route, x13, tab):
    tokens = x13.shape[0]
    return pl.pallas_call(
        _peer_v_kernel,
        grid=(tokens // PEER_TB,),
        in_specs=_head_smem_specs() + [
            pl.BlockSpec((PEER_TB, SUBLANES, LANES), lambda i: (i, 0, 0)),
            _const_spec((PEER_N * HALF_ROWS, LANES)),
        ],
        out_specs=pl.BlockSpec((PEER_TB, SUBLANES, LANES), lambda i: (i, 0, 0)),
        out_shape=jax.ShapeDtypeStruct((tokens, SUBLANES, LANES), F32),
        compiler_params=_params(1),
        name="peer_expert_out",
    )(*([route] * PEER_HEADS), x13, tab)


def _pack_table(t):
    bits = lax.bitcast_convert_type(t.astype(BF16), jnp.uint16).astype(jnp.uint32)
    half = D_MODEL // 2
    words = (bits[:, :half] << 16) | bits[:, half:]
    return lax.bitcast_convert_type(words, I32).reshape(t.shape[0] * HALF_ROWS, LANES)


def _rope_tables(seq):
    inv = ROPE_THETA ** (-jnp.arange(0, ROT_DIM, 2, dtype=F32) / ROT_DIM)
    ang = jnp.arange(seq, dtype=F32)[:, None] * inv[None, :]
    d = np.arange(LANES) % HEAD_DIM
    cos = jnp.where(d[None, :] < ROT_DIM, jnp.cos(ang)[:, d % ROT_HALF], 1.0)
    sin = jnp.where(d[None, :] < ROT_DIM, jnp.sin(ang)[:, d % ROT_HALF], 0.0)
    return cos.astype(F32), sin.astype(F32)


def _compress_weights(w1, pos):
    out_w, out_p = [], []
    for part in range(2):
        wpart = w1[part * CMP_STRIDE * HEAD_DIM:(part + 1) * CMP_STRIDE * HEAD_DIM]
        wpart = wpart.reshape(CMP_STRIDE, HEAD_DIM, CMP_HIDDEN)
        full = jnp.einsum("ldj,gh->lgdhj", wpart, jnp.eye(B_KV_HEADS, dtype=F32))
        out_w.append(full.reshape(CMP_STRIDE * LANES, B_KV_HEADS * CMP_HIDDEN))
        ppart = pos[part * CMP_STRIDE:(part + 1) * CMP_STRIDE]
        prow = jnp.tile(ppart[:, None, :], (1, B_KV_HEADS, 1)).reshape(1, CMP_STRIDE * LANES)
        out_p.append(jnp.tile(prow, (SUBLANES, 1)))
    return jnp.stack(out_w).astype(BF16), jnp.stack(out_p).astype(BF16)


def kernel(x, norm1_g, w_in, b_merge, a_q_g, a_k_g, b_q_g, b_kc_g, b_ks_g, b_kw_g, cmp_pos_k,
           cmp_k_w1, cmp_k_b1, cmp_k_w2, cmp_k_b2, cmp_pos_v, cmp_v_w1, cmp_v_b1, cmp_v_w2,
           cmp_v_b2, w_up_a, w_up_b, w_out, norm2_g, peer_wq, peer_k1, peer_k2, peer_u, peer_v):
    batch, seq, _ = x.shape
    assert seq % TILE == 0 and seq // SLC_BLOCK <= HEAD_DIM
    tokens = batch * seq
    l = 0
    x2d = x.reshape(tokens, D_MODEL)

    w = w_in[l]
    sizes = (A_WIDTH, A_WIDTH, A_WIDTH, B_WIDTH) + (B_KV_WIDTH,) * 6 + (3 * B_HEADS, MG_COLS)
    offs = np.concatenate([[0], np.cumsum(sizes)])
    seg = lambda k: w[:, offs[k]:offs[k + 1]]
    aq, ak, av, bq, bkc, bvc, bks, bvs, bkw, bvw, bgate, mgate = (seg(k) for k in range(12))
    bq_perm = bq.reshape(D_MODEL, B_HEADS, HEAD_DIM)[:, np.array(B_HEAD_ORDER)].reshape(D_MODEL, B_WIDTH)
    w_rope = jnp.concatenate([aq, ak, bq_perm, bks, bkw], axis=1).astype(BF16)
    w_plain = jnp.concatenate([av, bvs, bvw], axis=1).T.astype(BF16)
    w_small = jnp.concatenate([bkc, bvc, jnp.pad(bgate, ((0, 0), (0, LANES - 3 * B_HEADS)))],
                              axis=1).astype(BF16)
    hg = jnp.concatenate([jnp.tile(a_q_g[l], A_HEADS), jnp.tile(a_k_g[l], A_HEADS),
                          jnp.tile(b_q_g[l], B_HEADS), jnp.tile(b_ks_g[l], B_KV_HEADS),
                          jnp.tile(b_kw_g[l], B_KV_HEADS)])[None, :]
    cos_t, sin_t = _rope_tables(seq)
    bd = jnp.asarray(np.kron(np.eye(LANES // HEAD_DIM), np.full((HEAD_DIM, HEAD_DIM), 1.0 / HEAD_DIM)),
                     BF16)

    qk, vvt, kcin, vcin, gate, mg, kmean = _inproj(
        x2d, norm1_g[l][None, :], w_rope, w_plain, w_small, mgate.astype(BF16), hg, cos_t, sin_t,
        jnp.kron(jnp.eye(2, dtype=BF16), bd), b_merge[l].reshape(1, MG_COLS), seq)
    qk = qk.reshape(batch, seq, ROPE_COLS)

    n_blk = seq // A_BLOCK
    kmean = jnp.pad(kmean.reshape(batch, n_blk, A_WIDTH), ((0, 0), (0, LANES - n_blk), (0, 0)))
    km_hi = kmean.astype(BF16)
    km_lo = (kmean - km_hi.astype(F32)).astype(BF16)
    ya = _moba(qk, vvt, km_hi, km_lo, batch, seq)

    n_rows = seq // CMP_STRIDE
    wk1, pk = _compress_weights(cmp_k_w1[l], cmp_pos_k[l])
    wv1, pv = _compress_weights(cmp_v_w1[l], cmp_pos_v[l])
    blockdiag = lambda m: jnp.kron(jnp.eye(B_KV_HEADS, dtype=F32), m)
    w2 = jnp.stack([blockdiag(cmp_k_w2[l]), blockdiag(cmp_v_w2[l])]).astype(BF16)
    b1 = jnp.stack([jnp.tile(cmp_k_b1[l], B_KV_HEADS), jnp.tile(cmp_v_b1[l], B_KV_HEADS)])[:, None, :]
    b2 = jnp.stack([jnp.tile(cmp_k_b2[l], B_KV_HEADS), jnp.tile(cmp_v_b2[l], B_KV_HEADS)])[:, None, :]
    kc, vc = _compress(kcin.reshape(batch, n_rows, CMP_STRIDE * LANES),
                       vcin.reshape(batch, n_rows, CMP_STRIDE * LANES),
                       jnp.stack([wk1, wv1]), jnp.stack([pk, pv]), b1, w2, b2, bd,
                       jnp.tile(b_kc_g[l], B_KV_HEADS)[None, :], batch)

    n_cmp = (seq - CMP_LEN) // CMP_STRIDE + 1
    n_slc = seq // SLC_BLOCK
    ci = np.arange(n_rows)[:, None]
    sj = np.arange(HEAD_DIM)[None, :]
    ov = ((ci * CMP_STRIDE < (sj + 1) * SLC_BLOCK) & (ci * CMP_STRIDE + CMP_LEN > sj * SLC_BLOCK)
          & (ci < n_cmp) & (sj < n_slc)).astype(np.float32)
    ov = jnp.asarray(ov.T, BF16)
    ocmp, selb = _nsa_cmp(qk, kc, vc, ov, batch, seq)

    eg = np.zeros((B_PAIRS, LANES, 3 * LANES), np.float32)
    for p in range(B_PAIRS):
        for half in range(2):
            head = B_HEAD_ORDER[2 * p + half]
            for c in range(3):
                eg[p, head * 3 + c, c * LANES + half * HEAD_DIM:c * LANES + (half + 1) * HEAD_DIM] = 1.0
    yb = _nsa_attn(qk, vvt, selb, ocmp, gate.reshape(batch, seq, LANES), jnp.asarray(eg, BF16),
                   batch, seq)

    wb_perm = w_up_b[l].reshape(B_HEADS, HEAD_DIM, D_MODEL)[np.array(B_HEAD_ORDER)].reshape(
        B_WIDTH, D_MODEL)
    k1h, k1l = _split(peer_k1[l])
    k2h, k2l = _split(peer_k2[l])
    x1, hn, rows, gt = _merge_peer(
        x2d, ya.reshape(tokens, A_WIDTH), yb.reshape(tokens, B_WIDTH), mg,
        w_up_a[l].astype(BF16), wb_perm.astype(BF16), w_out[l].astype(BF16),
        norm2_g[l][None, :], peer_wq[l].astype(BF16), k1h, k1l, k2h, k2l)

    route = _peer_u(rows, hn.reshape(tokens, SUBLANES, LANES), gt, _pack_table(peer_u[l]))
    out = _peer_v(route, x1.reshape(tokens, SUBLANES, LANES), _pack_table(peer_v[l]))
    return out.reshape(batch, seq, D_MODEL)
```

```python
import functools
import math

import jax
import jax.numpy as jnp
import numpy as np
from jax import lax
from jax.experimental import pallas as pl
from jax.experimental.pallas import tpu as pltpu

D_MODEL = 1024
HEAD_DIM = 64
ROT_DIM = HEAD_DIM // 4
ROT_HALF = ROT_DIM // 2
ROPE_THETA = 500000.0
NORM_EPS = 1e-6
NEG_INF = -1e30
BIG = 1e30

A_HEADS = 8
A_BLOCK = 256
A_TOPK = 3

B_HEADS = 8
B_KV_HEADS = 2
B_GROUP = B_HEADS // B_KV_HEADS
CMP_LEN = 32
CMP_STRIDE = 16
CMP_HIDDEN = 256
SLC_BLOCK = 64
SLC_TOPN = 16
WINDOW = 512

PEER_HEADS = 8
PEER_NKEYS = 128
PEER_N = PEER_NKEYS * PEER_NKEYS
PEER_QDIM = 256
PEER_HALF = PEER_QDIM // 2
PEER_TOPK = 16
PEER_SLOTS = PEER_HEADS * PEER_TOPK
PEER_HEAD_UNROLL = 4

A_WIDTH = A_HEADS * HEAD_DIM
B_WIDTH = B_HEADS * HEAD_DIM
B_KV_WIDTH = B_KV_HEADS * HEAD_DIM
N_BRANCH = 2

LANES = 128
SUBLANES = 8
VMEM_LIMIT_BYTES = 56 * 1024 * 1024

ROPE_COLS = 2 * A_WIDTH + B_WIDTH + 2 * B_KV_WIDTH
QK_AQ, QK_AK, QK_BQ, QK_BKS, QK_BKW = 0, 4, 8, 12, 13
PLAIN_COLS = A_WIDTH + 2 * B_KV_WIDTH
VV_AV, VV_BVS, VV_BVW = 0, 4, 5
MG_COLS = N_BRANCH * D_MODEL
B_PAIRS = B_WIDTH // LANES
B_HEAD_ORDER = tuple(h for p in range(B_PAIRS) for h in (p, B_GROUP + p))

TILE = 256
PEER_TB = 128
F32 = jnp.float32
BF16 = jnp.bfloat16
I32 = jnp.int32


def _const_spec(shape):
    n = len(shape)
    return pl.BlockSpec(shape, lambda *_: (0,) * n, pipeline_mode=pl.Buffered(1))


def _params(n_axes):
    return pltpu.CompilerParams(dimension_semantics=("arbitrary",) * n_axes,
                                vmem_limit_bytes=VMEM_LIMIT_BYTES)


def _dot(a, b):
    return jnp.dot(a, b, preferred_element_type=F32)


def _dot_nt(a, b):
    return lax.dot_general(a, b, (((1,), (1,)), ((), ())), preferred_element_type=F32)


def _split(a_f32):
    hi = a_f32.astype(BF16)
    lo = (a_f32 - hi.astype(F32)).astype(BF16)
    return hi, lo


def _split_dot(a_f32, b_bf16):
    hi, lo = _split(a_f32)
    return _dot(hi, b_bf16) + _dot(lo, b_bf16)


def _gelu_tanh(x):
    return 0.5 * x * (1.0 + jnp.tanh(math.sqrt(2.0 / math.pi) * (x + 0.044715 * (x * x * x))))


def _group_rmsnorm(y, bd, gain):
    ms = _split_dot(y * y, bd)
    return y * lax.rsqrt(ms + NORM_EPS) * gain


def _inproj_kernel(x_ref, g1_ref, w_rope_ref, w_plain_ref, w_small_ref, w_mg_ref, hg_ref,
                   cos_ref, sin_ref, bd_ref, bm_ref,
                   qk_ref, vvt_ref, kcin_ref, vcin_ref, gate_ref, mg_ref, kmean_ref):
    x = x_ref[...]
    h = x * lax.rsqrt(jnp.mean(x * x, axis=-1, keepdims=True) + NORM_EPS) * g1_ref[...]
    hb = h.astype(BF16)

    y_all = _dot(hb, w_rope_ref[...])
    vvt_ref[0] = _dot_nt(w_plain_ref[...], hb).astype(BF16)
    small = _dot(hb, w_small_ref[...])
    kcin_ref[...] = small[:, 0:LANES].astype(BF16)
    vcin_ref[...] = small[:, LANES:2 * LANES].astype(BF16)
    gate_ref[...] = jax.nn.sigmoid(small[:, 2 * LANES:3 * LANES])
    mg_ref[...] = jax.nn.sigmoid(_dot(hb, w_mg_ref[...]) + bm_ref[...]).astype(BF16)

    cos = cos_ref[...]
    sin = sin_ref[...]
    bd = bd_ref[...]
    lane = lax.broadcasted_iota(I32, (TILE, LANES), 1)
    first = (lane % HEAD_DIM) < ROT_HALF
    wide = 2 * LANES
    for t2 in range(ROPE_COLS // wide):
        y2 = y_all[:, t2 * wide:(t2 + 1) * wide]
        ms2 = _split_dot(y2 * y2, bd)
        for half in range(2):
            t = 2 * t2 + half
            sl = slice(t * LANES, (t + 1) * LANES)
            hs = slice(half * LANES, (half + 1) * LANES)
            yn = y2[:, hs] * lax.rsqrt(ms2[:, hs] + NORM_EPS) * hg_ref[:, sl]
            up = pltpu.roll(yn, LANES - ROT_HALF, axis=1)
            dn = pltpu.roll(yn, ROT_HALF, axis=1)
            out = yn * cos + jnp.where(first, -up, dn) * sin
            qk_ref[:, sl] = out.astype(BF16)
            if QK_AK <= t < QK_AK + A_WIDTH // LANES:
                c = t - QK_AK
                kmean_ref[0, :, c * LANES:(c + 1) * LANES] = jnp.mean(out, axis=0, keepdims=True)


def _inproj(x2d, g1, w_rope, w_plain, w_small, w_mg, hg, cos_t, sin_t, bd2, bm, seq):
    tokens = x2d.shape[0]
    nt = tokens // TILE
    tiles_per_seq = seq // TILE
    row = lambda i: (i, 0)
    pos = lambda i: (i % tiles_per_seq, 0)
    return pl.pallas_call(
        _inproj_kernel,
        grid=(nt,),
        in_specs=[
            pl.BlockSpec((TILE, D_MODEL), row),
            _const_spec((1, D_MODEL)),
            _const_spec((D_MODEL, ROPE_COLS)),
            _const_spec((PLAIN_COLS, D_MODEL)),
            _const_spec((D_MODEL, 3 * LANES)),
            _const_spec((D_MODEL, MG_COLS)),
            _const_spec((1, ROPE_COLS)),
            pl.BlockSpec((TILE, LANES), pos),
            pl.BlockSpec((TILE, LANES), pos),
            _const_spec((2 * LANES, 2 * LANES)),
            _const_spec((1, MG_COLS)),
        ],
        out_specs=[
            pl.BlockSpec((TILE, ROPE_COLS), row),
            pl.BlockSpec((1, PLAIN_COLS, TILE), lambda i: (i // tiles_per_seq, 0, i % tiles_per_seq)),
            pl.BlockSpec((TILE, LANES), row),
            pl.BlockSpec((TILE, LANES), row),
            pl.BlockSpec((TILE, LANES), row),
            pl.BlockSpec((TILE, MG_COLS), row),
            pl.BlockSpec((1, 1, A_WIDTH), lambda i: (i, 0, 0)),
        ],
        out_shape=[
            jax.ShapeDtypeStruct((tokens, ROPE_COLS), BF16),
            jax.ShapeDtypeStruct((tokens // seq, PLAIN_COLS, seq), BF16),
            jax.ShapeDtypeStruct((tokens, LANES), BF16),
            jax.ShapeDtypeStruct((tokens, LANES), BF16),
            jax.ShapeDtypeStruct((tokens, LANES), F32),
            jax.ShapeDtypeStruct((tokens, MG_COLS), BF16),
            jax.ShapeDtypeStruct((nt, 1, A_WIDTH), F32),
        ],
        compiler_params=_params(1),
        name="inproj",
    )(x2d, g1, w_rope, w_plain, w_small, w_mg, hg, cos_t, sin_t, bd2, bm)


LOG2E = math.log2(math.e)


def _scaled_halves(q_pair):
    lane = lax.broadcasted_iota(I32, (TILE, LANES), 1)
    q = q_pair.astype(F32)
    c = HEAD_DIM ** -0.5 * LOG2E
    plain = [jnp.where((lane // HEAD_DIM) == h, q, 0.0).astype(BF16) for h in range(2)]
    scaled = [jnp.where((lane // HEAD_DIM) == h, q * c, 0.0).astype(BF16) for h in range(2)]
    return plain, scaled


def _col_max(s_t):
    part = jnp.max(s_t.reshape(-1, SUBLANES, s_t.shape[-1]), axis=0)
    return jnp.max(part, axis=0, keepdims=True)


def _values_and_sum(v_t, p_t):
    ones = jnp.ones((2 * SUBLANES, v_t.shape[1]), BF16)
    res = _dot(jnp.concatenate([v_t, ones], axis=0), p_t.astype(BF16))
    return res[0:HEAD_DIM], res[HEAD_DIM:HEAD_DIM + SUBLANES]


def _flash_step(s_t, shift_bias, v_t, m, l8, acc_ref, rows):
    m_new = jnp.maximum(m, _col_max(s_t) + shift_bias)
    p_t = jnp.exp2(s_t - (m_new - shift_bias))
    alpha = jnp.exp2(m - m_new)
    pv, psum = _values_and_sum(v_t, p_t)
    l8 = alpha * l8 + psum
    acc_ref[rows, :] = alpha * acc_ref[rows, :] + pv
    return m_new, l8


def _flash_finish(l8s, acc_ref):
    inv = [1.0 / l8[0:1] for l8 in l8s]
    out_t = jnp.concatenate([acc_ref[0:HEAD_DIM, :] * inv[0], acc_ref[HEAD_DIM:, :] * inv[1]], axis=0)
    return jnp.transpose(out_t)


def _moba_kernel(q_ref, k_ref, vt_ref, kmh_ref, kml_ref, o_ref, acc_ref, *, topk, gate_rows):
    i = pl.program_id(1)
    pairs = A_WIDTH // LANES
    heads = [(p, h) for p in range(pairs) for h in range(2)]
    key_i = lax.broadcasted_iota(I32, (TILE, TILE), 0)
    qry_i = lax.broadcasted_iota(I32, (TILE, TILE), 1)
    causal = key_i <= qry_i
    blk = lax.broadcasted_iota(I32, (gate_rows, TILE), 0)
    blk_f = blk.astype(F32)
    own = pl.multiple_of(i * TILE, TILE)
    lanes_of = lambda p: slice(p * LANES, (p + 1) * LANES)
    rows_of = lambda p, h: slice((2 * p + h) * HEAD_DIM, (2 * p + h + 1) * HEAD_DIM)
    zero = jnp.zeros((1, TILE), F32)
    m0 = jnp.full((1, TILE), NEG_INF, F32)
    l0 = jnp.zeros((SUBLANES, TILE), F32)

    plain, scaled = [], []
    for p in range(pairs):
        plain_p, scaled_p = _scaled_halves(q_ref[0, :, lanes_of(p)])
        plain.extend(plain_p)
        scaled.extend(scaled_p)
    gates = [(_dot_nt(kmh_ref[0, :, lanes_of(p)], plain[c])
              + _dot_nt(kml_ref[0, :, lanes_of(p)], plain[c]))[0:gate_rows]
             for c, (p, h) in enumerate(heads)]
    own_scores = [_dot_nt(k_ref[0, pl.ds(own, TILE), lanes_of(p)], scaled[c])
                  for c, (p, h) in enumerate(heads)]

    biases, state = [], []
    for c, (p, h) in enumerate(heads):
        gate = jnp.where(blk < i, gates[c], NEG_INF)
        bias = jnp.full((gate_rows, TILE), NEG_INF, F32)
        for _ in range(topk):
            best = jnp.max(gate, axis=0, keepdims=True)
            idx = jnp.min(jnp.where(gate == best, blk_f, float(gate_rows)), axis=0, keepdims=True)
            pick = blk_f == idx
            bias = jnp.where(pick, jnp.where(best > 0.5 * NEG_INF, 0.0, bias), bias)
            gate = jnp.where(pick, NEG_INF, gate)
        biases.append(bias)
        acc_ref[rows_of(p, h), :] = jnp.zeros((HEAD_DIM, TILE), F32)
        state.extend(_flash_step(jnp.where(causal, own_scores[c], NEG_INF), zero,
                                 vt_ref[0, rows_of(p, h), pl.ds(own, TILE)], m0, l0,
                                 acc_ref, rows_of(p, h)))

    def past_block(n, carry):
        start = pl.multiple_of(n * TILE, TILE)
        out = []
        scores = [_dot_nt(k_ref[0, pl.ds(start, TILE), lanes_of(p)], scaled[c])
                  for c, (p, h) in enumerate(heads)]
        for c, (p, h) in enumerate(heads):
            b = jnp.max(jnp.where(blk == n, biases[c], NEG_INF), axis=0, keepdims=True)
            out.extend(_flash_step(scores[c], b, vt_ref[0, rows_of(p, h), pl.ds(start, TILE)],
                                   carry[2 * c], carry[2 * c + 1], acc_ref, rows_of(p, h)))
        return tuple(out)

    state = lax.fori_loop(0, i, past_block, tuple(state))
    for p in range(pairs):
        l8s = (state[4 * p + 1], state[4 * p + 3])
        o_ref[0, :, lanes_of(p)] = _flash_finish(l8s, acc_ref.at[lanes_of(p)]).astype(BF16)


def _moba(qk, vvt, kmean_hi, kmean_lo, batch, seq):
    n_blk = seq // A_BLOCK
    gate_rows = -(-n_blk // SUBLANES) * SUBLANES
    kernel = functools.partial(_moba_kernel, topk=min(A_TOPK, n_blk), gate_rows=gate_rows)
    return pl.pallas_call(
        kernel,
        grid=(batch, n_blk),
        in_specs=[
            pl.BlockSpec((1, TILE, A_WIDTH), lambda b, i: (b, i, QK_AQ * LANES // A_WIDTH)),
            pl.BlockSpec((1, seq, A_WIDTH), lambda b, i: (b, 0, QK_AK * LANES // A_WIDTH)),
            pl.BlockSpec((1, A_WIDTH, seq), lambda b, i: (b, VV_AV * LANES // A_WIDTH, 0)),
            pl.BlockSpec((1, LANES, A_WIDTH), lambda b, i: (b, 0, 0)),
            pl.BlockSpec((1, LANES, A_WIDTH), lambda b, i: (b, 0, 0)),
        ],
        out_specs=pl.BlockSpec((1, TILE, A_WIDTH), lambda b, i: (b, i, 0)),
        out_shape=jax.ShapeDtypeStruct((batch, seq, A_WIDTH), BF16),
        scratch_shapes=[pltpu.VMEM((A_WIDTH, TILE), F32)],
        compiler_params=_params(2),
        name="moba",
    )(qk, qk, vvt, kmean_hi, kmean_lo)


def _compress_kernel(kin_ref, vin_ref, w1_ref, pos_ref, b1_ref, w2_ref, b2_ref, bd_ref, g_ref,
                     kc_ref, vc_ref):
    n_rows = kin_ref.shape[1]
    for c, (in_ref, out_ref) in enumerate(((kin_ref, kc_ref), (vin_ref, vc_ref))):
        r = in_ref[0]
        first = _dot(r, w1_ref[c, 0])
        second = _dot(r, w1_ref[c, 1])
        const = (_dot(pos_ref[c, 0], w1_ref[c, 0]) + _dot(pos_ref[c, 1], w1_ref[c, 1]))[0:1]
        hid = first + pltpu.roll(second, n_rows - 1, axis=0) + const + b1_ref[c]
        out = _dot(_gelu_tanh(hid).astype(BF16), w2_ref[c]) + b2_ref[c]
        if c == 0:
            out = _group_rmsnorm(out, bd_ref[...], g_ref[...])
        out_ref[0] = out.astype(BF16)


def _compress(kin, vin, w1, pos, b1, w2, b2, bd, gain, batch):
    n_rows = kin.shape[1]
    blk = lambda b: (b, 0, 0)
    hid2 = B_KV_HEADS * CMP_HIDDEN
    return pl.pallas_call(
        _compress_kernel,
        grid=(batch,),
        in_specs=[
            pl.BlockSpec((1, n_rows, CMP_STRIDE * LANES), blk),
            pl.BlockSpec((1, n_rows, CMP_STRIDE * LANES), blk),
            _const_spec((2, 2, CMP_STRIDE * LANES, hid2)),
            _const_spec((2, 2, SUBLANES, CMP_STRIDE * LANES)),
            _const_spec((2, 1, hid2)),
            _const_spec((2, hid2, LANES)),
            _const_spec((2, 1, LANES)),
            _const_spec((LANES, LANES)),
            _const_spec((1, LANES)),
        ],
        out_specs=[pl.BlockSpec((1, n_rows, LANES), blk), pl.BlockSpec((1, n_rows, LANES), blk)],
        out_shape=[jax.ShapeDtypeStruct((batch, n_rows, LANES), BF16)] * 2,
        compiler_params=_params(1),
        name="nsa_compress",
    )(kin, vin, w1, pos, b1, w2, b2, bd, gain)


def _nsa_cmp_kernel(q_ref, kc_ref, vc_ref, ov_ref, ocmp_ref, selb_ref, *, n_cmp, topn):
    i = pl.program_id(1)
    scale = HEAD_DIM ** -0.5
    n_pad = kc_ref.shape[1]
    lane = lax.broadcasted_iota(I32, (TILE, LANES), 1)
    pos = i * TILE + lax.broadcasted_iota(I32, (TILE, 1), 0)
    ncol = lax.broadcasted_iota(I32, (TILE, n_pad), 1)
    visible = (ncol * CMP_STRIDE + (CMP_LEN - 1) <= pos) & (ncol < n_cmp)
    kc = kc_ref[0]
    vc = vc_ref[0]
    ov_t = ov_ref[...]
    heads = [(p, h) for p in range(B_PAIRS) for h in range(2)]
    qs = []
    for p in range(B_PAIRS):
        q_pair = q_ref[0, :, p * LANES:(p + 1) * LANES]
        qs.extend(jnp.where((lane // HEAD_DIM) == h, q_pair, jnp.zeros_like(q_pair)) for h in range(2))
    scores = [_dot_nt(q, kc) for q in qs]
    probs = []
    for s in scores:
        s = jnp.where(visible, s * scale, NEG_INF)
        e = jnp.where(visible, jnp.exp(s - jnp.max(s, axis=-1, keepdims=True)), 0.0)
        tot = jnp.sum(e, axis=-1, keepdims=True)
        probs.append(jnp.where(tot > 0.0, e / tot, 0.0))
    outs = [_dot(prob.astype(BF16), vc) for prob in probs]
    parts = []
    for prob in probs:
        hi, lo = _split(prob)
        parts.append(_dot_nt(ov_t, hi) + _dot_nt(ov_t, lo))
    for p in range(B_PAIRS):
        o_pair = jnp.where(lane < HEAD_DIM, outs[2 * p], outs[2 * p + 1])
        ocmp_ref[0, :, p * LANES:(p + 1) * LANES] = o_pair.astype(BF16)

    blk = lax.broadcasted_iota(I32, (HEAD_DIM, TILE), 0)
    own = (i * TILE + lax.broadcasted_iota(I32, (1, TILE), 1)) // SLC_BLOCK
    forced = (blk == 0) | (blk == own) | (blk == own - 1)
    for g in range(B_KV_HEADS):
        imp = functools.reduce(lambda a, b: a + b, [parts[c] for c, (p, h) in enumerate(heads) if h == g])
        imp = jnp.where(forced, BIG, imp)
        imp = jnp.where(blk <= own, imp, NEG_INF)
        rank = jnp.zeros((HEAD_DIM, TILE), F32)
        for k in range(HEAD_DIM):
            other = imp[k:k + 1]
            ahead = jnp.where(other > imp, 1.0, jnp.where(other == imp, jnp.where(k < blk, 1.0, 0.0), 0.0))
            rank = rank + ahead
        keep = jnp.where(rank < float(topn), jnp.where(imp > 0.5 * NEG_INF, 0.0, NEG_INF), NEG_INF)
        selb_ref[0, g * HEAD_DIM:(g + 1) * HEAD_DIM, :] = keep


def _nsa_cmp(qk, kc, vc, ov, batch, seq):
    n_pad = kc.shape[1]
    n_cmp = (seq - CMP_LEN) // CMP_STRIDE + 1
    topn = min(SLC_TOPN, seq // SLC_BLOCK)
    kernel = functools.partial(_nsa_cmp_kernel, n_cmp=n_cmp, topn=topn)
    return pl.pallas_call(
        kernel,
        grid=(batch, seq // TILE),
        in_specs=[
            pl.BlockSpec((1, TILE, B_WIDTH), lambda b, i: (b, i, QK_BQ * LANES // B_WIDTH)),
            pl.BlockSpec((1, n_pad, LANES), lambda b, i: (b, 0, 0)),
            pl.BlockSpec((1, n_pad, LANES), lambda b, i: (b, 0, 0)),
            _const_spec((HEAD_DIM, n_pad)),
        ],
        out_specs=[
            pl.BlockSpec((1, TILE, B_WIDTH), lambda b, i: (b, i, 0)),
            pl.BlockSpec((1, LANES, TILE), lambda b, i: (b, 0, i)),
        ],
        out_shape=[
            jax.ShapeDtypeStruct((batch, seq, B_WIDTH), BF16),
            jax.ShapeDtypeStruct((batch, LANES, seq), F32),
        ],
        compiler_params=_params(2),
        name="nsa_cmp_select",
    )(qk, kc, vc, ov)


def _nsa_attn_kernel(q_ref, ks_ref, vst_ref, kw_ref, vwt_ref, selb_ref, ocmp_ref, gate_ref, egt_ref,
                     o_ref, acc_s_ref, acc_w_ref):
    i = pl.program_id(1)
    blocks_per_tile = TILE // SLC_BLOCK
    heads = [(p, h) for p in range(B_PAIRS) for h in range(2)]
    key_i = lax.broadcasted_iota(I32, (TILE, TILE), 0)
    qry_i = lax.broadcasted_iota(I32, (TILE, TILE), 1)
    causal = key_i <= qry_i
    sub = lax.broadcasted_iota(I32, (SUBLANES, TILE), 0)
    own = pl.multiple_of(i * TILE, TILE)
    lanes_of = lambda p: slice(p * LANES, (p + 1) * LANES)
    rows_of = lambda p, h: slice((2 * p + h) * HEAD_DIM, (2 * p + h + 1) * HEAD_DIM)
    kv_rows = (slice(0, HEAD_DIM), slice(HEAD_DIM, LANES))
    zero = jnp.zeros((1, TILE), F32)
    m0 = jnp.full((1, TILE), NEG_INF, F32)
    l0 = jnp.zeros((SUBLANES, TILE), F32)
    scaled = []
    for p in range(B_PAIRS):
        scaled.extend(_scaled_halves(q_ref[0, :, lanes_of(p)])[1])

    def block_biases(h, n):
        first = h * HEAD_DIM + n * blocks_per_tile
        base = pl.multiple_of((first >> 3) << 3, SUBLANES)
        rows8 = selb_ref[0, pl.ds(base, SUBLANES), :]
        off = first - base
        return [jnp.max(jnp.where(sub == off + r, rows8, NEG_INF), axis=0, keepdims=True)
                for r in range(blocks_per_tile)]

    def slc_step(p, h, n, bias, s_t, m, l8):
        parts = [s_t[r * SLC_BLOCK:(r + 1) * SLC_BLOCK] for r in range(blocks_per_tile)]
        cand = [_col_max(parts[r]) + bias[r] for r in range(blocks_per_tile)]
        m_new = functools.reduce(jnp.maximum, cand, m)
        p_t = jnp.concatenate([jnp.exp2(parts[r] - (m_new - bias[r])) for r in range(blocks_per_tile)],
                              axis=0)
        alpha = jnp.exp2(m - m_new)
        start = pl.multiple_of(n * TILE, TILE)
        v_t = vst_ref[0, kv_rows[h], pl.ds(start, TILE)]
        pv, psum = _values_and_sum(v_t, p_t)
        acc_s_ref[rows_of(p, h), :] = alpha * acc_s_ref[rows_of(p, h), :] + pv
        return m_new, alpha * l8 + psum

    def scores_of(k_ref_, start):
        k = k_ref_[0, pl.ds(start, TILE), :]
        return [_dot_nt(k, scaled[c]) for c in range(len(heads))]

    state = []
    own_bias = [block_biases(h, i) for h in range(2)]
    own_scores = scores_of(ks_ref, own)
    for c, (p, h) in enumerate(heads):
        acc_s_ref[rows_of(p, h), :] = jnp.zeros((HEAD_DIM, TILE), F32)
        state.extend(slc_step(p, h, i, own_bias[h], jnp.where(causal, own_scores[c], NEG_INF), m0, l0))

    def past_tile(n, carry):
        scores = scores_of(ks_ref, pl.multiple_of(n * TILE, TILE))
        bias = [block_biases(h, n) for h in range(2)]
        out = []
        for c, (p, h) in enumerate(heads):
            out.extend(slc_step(p, h, n, bias[h], scores[c], carry[2 * c], carry[2 * c + 1]))
        return tuple(out)

    state = lax.fori_loop(0, i, past_tile, tuple(state))

    wstate = [(m0, l0)] * len(heads)
    for back, mask in ((0, causal), (1, None), (2, key_i > qry_i)):
        start = pl.multiple_of(jnp.maximum(i - back, 0) * TILE, TILE)
        scores = scores_of(kw_ref, start)
        exists = jnp.where(i >= back, 0.0, NEG_INF) + zero
        for c, (p, h) in enumerate(heads):
            if back == 0:
                acc_w_ref[rows_of(p, h), :] = jnp.zeros((HEAD_DIM, TILE), F32)
            s_t = scores[c] if mask is None else jnp.where(mask, scores[c], NEG_INF)
            wstate[c] = _flash_step(s_t, exists, vwt_ref[0, kv_rows[h], pl.ds(start, TILE)],
                                    wstate[c][0], wstate[c][1], acc_w_ref, rows_of(p, h))
    wstate = [l8 for (_, l8) in wstate]

    for p in range(B_PAIRS):
        o_slc = _flash_finish((state[4 * p + 1], state[4 * p + 3]), acc_s_ref.at[lanes_of(p)])
        o_win = _flash_finish((wstate[2 * p], wstate[2 * p + 1]), acc_w_ref.at[lanes_of(p)])
        g = _split_dot(gate_ref[0], egt_ref[p])
        out = (g[:, 0:LANES] * ocmp_ref[0, :, lanes_of(p)].astype(F32) + g[:, LANES:2 * LANES] * o_slc
               + g[:, 2 * LANES:3 * LANES] * o_win)
        o_ref[0, :, lanes_of(p)] = out.astype(BF16)


def _nsa_attn(qk, vvt, selb, ocmp, gate, eg, batch, seq):
    assert WINDOW == 2 * TILE
    k_spec = lambda t: pl.BlockSpec((1, seq, LANES), lambda b, i: (b, 0, t))
    vt_spec = lambda t: pl.BlockSpec((1, LANES, seq), lambda b, i: (b, t, 0))
    return pl.pallas_call(
        _nsa_attn_kernel,
        grid=(batch, seq // TILE),
        in_specs=[
            pl.BlockSpec((1, TILE, B_WIDTH), lambda b, i: (b, i, QK_BQ * LANES // B_WIDTH)),
            k_spec(QK_BKS),
            vt_spec(VV_BVS),
            k_spec(QK_BKW),
            vt_spec(VV_BVW),
            pl.BlockSpec((1, LANES, TILE), lambda b, i: (b, 0, i)),
            pl.BlockSpec((1, TILE, B_WIDTH), lambda b, i: (b, i, 0)),
            pl.BlockSpec((1, TILE, LANES), lambda b, i: (b, i, 0)),
            _const_spec((B_PAIRS, LANES, 3 * LANES)),
        ],
        out_specs=pl.BlockSpec((1, TILE, B_WIDTH), lambda b, i: (b, i, 0)),
        out_shape=jax.ShapeDtypeStruct((batch, seq, B_WIDTH), BF16),
        scratch_shapes=[pltpu.VMEM((B_WIDTH, TILE), F32), pltpu.VMEM((B_WIDTH, TILE), F32)],
        compiler_params=_params(2),
        name="nsa_select_window",
    )(qk, qk, vvt, qk, vvt, selb, ocmp, gate, eg)


def _top_rows(v, row_f, count, fill):
    vals, rows = [], []
    for _ in range(count):
        best = jnp.max(v, axis=0, keepdims=True)
        idx = jnp.min(jnp.where(v == best, row_f, fill), axis=0, keepdims=True)
        vals.append(best)
        rows.append(idx)
        v = jnp.where(row_f == idx, -jnp.inf, v)
    return vals, rows


def _store_token_tiles(ref, val):
    for c in range(D_MODEL // LANES):
        ref[pl.ds(c, TILE, stride=D_MODEL // LANES), :] = val[:, c * LANES:(c + 1) * LANES]


def _merge_peer_kernel(x_ref, ya_ref, yb_ref, mg_ref, wa_ref, wb_ref, wo_ref, g2_ref, wq_ref,
                       k1h_ref, k1l_ref, k2h_ref, k2l_ref,
                       x1_ref, hn_ref, idx_ref, gt_ref, q_scr):
    ua = _dot(ya_ref[...], wa_ref[...])
    ub = _dot(yb_ref[...], wb_ref[...])
    merged = (mg_ref[:, 0:D_MODEL].astype(F32) * ua + mg_ref[:, D_MODEL:2 * D_MODEL].astype(F32) * ub)
    x1 = x_ref[...] + _dot(merged.astype(BF16), wo_ref[...])
    _store_token_tiles(x1_ref, x1)
    hn = x1 * lax.rsqrt(jnp.mean(x1 * x1, axis=-1, keepdims=True) + NORM_EPS) * g2_ref[...]
    _store_token_tiles(hn_ref, hn)
    q_scr[...] = _dot(hn.astype(BF16), wq_ref[...])

    key_row = lax.broadcasted_iota(I32, (PEER_NKEYS, TILE), 0).astype(F32)
    n_cand = 7 * SUBLANES
    cand_row = lax.broadcasted_iota(I32, (n_cand, TILE), 0).astype(F32)

    def head_scores(h):
        base = pl.multiple_of(h * PEER_QDIM, PEER_QDIM)
        q1h, q1l = _split(q_scr[:, pl.ds(base, PEER_HALF)])
        q2h, q2l = _split(q_scr[:, pl.ds(base + PEER_HALF, PEER_HALF)])
        k1h, k2h = k1h_ref[...], k2h_ref[...]
        s1 = _dot_nt(k1h, q1h) + _dot_nt(k1h, q1l) + _dot_nt(k1l_ref[...], q1h)
        s2 = _dot_nt(k2h, q2h) + _dot_nt(k2h, q2l) + _dot_nt(k2l_ref[...], q2h)
        return s1, s2

    def head_select(h, s1, s2):
        v1, i1 = _top_rows(s1, key_row, PEER_TOPK, float(PEER_NKEYS))
        v2, i2 = _top_rows(s2, key_row, PEER_TOPK, float(PEER_NKEYS))
        v1m, i1m = jnp.concatenate(v1, axis=0), jnp.concatenate(i1, axis=0)
        v2m, i2m = jnp.concatenate(v2, axis=0), jnp.concatenate(i2, axis=0)
        sub8 = lax.broadcasted_iota(I32, (SUBLANES, TILE), 0)
        lo_v, lo_i = v1m[0:SUBLANES], i1m[0:SUBLANES] * float(PEER_NKEYS)
        mid = (sub8 >= 2)

        def shifted(x, rows_up):
            return pltpu.roll(x, rows_up, axis=0)

        cand = [v1[0] + v2m,
                v1[1] + v2m[0:SUBLANES],
                v1m[SUBLANES:] + v2[0],
                jnp.where(mid, lo_v + v2[0], -jnp.inf),
                jnp.where(mid, lo_v + v2[1], -jnp.inf),
                jnp.where(sub8 < 2, -jnp.inf,
                          jnp.where(sub8 < 5, lo_v + v2[2],
                                    jnp.where(sub8 < 7, shifted(lo_v, 3) + v2[3],
                                              shifted(lo_v, 5) + v2[4])))]
        cidx = [i1[0] * float(PEER_NKEYS) + i2m,
                i1[1] * float(PEER_NKEYS) + i2m[0:SUBLANES],
                i1m[SUBLANES:] * float(PEER_NKEYS) + i2[0],
                lo_i + i2[0],
                lo_i + i2[1],
                jnp.where(sub8 < 5, lo_i + i2[2],
                          jnp.where(sub8 < 7, shifted(lo_i, 3) + i2[3], shifted(lo_i, 5) + i2[4]))]
        cand = jnp.concatenate(cand, axis=0)
        cidx = jnp.concatenate(cidx, axis=0)
        vals, experts = [], []
        v = cand
        for _ in range(PEER_TOPK):
            best = jnp.max(v, axis=0, keepdims=True)
            at = jnp.min(jnp.where(v == best, cand_row, float(n_cand)), axis=0, keepdims=True)
            hit = cand_row == at
            experts.append(jnp.max(jnp.where(hit, cidx, -1.0), axis=0, keepdims=True))
            vals.append(best)
            v = jnp.where(hit, -jnp.inf, v)
        vals = jnp.concatenate(vals, axis=0)
        e = jnp.exp(vals - vals[0:1])
        out_row = pl.multiple_of(h * PEER_TOPK, PEER_TOPK)
        gt_ref[pl.ds(out_row, PEER_TOPK), :] = e / jnp.sum(e, axis=0, keepdims=True)
        first_row = jnp.concatenate(experts, axis=0) * float(HALF_ROWS)
        idx_ref[pl.ds(out_row, PEER_TOPK), :] = first_row.astype(I32)

    def head_group(i, carry):
        hs = [i * PEER_HEAD_UNROLL + k for k in range(PEER_HEAD_UNROLL)]
        scores = [head_scores(h) for h in hs]
        for h, (s1, s2) in zip(hs, scores):
            head_select(h, s1, s2)
        return carry

    lax.fori_loop(0, PEER_HEADS // PEER_HEAD_UNROLL, head_group, 0)


def _merge_peer(x2d, ya, yb, mg, wa, wb, wo, g2, wq, k1h, k1l, k2h, k2l):
    tokens = x2d.shape[0]
    row = lambda i: (i, 0)
    return pl.pallas_call(
        _merge_peer_kernel,
        grid=(tokens // TILE,),
        in_specs=[
            pl.BlockSpec((TILE, D_MODEL), row),
            pl.BlockSpec((TILE, A_WIDTH), row),
            pl.BlockSpec((TILE, B_WIDTH), row),
            pl.BlockSpec((TILE, MG_COLS), row),
            _const_spec((A_WIDTH, D_MODEL)),
            _const_spec((B_WIDTH, D_MODEL)),
            _const_spec((D_MODEL, D_MODEL)),
            _const_spec((1, D_MODEL)),
            _const_spec((D_MODEL, PEER_HEADS * PEER_QDIM)),
            _const_spec((PEER_NKEYS, PEER_HALF)),
            _const_spec((PEER_NKEYS, PEER_HALF)),
            _const_spec((PEER_NKEYS, PEER_HALF)),
            _const_spec((PEER_NKEYS, PEER_HALF)),
        ],
        out_specs=[
            pl.BlockSpec((TILE * SUBLANES, LANES), row),
            pl.BlockSpec((TILE * SUBLANES, LANES), row),
            pl.BlockSpec((PEER_SLOTS, TILE), lambda i: (0, i)),
            pl.BlockSpec((PEER_SLOTS, TILE), lambda i: (0, i)),
        ],
        out_shape=[
            jax.ShapeDtypeStruct((tokens * SUBLANES, LANES), F32),
            jax.ShapeDtypeStruct((tokens * SUBLANES, LANES), F32),
            jax.ShapeDtypeStruct((PEER_SLOTS, tokens), I32),
            jax.ShapeDtypeStruct((PEER_SLOTS, tokens), F32),
        ],
        scratch_shapes=[pltpu.VMEM((TILE, PEER_HEADS * PEER_QDIM), F32)],
        compiler_params=_params(1),
        name="merge_peer_topk",
    )(x2d, ya, yb, mg, wa, wb, wo, g2, wq, k1h, k1l, k2h, k2l)


HALF_ROWS = SUBLANES // 2
HI_MASK = -65536
ROW_MASK = 65535
PEER_UNROLL = 16
PEER_OUT_UNROLL = 2


def _unpack(words):
    hi = pltpu.bitcast(words & HI_MASK, F32)
    lo = pltpu.bitcast(words << 16, F32)
    return hi, lo


def _expert_row(tab_ref, row):
    return tab_ref[pl.ds(pl.multiple_of(row, HALF_ROWS), HALF_ROWS), :]


def _slot_reader(head_refs, t):
    return lambda j: head_refs[j // PEER_TOPK][j % PEER_TOPK, t]


def _head_smem_specs():
    return [pl.BlockSpec((PEER_TOPK, PEER_TB), lambda i, h=h: (h, i), memory_space=pltpu.SMEM)
            for h in range(PEER_HEADS)]


def _fold_rows(v, shift):
    return v + pltpu.roll(v, shift, axis=0)


def _peer_u_kernel(*refs):
    row_refs = refs[:PEER_HEADS]
    rowv_ref, hn_ref, gt_ref, tab_ref, route_ref = refs[PEER_HEADS:]
    lane = lax.broadcasted_iota(I32, (SUBLANES, LANES), 1)
    sub = lax.broadcasted_iota(I32, (SUBLANES, LANES), 0)
    keep_hi_pair = (sub % 4) >= 2
    odd = (sub % 2) == 1
    groups = PEER_SLOTS // SUBLANES

    def one_token(t, accs):
        x = hn_ref[t]
        xa = jnp.concatenate([x[0:HALF_ROWS], x[0:HALF_ROWS]], axis=0)
        xb = jnp.concatenate([x[HALF_ROWS:], x[HALF_ROWS:]], axis=0)

        slot_row = _slot_reader(row_refs, t)

        def pair(ja, jb):
            words = jnp.concatenate([_expert_row(tab_ref, slot_row(ja)),
                                     _expert_row(tab_ref, slot_row(jb))], axis=0)
            hi, lo = _unpack(words)
            return hi * xa + lo * xb

        new = list(accs)
        for g in [2 * h + half for half in range(2) for h in range(PEER_HEADS)]:
            j = g * SUBLANES
            quads = []
            for (a, b, c, d) in ((j + 3, j + 7, j + 1, j + 5), (j + 2, j + 6, j, j + 4)):
                w_ab = _fold_rows(pair(a, b), 2)
                w_cd = _fold_rows(pair(c, d), 2)
                z = jnp.where(keep_hi_pair, w_ab, pltpu.roll(w_cd, 6, axis=0))
                quads.append(_fold_rows(z, 1))
            folded = jnp.where(odd, quads[0], pltpu.roll(quads[1], 7, axis=0))
            total = jnp.sum(folded, axis=1, keepdims=True)
            new[g] = jnp.where(lane == t, total, accs[g])
        return tuple(new)

    def step(i, accs):
        for k in range(PEER_UNROLL):
            accs = one_token(i * PEER_UNROLL + k, accs)
        return accs

    init = tuple(jnp.zeros((SUBLANES, LANES), F32) for _ in range(groups))
    accs = lax.fori_loop(0, PEER_TB // PEER_UNROLL, step, init)
    a = jnp.concatenate(accs, axis=0)
    w = (_gelu_tanh(a) * gt_ref[...]).astype(BF16).astype(F32)
    route_ref[...] = (pltpu.bitcast(w, I32) & HI_MASK) | rowv_ref[...]


def _peer_u(rows, hn3, gt, tab):
    tokens = hn3.shape[0]
    blk = lambda i: (0, i)
    return pl.pallas_call(
        _peer_u_kernel,
        grid=(tokens // PEER_TB,),
        in_specs=_head_smem_specs() + [
            pl.BlockSpec((PEER_SLOTS, PEER_TB), blk),
            pl.BlockSpec((PEER_TB, SUBLANES, LANES), lambda i: (i, 0, 0)),
            pl.BlockSpec((PEER_SLOTS, PEER_TB), blk),
            _const_spec((PEER_N * HALF_ROWS, LANES)),
        ],
        out_specs=pl.BlockSpec((PEER_SLOTS, PEER_TB), blk),
        out_shape=jax.ShapeDtypeStruct((PEER_SLOTS, tokens), I32),
        compiler_params=_params(1),
        name="peer_expert_in",
    )(*([rows] * PEER_HEADS), rows, hn3, gt, tab)


def _peer_v_kernel(*refs):
    route_refs = refs[:PEER_HEADS]
    x1_ref, tab_ref, o_ref = refs[PEER_HEADS:]

    upper = lax.broadcasted_iota(I32, (SUBLANES, LANES), 0) >= HALF_ROWS

    def one_token(t):
        acc_hi = jnp.zeros((SUBLANES, LANES), F32)
        acc_lo = jnp.zeros((SUBLANES, LANES), F32)
        slot_word = _slot_reader(route_refs, t)
        for j in [h * PEER_TOPK + k for k in range(0, PEER_TOPK, 2) for h in range(PEER_HEADS)]:
            wa, wb = slot_word(j), slot_word(j + 1)
            words = jnp.concatenate([_expert_row(tab_ref, wa & ROW_MASK),
                                     _expert_row(tab_ref, wb & ROW_MASK)], axis=0)
            hi, lo = _unpack(words)
            both = jnp.where(upper, jnp.full((SUBLANES, LANES), wb, I32),
                             jnp.full((SUBLANES, LANES), wa, I32))
            w = pltpu.bitcast(both & HI_MASK, F32)
            acc_hi = acc_hi + w * hi
            acc_lo = acc_lo + w * lo
        out = jnp.concatenate([acc_hi[0:HALF_ROWS] + acc_hi[HALF_ROWS:],
                               acc_lo[0:HALF_ROWS] + acc_lo[HALF_ROWS:]], axis=0)
        o_ref[t] = x1_ref[t] + out

    def step(i, carry):
        for k in range(PEER_OUT_UNROLL):
            one_token(i * PEER_OUT_UNROLL + k)
        return carry

    lax.fori_loop(0, PEER_TB // PEER_OUT_UNROLL, step, 0)


def _peer_v(route, x13, tab):
    tokens = x13.shape[0]
    return pl.pallas_call(
        _peer_v_kernel,
        grid=(tokens // PEER_TB,),
        in_specs=_head_smem_specs() + [
            pl.BlockSpec((PEER_TB, SUBLANES, LANES), lambda i: (i, 0, 0)),
            _const_spec((PEER_N * HALF_ROWS, LANES)),
        ],
        out_specs=pl.BlockSpec((PEER_TB, SUBLANES, LANES), lambda i: (i, 0, 0)),
        out_shape=jax.ShapeDtypeStruct((tokens, SUBLANES, LANES), F32),
        compiler_params=_params(1),
        name="peer_expert_out",
    )(*([route] * PEER_HEADS), x13, tab)


def _pack_table(t):
    bits = lax.bitcast_convert_type(t.astype(BF16), jnp.uint16).astype(jnp.uint32)
    half = D_MODEL // 2
    words = (bits[:, :half] << 16) | bits[:, half:]
    return lax.bitcast_convert_type(words, I32).reshape(t.shape[0] * HALF_ROWS, LANES)


def _rope_tables(seq):
    inv = ROPE_THETA ** (-jnp.arange(0, ROT_DIM, 2, dtype=F32) / ROT_DIM)
    ang = jnp.arange(seq, dtype=F32)[:, None] * inv[None, :]
    d = np.arange(LANES) % HEAD_DIM
    cos = jnp.where(d[None, :] < ROT_DIM, jnp.cos(ang)[:, d % ROT_HALF], 1.0)
    sin = jnp.where(d[None, :] < ROT_DIM, jnp.sin(ang)[:, d % ROT_HALF], 0.0)
    return cos.astype(F32), sin.astype(F32)


def _compress_weights(w1, pos):
    out_w, out_p = [], []
    for part in range(2):
        wpart = w1[part * CMP_STRIDE * HEAD_DIM:(part + 1) * CMP_STRIDE * HEAD_DIM]
        wpart = wpart.reshape(CMP_STRIDE, HEAD_DIM, CMP_HIDDEN)
        full = jnp.einsum("ldj,gh->lgdhj", wpart, jnp.eye(B_KV_HEADS, dtype=F32))
        out_w.append(full.reshape(CMP_STRIDE * LANES, B_KV_HEADS * CMP_HIDDEN))
        ppart = pos[part * CMP_STRIDE:(part + 1) * CMP_STRIDE]
        prow = jnp.tile(ppart[:, None, :], (1, B_KV_HEADS, 1)).reshape(1, CMP_STRIDE * LANES)
        out_p.append(jnp.tile(prow, (SUBLANES, 1)))
    return jnp.stack(out_w).astype(BF16), jnp.stack(out_p).astype(BF16)


def kernel(x, norm1_g, w_in, b_merge, a_q_g, a_k_g, b_q_g, b_kc_g, b_ks_g, b_kw_g, cmp_pos_k,
           cmp_k_w1, cmp_k_b1, cmp_k_w2, cmp_k_b2, cmp_pos_v, cmp_v_w1, cmp_v_b1, cmp_v_w2,
           cmp_v_b2, w_up_a, w_up_b, w_out, norm2_g, peer_wq, peer_k1, peer_k2, peer_u, peer_v):
    batch, seq, _ = x.shape
    assert seq % TILE == 0 and seq // SLC_BLOCK <= HEAD_DIM
    tokens = batch * seq
    l = 0
    x2d = x.reshape(tokens, D_MODEL)

    w = w_in[l]
    sizes = (A_WIDTH, A_WIDTH, A_WIDTH, B_WIDTH) + (B_KV_WIDTH,) * 6 + (3 * B_HEADS, MG_COLS)
    offs = np.concatenate([[0], np.cumsum(sizes)])
    seg = lambda k: w[:, offs[k]:offs[k + 1]]
    aq, ak, av, bq, bkc, bvc, bks, bvs, bkw, bvw, bgate, mgate = (seg(k) for k in range(12))
    bq_perm = bq.reshape(D_MODEL, B_HEADS, HEAD_DIM)[:, np.array(B_HEAD_ORDER)].reshape(D_MODEL, B_WIDTH)
    w_rope = jnp.concatenate([aq, ak, bq_perm, bks, bkw], axis=1).astype(BF16)
    w_plain = jnp.concatenate([av, bvs, bvw], axis=1).T.astype(BF16)
    w_small = jnp.concatenate([bkc, bvc, jnp.pad(bgate, ((0, 0), (0, LANES - 3 * B_HEADS)))],
                              axis=1).astype(BF16)
    hg = jnp.concatenate([jnp.tile(a_q_g[l], A_HEADS), jnp.tile(a_k_g[l], A_HEADS),
                          jnp.tile(b_q_g[l], B_HEADS), jnp.tile(b_ks_g[l], B_KV_HEADS),
                          jnp.tile(b_kw_g[l], B_KV_HEADS)])[None, :]
    cos_t, sin_t = _rope_tables(seq)
    bd = jnp.asarray(np.kron(np.eye(LANES // HEAD_DIM), np.full((HEAD_DIM, HEAD_DIM), 1.0 / HEAD_DIM)),
                     BF16)

    qk, vvt, kcin, vcin, gate, mg, kmean = _inproj(
        x2d, norm1_g[l][None, :], w_rope, w_plain, w_small, mgate.astype(BF16), hg, cos_t, sin_t,
        jnp.kron(jnp.eye(2, dtype=BF16), bd), b_merge[l].reshape(1, MG_COLS), seq)
    qk = qk.reshape(batch, seq, ROPE_COLS)

    n_blk = seq // A_BLOCK
    kmean = jnp.pad(kmean.reshape(batch, n_blk, A_WIDTH), ((0, 0), (0, LANES - n_blk), (0, 0)))
    km_hi = kmean.astype(BF16)
    km_lo = (kmean - km_hi.astype(F32)).astype(BF16)
    ya = _moba(qk, vvt, km_hi, km_lo, batch, seq)

    n_rows = seq // CMP_STRIDE
    wk1, pk = _compress_weights(cmp_k_w1[l], cmp_pos_k[l])
    wv1, pv = _compress_weights(cmp_v_w1[l], cmp_pos_v[l])
    blockdiag = lambda m: jnp.kron(jnp.eye(B_KV_HEADS, dtype=F32), m)
    w2 = jnp.stack([blockdiag(cmp_k_w2[l]), blockdiag(cmp_v_w2[l])]).astype(BF16)
    b1 = jnp.stack([jnp.tile(cmp_k_b1[l], B_KV_HEADS), jnp.tile(cmp_v_b1[l], B_KV_HEADS)])[:, None, :]
    b2 = jnp.stack([jnp.tile(cmp_k_b2[l], B_KV_HEADS), jnp.tile(cmp_v_b2[l], B_KV_HEADS)])[:, None, :]
    kc, vc = _compress(kcin.reshape(batch, n_rows, CMP_STRIDE * LANES),
                       vcin.reshape(batch, n_rows, CMP_STRIDE * LANES),
                       jnp.stack([wk1, wv1]), jnp.stack([pk, pv]), b1, w2, b2, bd,
                       jnp.tile(b_kc_g[l], B_KV_HEADS)[None, :], batch)

    n_cmp = (seq - CMP_LEN) // CMP_STRIDE + 1
    n_slc = seq // SLC_BLOCK
    ci = np.arange(n_rows)[:, None]
    sj = np.arange(HEAD_DIM)[None, :]
    ov = ((ci * CMP_STRIDE < (sj + 1) * SLC_BLOCK) & (ci * CMP_STRIDE + CMP_LEN > sj * SLC_BLOCK)
          & (ci < n_cmp) & (sj < n_slc)).astype(np.float32)
    ov = jnp.asarray(ov.T, BF16)
    ocmp, selb = _nsa_cmp(qk, kc, vc, ov, batch, seq)

    eg = np.zeros((B_PAIRS, LANES, 3 * LANES), np.float32)
    for p in range(B_PAIRS):
        for half in range(2):
            head = B_HEAD_ORDER[2 * p + half]
            for c in range(3):
                eg[p, head * 3 + c, c * LANES + half * HEAD_DIM:c * LANES + (half + 1) * HEAD_DIM] = 1.0
    yb = _nsa_attn(qk, vvt, selb, ocmp, gate.reshape(batch, seq, LANES), jnp.asarray(eg, BF16),
                   batch, seq)

    wb_perm = w_up_b[l].reshape(B_HEADS, HEAD_DIM, D_MODEL)[np.array(B_HEAD_ORDER)].reshape(
        B_WIDTH, D_MODEL)
    k1h, k1l = _split(peer_k1[l])
    k2h, k2l = _split(peer_k2[l])
    x1, hn, rows, gt = _merge_peer(
        x2d, ya.reshape(tokens, A_WIDTH), yb.reshape(tokens, B_WIDTH), mg,
        w_up_a[l].astype(BF16), wb_perm.astype(BF16), w_out[l].astype(BF16),
        norm2_g[l][None, :], peer_wq[l].astype(BF16), k1h, k1l, k2h, k2l)

    route = _peer_u(rows, hn.reshape(tokens, SUBLANES, LANES), gt, _pack_table(peer_u[l]))
    out = _peer_v(route, x1.reshape(tokens, SUBLANES, LANES), _pack_table(peer_v[l]))
    return out.reshape(batch, seq, D_MODEL)
```

```python
import functools
import math

import jax
import jax.numpy as jnp
import numpy as np
from jax import lax
from jax.experimental import pallas as pl
from jax.experimental.pallas import tpu as pltpu

D_MODEL = 1024
HEAD_DIM = 64
ROT_DIM = HEAD_DIM // 4
ROT_HALF = ROT_DIM // 2
ROPE_THETA = 500000.0
NORM_EPS = 1e-6
NEG_INF = -1e30
BIG = 1e30

A_HEADS = 8
A_BLOCK = 256
A_TOPK = 3

B_HEADS = 8
B_KV_HEADS = 2
B_GROUP = B_HEADS // B_KV_HEADS
CMP_LEN = 32
CMP_STRIDE = 16
CMP_HIDDEN = 256
SLC_BLOCK = 64
SLC_TOPN = 16
WINDOW = 512

PEER_HEADS = 8
PEER_NKEYS = 128
PEER_N = PEER_NKEYS * PEER_NKEYS
PEER_QDIM = 256
PEER_HALF = PEER_QDIM // 2
PEER_TOPK = 16
PEER_SLOTS = PEER_HEADS * PEER_TOPK
PEER_HEAD_UNROLL = 4

A_WIDTH = A_HEADS * HEAD_DIM
B_WIDTH = B_HEADS * HEAD_DIM
B_KV_WIDTH = B_KV_HEADS * HEAD_DIM
N_BRANCH = 2

LANES = 128
SUBLANES = 8
VMEM_LIMIT_BYTES = 56 * 1024 * 1024

ROPE_COLS = 2 * A_WIDTH + B_WIDTH + 2 * B_KV_WIDTH
QK_AQ, QK_AK, QK_BQ, QK_BKS, QK_BKW = 0, 4, 8, 12, 13
PLAIN_COLS = A_WIDTH + 2 * B_KV_WIDTH
VV_AV, VV_BVS, VV_BVW = 0, 4, 5
MG_COLS = N_BRANCH * D_MODEL
B_PAIRS = B_WIDTH // LANES
B_HEAD_ORDER = tuple(h for p in range(B_PAIRS) for h in (p, B_GROUP + p))

TILE = 256
PEER_TB = 128
F32 = jnp.float32
BF16 = jnp.bfloat16
I32 = jnp.int32


def _const_spec(shape):
    n = len(shape)
    return pl.BlockSpec(shape, lambda *_: (0,) * n, pipeline_mode=pl.Buffered(1))


def _params(n_axes):
    return pltpu.CompilerParams(dimension_semantics=("arbitrary",) * n_axes,
                                vmem_limit_bytes=VMEM_LIMIT_BYTES)


def _dot(a, b):
    return jnp.dot(a, b, preferred_element_type=F32)


def _dot_nt(a, b):
    return lax.dot_general(a, b, (((1,), (1,)), ((), ())), preferred_element_type=F32)


def _split(a_f32):
    hi = a_f32.astype(BF16)
    lo = (a_f32 - hi.astype(F32)).astype(BF16)
    return hi, lo


def _split_dot(a_f32, b_bf16):
    hi, lo = _split(a_f32)
    return _dot(hi, b_bf16) + _dot(lo, b_bf16)


def _gelu_tanh(x):
    return 0.5 * x * (1.0 + jnp.tanh(math.sqrt(2.0 / math.pi) * (x + 0.044715 * (x * x * x))))


def _group_rmsnorm(y, bd, gain):
    ms = _split_dot(y * y, bd)
    return y * lax.rsqrt(ms + NORM_EPS) * gain


def _inproj_kernel(x_ref, g1_ref, w_rope_ref, w_plain_ref, w_small_ref, w_mg_ref, hg_ref,
                   cos_ref, sin_ref, bd_ref, bm_ref,
                   qk_ref, vvt_ref, kcin_ref, vcin_ref, gate_ref, mg_ref, kmean_ref):
    x = x_ref[...]
    h = x * lax.rsqrt(jnp.mean(x * x, axis=-1, keepdims=True) + NORM_EPS) * g1_ref[...]
    hb = h.astype(BF16)

    y_all = _dot(hb, w_rope_ref[...])
    vvt_ref[0] = _dot_nt(w_plain_ref[...], hb).astype(BF16)
    small = _dot(hb, w_small_ref[...])
    kcin_ref[...] = small[:, 0:LANES].astype(BF16)
    vcin_ref[...] = small[:, LANES:2 * LANES].astype(BF16)
    gate_ref[...] = jax.nn.sigmoid(small[:, 2 * LANES:3 * LANES])
    mg_ref[...] = jax.nn.sigmoid(_dot(hb, w_mg_ref[...]) + bm_ref[...]).astype(BF16)

    cos = cos_ref[...]
    sin = sin_ref[...]
    bd = bd_ref[...]
    lane = lax.broadcasted_iota(I32, (TILE, LANES), 1)
    first = (lane % HEAD_DIM) < ROT_HALF
    wide = 2 * LANES
    for t2 in range(ROPE_COLS // wide):
        y2 = y_all[:, t2 * wide:(t2 + 1) * wide]
        ms2 = _split_dot(y2 * y2, bd)
        for half in range(2):
            t = 2 * t2 + half
            sl = slice(t * LANES, (t + 1) * LANES)
            hs = slice(half * LANES, (half + 1) * LANES)
            yn = y2[:, hs] * lax.rsqrt(ms2[:, hs] + NORM_EPS) * hg_ref[:, sl]
            up = pltpu.roll(yn, LANES - ROT_HALF, axis=1)
            dn = pltpu.roll(yn, ROT_HALF, axis=1)
            out = yn * cos + jnp.where(first, -up, dn) * sin
            qk_ref[:, sl] = out.astype(BF16)
            if QK_AK <= t < QK_AK + A_WIDTH // LANES:
                c = t - QK_AK
                kmean_ref[0, :, c * LANES:(c + 1) * LANES] = jnp.mean(out, axis=0, keepdims=True)


def _inproj(x2d, g1, w_rope, w_plain, w_small, w_mg, hg, cos_t, sin_t, bd2, bm, seq):
    tokens = x2d.shape[0]
    nt = tokens // TILE
    tiles_per_seq = seq // TILE
    row = lambda i: (i, 0)
    pos = lambda i: (i % tiles_per_seq, 0)
    return pl.pallas_call(
        _inproj_kernel,
        grid=(nt,),
        in_specs=[
            pl.BlockSpec((TILE, D_MODEL), row),
            _const_spec((1, D_MODEL)),
            _const_spec((D_MODEL, ROPE_COLS)),
            _const_spec((PLAIN_COLS, D_MODEL)),
            _const_spec((D_MODEL, 3 * LANES)),
            _const_spec((D_MODEL, MG_COLS)),
            _const_spec((1, ROPE_COLS)),
            pl.BlockSpec((TILE, LANES), pos),
            pl.BlockSpec((TILE, LANES), pos),
            _const_spec((2 * LANES, 2 * LANES)),
            _const_spec((1, MG_COLS)),
        ],
        out_specs=[
            pl.BlockSpec((TILE, ROPE_COLS), row),
            pl.BlockSpec((1, PLAIN_COLS, TILE), lambda i: (i // tiles_per_seq, 0, i % tiles_per_seq)),
            pl.BlockSpec((TILE, LANES), row),
            pl.BlockSpec((TILE, LANES), row),
            pl.BlockSpec((TILE, LANES), row),
            pl.BlockSpec((TILE, MG_COLS), row),
            pl.BlockSpec((1, 1, A_WIDTH), lambda i: (i, 0, 0)),
        ],
        out_shape=[
            jax.ShapeDtypeStruct((tokens, ROPE_COLS), BF16),
            jax.ShapeDtypeStruct((tokens // seq, PLAIN_COLS, seq), BF16),
            jax.ShapeDtypeStruct((tokens, LANES), BF16),
            jax.ShapeDtypeStruct((tokens, LANES), BF16),
            jax.ShapeDtypeStruct((tokens, LANES), F32),
            jax.ShapeDtypeStruct((tokens, MG_COLS), BF16),
            jax.ShapeDtypeStruct((nt, 1, A_WIDTH), F32),
        ],
        compiler_params=_params(1),
        name="inproj",
    )(x2d, g1, w_rope, w_plain, w_small, w_mg, hg, cos_t, sin_t, bd2, bm)


LOG2E = math.log2(math.e)


def _scaled_halves(q_pair):
    lane = lax.broadcasted_iota(I32, (TILE, LANES), 1)
    q = q_pair.astype(F32)
    c = HEAD_DIM ** -0.5 * LOG2E
    plain = [jnp.where((lane // HEAD_DIM) == h, q, 0.0).astype(BF16) for h in range(2)]
    scaled = [jnp.where((lane // HEAD_DIM) == h, q * c, 0.0).astype(BF16) for h in range(2)]
    return plain, scaled


def _col_max(s_t):
    part = jnp.max(s_t.reshape(-1, SUBLANES, s_t.shape[-1]), axis=0)
    return jnp.max(part, axis=0, keepdims=True)


def _values_and_sum(v_t, p_t):
    ones = jnp.ones((2 * SUBLANES, v_t.shape[1]), BF16)
    res = _dot(jnp.concatenate([v_t, ones], axis=0), p_t.astype(BF16))
    return res[0:HEAD_DIM], res[HEAD_DIM:HEAD_DIM + SUBLANES]


def _flash_step(s_t, shift_bias, v_t, m, l8, acc_ref, rows):
    m_new = jnp.maximum(m, _col_max(s_t) + shift_bias)
    p_t = jnp.exp2(s_t - (m_new - shift_bias))
    alpha = jnp.exp2(m - m_new)
    pv, psum = _values_and_sum(v_t, p_t)
    l8 = alpha * l8 + psum
    acc_ref[rows, :] = alpha * acc_ref[rows, :] + pv
    return m_new, l8


def _flash_finish(l8s, acc_ref):
    inv = [1.0 / l8[0:1] for l8 in l8s]
    out_t = jnp.concatenate([acc_ref[0:HEAD_DIM, :] * inv[0], acc_ref[HEAD_DIM:, :] * inv[1]], axis=0)
    return jnp.transpose(out_t)


def _moba_kernel(q_ref, k_ref, vt_ref, kmh_ref, kml_ref, o_ref, acc_ref, *, topk, gate_rows):
    i = pl.program_id(1)
    pairs = A_WIDTH // LANES
    heads = [(p, h) for p in range(pairs) for h in range(2)]
    key_i = lax.broadcasted_iota(I32, (TILE, TILE), 0)
    qry_i = lax.broadcasted_iota(I32, (TILE, TILE), 1)
    causal = key_i <= qry_i
    blk = lax.broadcasted_iota(I32, (gate_rows, TILE), 0)
    blk_f = blk.astype(F32)
    own = pl.multiple_of(i * TILE, TILE)
    lanes_of = lambda p: slice(p * LANES, (p + 1) * LANES)
    rows_of = lambda p, h: slice((2 * p + h) * HEAD_DIM, (2 * p + h + 1) * HEAD_DIM)
    zero = jnp.zeros((1, TILE), F32)
    m0 = jnp.full((1, TILE), NEG_INF, F32)
    l0 = jnp.zeros((SUBLANES, TILE), F32)

    plain, scaled = [], []
    for p in range(pairs):
        plain_p, scaled_p = _scaled_halves(q_ref[0, :, lanes_of(p)])
        plain.extend(plain_p)
        scaled.extend(scaled_p)
    gates = [(_dot_nt(kmh_ref[0, :, lanes_of(p)], plain[c])
              + _dot_nt(kml_ref[0, :, lanes_of(p)], plain[c]))[0:gate_rows]
             for c, (p, h) in enumerate(heads)]
    own_scores = [_dot_nt(k_ref[0, pl.ds(own, TILE), lanes_of(p)], scaled[c])
                  for c, (p, h) in enumerate(heads)]

    biases, state = [], []
    for c, (p, h) in enumerate(heads):
        gate = jnp.where(blk < i, gates[c], NEG_INF)
        bias = jnp.full((gate_rows, TILE), NEG_INF, F32)
        for _ in range(topk):
            best = jnp.max(gate, axis=0, keepdims=True)
            idx = jnp.min(jnp.where(gate == best, blk_f, float(gate_rows)), axis=0, keepdims=True)
            pick = blk_f == idx
            bias = jnp.where(pick, jnp.where(best > 0.5 * NEG_INF, 0.0, bias), bias)
            gate = jnp.where(pick, NEG_INF, gate)
        biases.append(bias)
        acc_ref[rows_of(p, h), :] = jnp.zeros((HEAD_DIM, TILE), F32)
        state.extend(_flash_step(jnp.where(causal, own_scores[c], NEG_INF), zero,
                                 vt_ref[0, rows_of(p, h), pl.ds(own, TILE)], m0, l0,
                                 acc_ref, rows_of(p, h)))

    def past_block(n, carry):
        start = pl.multiple_of(n * TILE, TILE)
        out = []
        scores = [_dot_nt(k_ref[0, pl.ds(start, TILE), lanes_of(p)], scaled[c])
                  for c, (p, h) in enumerate(heads)]
        for c, (p, h) in enumerate(heads):
            b = jnp.max(jnp.where(blk == n, biases[c], NEG_INF), axis=0, keepdims=True)
            out.extend(_flash_step(scores[c], b, vt_ref[0, rows_of(p, h), pl.ds(start, TILE)],
                                   carry[2 * c], carry[2 * c + 1], acc_ref, rows_of(p, h)))
        return tuple(out)

    state = lax.fori_loop(0, i, past_block, tuple(state))
    for p in range(pairs):
        l8s = (state[4 * p + 1], state[4 * p + 3])
        o_ref[0, :, lanes_of(p)] = _flash_finish(l8s, acc_ref.at[lanes_of(p)]).astype(BF16)


def _moba(qk, vvt, kmean_hi, kmean_lo, batch, seq):
    n_blk = seq // A_BLOCK
    gate_rows = -(-n_blk // SUBLANES) * SUBLANES
    kernel = functools.partial(_moba_kernel, topk=min(A_TOPK, n_blk), gate_rows=gate_rows)
    return pl.pallas_call(
        kernel,
        grid=(batch, n_blk),
        in_specs=[
            pl.BlockSpec((1, TILE, A_WIDTH), lambda b, i: (b, i, QK_AQ * LANES // A_WIDTH)),
            pl.BlockSpec((1, seq, A_WIDTH), lambda b, i: (b, 0, QK_AK * LANES // A_WIDTH)),
            pl.BlockSpec((1, A_WIDTH, seq), lambda b, i: (b, VV_AV * LANES // A_WIDTH, 0)),
            pl.BlockSpec((1, LANES, A_WIDTH), lambda b, i: (b, 0, 0)),
            pl.BlockSpec((1, LANES, A_WIDTH), lambda b, i: (b, 0, 0)),
        ],
        out_specs=pl.BlockSpec((1, TILE, A_WIDTH), lambda b, i: (b, i, 0)),
        out_shape=jax.ShapeDtypeStruct((batch, seq, A_WIDTH), BF16),
        scratch_shapes=[pltpu.VMEM((A_WIDTH, TILE), F32)],
        compiler_params=_params(2),
        name="moba",
    )(qk, qk, vvt, kmean_hi, kmean_lo)


def _compress_kernel(kin_ref, vin_ref, w1_ref, pos_ref, b1_ref, w2_ref, b2_ref, bd_ref, g_ref,
                     kc_ref, vc_ref):
    n_rows = kin_ref.shape[1]
    for c, (in_ref, out_ref) in enumerate(((kin_ref, kc_ref), (vin_ref, vc_ref))):
        r = in_ref[0]
        first = _dot(r, w1_ref[c, 0])
        second = _dot(r, w1_ref[c, 1])
        const = (_dot(pos_ref[c, 0], w1_ref[c, 0]) + _dot(pos_ref[c, 1], w1_ref[c, 1]))[0:1]
        hid = first + pltpu.roll(second, n_rows - 1, axis=0) + const + b1_ref[c]
        out = _dot(_gelu_tanh(hid).astype(BF16), w2_ref[c]) + b2_ref[c]
        if c == 0:
            out = _group_rmsnorm(out, bd_ref[...], g_ref[...])
        out_ref[0] = out.astype(BF16)


def _compress(kin, vin, w1, pos, b1, w2, b2, bd, gain, batch):
    n_rows = kin.shape[1]
    blk = lambda b: (b, 0, 0)
    hid2 = B_KV_HEADS * CMP_HIDDEN
    return pl.pallas_call(
        _compress_kernel,
        grid=(batch,),
        in_specs=[
            pl.BlockSpec((1, n_rows, CMP_STRIDE * LANES), blk),
            pl.BlockSpec((1, n_rows, CMP_STRIDE * LANES), blk),
            _const_spec((2, 2, CMP_STRIDE * LANES, hid2)),
            _const_spec((2, 2, SUBLANES, CMP_STRIDE * LANES)),
            _const_spec((2, 1, hid2)),
            _const_spec((2, hid2, LANES)),
            _const_spec((2, 1, LANES)),
            _const_spec((LANES, LANES)),
            _const_spec((1, LANES)),
        ],
        out_specs=[pl.BlockSpec((1, n_rows, LANES), blk), pl.BlockSpec((1, n_rows, LANES), blk)],
        out_shape=[jax.ShapeDtypeStruct((batch, n_rows, LANES), BF16)] * 2,
        compiler_params=_params(1),
        name="nsa_compress",
    )(kin, vin, w1, pos, b1, w2, b2, bd, gain)


def _nsa_cmp_kernel(q_ref, kc_ref, vc_ref, ov_ref, ocmp_ref, selb_ref, *, n_cmp, topn):
    i = pl.program_id(1)
    scale = HEAD_DIM ** -0.5
    n_pad = kc_ref.shape[1]
    lane = lax.broadcasted_iota(I32, (TILE, LANES), 1)
    pos = i * TILE + lax.broadcasted_iota(I32, (TILE, 1), 0)
    ncol = lax.broadcasted_iota(I32, (TILE, n_pad), 1)
    visible = (ncol * CMP_STRIDE + (CMP_LEN - 1) <= pos) & (ncol < n_cmp)
    kc = kc_ref[0]
    vc = vc_ref[0]
    ov_t = ov_ref[...]
    heads = [(p, h) for p in range(B_PAIRS) for h in range(2)]
    qs = []
    for p in range(B_PAIRS):
        q_pair = q_ref[0, :, p * LANES:(p + 1) * LANES]
        qs.extend(jnp.where((lane // HEAD_DIM) == h, q_pair, jnp.zeros_like(q_pair)) for h in range(2))
    scores = [_dot_nt(q, kc) for q in qs]
    probs = []
    for s in scores:
        s = jnp.where(visible, s * scale, NEG_INF)
        e = jnp.where(visible, jnp.exp(s - jnp.max(s, axis=-1, keepdims=True)), 0.0)
        tot = jnp.sum(e, axis=-1, keepdims=True)
        probs.append(jnp.where(tot > 0.0, e / tot, 0.0))
    outs = [_dot(prob.astype(BF16), vc) for prob in probs]
    parts = []
    for prob in probs:
        hi, lo = _split(prob)
        parts.append(_dot_nt(ov_t, hi) + _dot_nt(ov_t, lo))
    for p in range(B_PAIRS):
        o_pair = jnp.where(lane < HEAD_DIM, outs[2 * p], outs[2 * p + 1])
        ocmp_ref[0, :, p * LANES:(p + 1) * LANES] = o_pair.astype(BF16)

    blk = lax.broadcasted_iota(I32, (HEAD_DIM, TILE), 0)
    own = (i * TILE + lax.broadcasted_iota(I32, (1, TILE), 1)) // SLC_BLOCK
    forced = (blk == 0) | (blk == own) | (blk == own - 1)
    for g in range(B_KV_HEADS):
        imp = functools.reduce(lambda a, b: a + b, [parts[c] for c, (p, h) in enumerate(heads) if h == g])
        imp = jnp.where(forced, BIG, imp)
        imp = jnp.where(blk <= own, imp, NEG_INF)
        groups = [imp[r:r + SUBLANES] for r in range(0, HEAD_DIM, SUBLANES)]
        ranks = [jnp.zeros((SUBLANES, TILE), F32) for _ in groups]
        sub = lax.broadcasted_iota(I32, (SUBLANES, TILE), 0)
        for k in range(HEAD_DIM):
            other = imp[k:k + 1]
            for r, grp in enumerate(groups):
                ge = jnp.where(other >= grp, 1.0, 0.0)
                gt = jnp.where(other > grp, 1.0, 0.0)
                if r * SUBLANES > k:
                    ahead = ge
                elif r * SUBLANES + SUBLANES - 1 <= k:
                    ahead = gt
                else:
                    ahead = jnp.where(sub + r * SUBLANES > k, ge, gt)
                ranks[r] = ranks[r] + ahead
        rank = jnp.concatenate(ranks, axis=0)
        keep = jnp.where(rank < float(topn), jnp.where(imp > 0.5 * NEG_INF, 0.0, NEG_INF), NEG_INF)
        selb_ref[0, g * HEAD_DIM:(g + 1) * HEAD_DIM, :] = keep


def _nsa_cmp(qk, kc, vc, ov, batch, seq):
    n_pad = kc.shape[1]
    n_cmp = (seq - CMP_LEN) // CMP_STRIDE + 1
    topn = min(SLC_TOPN, seq // SLC_BLOCK)
    kernel = functools.partial(_nsa_cmp_kernel, n_cmp=n_cmp, topn=topn)
    return pl.pallas_call(
        kernel,
        grid=(batch, seq // TILE),
        in_specs=[
            pl.BlockSpec((1, TILE, B_WIDTH), lambda b, i: (b, i, QK_BQ * LANES // B_WIDTH)),
            pl.BlockSpec((1, n_pad, LANES), lambda b, i: (b, 0, 0)),
            pl.BlockSpec((1, n_pad, LANES), lambda b, i: (b, 0, 0)),
            _const_spec((HEAD_DIM, n_pad)),
        ],
        out_specs=[
            pl.BlockSpec((1, TILE, B_WIDTH), lambda b, i: (b, i, 0)),
            pl.BlockSpec((1, LANES, TILE), lambda b, i: (b, 0, i)),
        ],
        out_shape=[
            jax.ShapeDtypeStruct((batch, seq, B_WIDTH), BF16),
            jax.ShapeDtypeStruct((batch, LANES, seq), F32),
        ],
        compiler_params=_params(2),
        name="nsa_cmp_select",
    )(qk, kc, vc, ov)


def _nsa_attn_kernel(q_ref, ks_ref, vst_ref, kw_ref, vwt_ref, selb_ref, ocmp_ref, gate_ref, egt_ref,
                     o_ref, acc_s_ref, acc_w_ref):
    i = pl.program_id(1)
    blocks_per_tile = TILE // SLC_BLOCK
    heads = [(p, h) for p in range(B_PAIRS) for h in range(2)]
    key_i = lax.broadcasted_iota(I32, (TILE, TILE), 0)
    qry_i = lax.broadcasted_iota(I32, (TILE, TILE), 1)
    causal = key_i <= qry_i
    sub = lax.broadcasted_iota(I32, (SUBLANES, TILE), 0)
    own = pl.multiple_of(i * TILE, TILE)
    lanes_of = lambda p: slice(p * LANES, (p + 1) * LANES)
    rows_of = lambda p, h: slice((2 * p + h) * HEAD_DIM, (2 * p + h + 1) * HEAD_DIM)
    kv_rows = (slice(0, HEAD_DIM), slice(HEAD_DIM, LANES))
    zero = jnp.zeros((1, TILE), F32)
    m0 = jnp.full((1, TILE), NEG_INF, F32)
    l0 = jnp.zeros((SUBLANES, TILE), F32)
    scaled = []
    for p in range(B_PAIRS):
        scaled.extend(_scaled_halves(q_ref[0, :, lanes_of(p)])[1])

    def block_biases(h, n):
        first = h * HEAD_DIM + n * blocks_per_tile
        base = pl.multiple_of((first >> 3) << 3, SUBLANES)
        rows8 = selb_ref[0, pl.ds(base, SUBLANES), :]
        off = first - base
        return [jnp.max(jnp.where(sub == off + r, rows8, NEG_INF), axis=0, keepdims=True)
                for r in range(blocks_per_tile)]

    def slc_step(p, h, n, bias, s_t, m, l8):
        parts = [s_t[r * SLC_BLOCK:(r + 1) * SLC_BLOCK] for r in range(blocks_per_tile)]
        cand = [_col_max(parts[r]) + bias[r] for r in range(blocks_per_tile)]
        m_new = functools.reduce(jnp.maximum, cand, m)
        p_t = jnp.concatenate([jnp.exp2(parts[r] - (m_new - bias[r])) for r in range(blocks_per_tile)],
                              axis=0)
        alpha = jnp.exp2(m - m_new)
        start = pl.multiple_of(n * TILE, TILE)
        v_t = vst_ref[0, kv_rows[h], pl.ds(start, TILE)]
        pv, psum = _values_and_sum(v_t, p_t)
        acc_s_ref[rows_of(p, h), :] = alpha * acc_s_ref[rows_of(p, h), :] + pv
        return m_new, alpha * l8 + psum

    def scores_of(k_ref_, start):
        k = k_ref_[0, pl.ds(start, TILE), :]
        return [_dot_nt(k, scaled[c]) for c in range(len(heads))]

    state = []
    own_bias = [block_biases(h, i) for h in range(2)]
    own_scores = scores_of(ks_ref, own)
    for c, (p, h) in enumerate(heads):
        acc_s_ref[rows_of(p, h), :] = jnp.zeros((HEAD_DIM, TILE), F32)
        state.extend(slc_step(p, h, i, own_bias[h], jnp.where(causal, own_scores[c], NEG_INF), m0, l0))

    def past_tile(n, carry):
        scores = scores_of(ks_ref, pl.multiple_of(n * TILE, TILE))
        bias = [block_biases(h, n) for h in range(2)]
        out = []
        for c, (p, h) in enumerate(heads):
            out.extend(slc_step(p, h, n, bias[h], scores[c], carry[2 * c], carry[2 * c + 1]))
        return tuple(out)

    state = lax.fori_loop(0, i, past_tile, tuple(state))

    wstate = [(m0, l0)] * len(heads)
    for back, mask in ((0, causal), (1, None), (2, key_i > qry_i)):
        start = pl.multiple_of(jnp.maximum(i - back, 0) * TILE, TILE)
        scores = scores_of(kw_ref, start)
        exists = jnp.where(i >= back, 0.0, NEG_INF) + zero
        for c, (p, h) in enumerate(heads):
            if back == 0:
                acc_w_ref[rows_of(p, h), :] = jnp.zeros((HEAD_DIM, TILE), F32)
            s_t = scores[c] if mask is None else jnp.where(mask, scores[c], NEG_INF)
            wstate[c] = _flash_step(s_t, exists, vwt_ref[0, kv_rows[h], pl.ds(start, TILE)],
                                    wstate[c][0], wstate[c][1], acc_w_ref, rows_of(p, h))
    wstate = [l8 for (_, l8) in wstate]

    for p in range(B_PAIRS):
        o_slc = _flash_finish((state[4 * p + 1], state[4 * p + 3]), acc_s_ref.at[lanes_of(p)])
        o_win = _flash_finish((wstate[2 * p], wstate[2 * p + 1]), acc_w_ref.at[lanes_of(p)])
        g = _split_dot(gate_ref[0], egt_ref[p])
        out = (g[:, 0:LANES] * ocmp_ref[0, :, lanes_of(p)].astype(F32) + g[:, LANES:2 * LANES] * o_slc
               + g[:, 2 * LANES:3 * LANES] * o_win)
        o_ref[0, :, lanes_of(p)] = out.astype(BF16)


def _nsa_attn(qk, vvt, selb, ocmp, gate, eg, batch, seq):
    assert WINDOW == 2 * TILE
    k_spec = lambda t: pl.BlockSpec((1, seq, LANES), lambda b, i: (b, 0, t))
    vt_spec = lambda t: pl.BlockSpec((1, LANES, seq), lambda b, i: (b, t, 0))
    return pl.pallas_call(
        _nsa_attn_kernel,
        grid=(batch, seq // TILE),
        in_specs=[
            pl.BlockSpec((1, TILE, B_WIDTH), lambda b, i: (b, i, QK_BQ * LANES // B_WIDTH)),
            k_spec(QK_BKS),
            vt_spec(VV_BVS),
            k_spec(QK_BKW),
            vt_spec(VV_BVW),
            pl.BlockSpec((1, LANES, TILE), lambda b, i: (b, 0, i)),
            pl.BlockSpec((1, TILE, B_WIDTH), lambda b, i: (b, i, 0)),
            pl.BlockSpec((1, TILE, LANES), lambda b, i: (b, i, 0)),
            _const_spec((B_PAIRS, LANES, 3 * LANES)),
        ],
        out_specs=pl.BlockSpec((1, TILE, B_WIDTH), lambda b, i: (b, i, 0)),
        out_shape=jax.ShapeDtypeStruct((batch, seq, B_WIDTH), BF16),
        scratch_shapes=[pltpu.VMEM((B_WIDTH, TILE), F32), pltpu.VMEM((B_WIDTH, TILE), F32)],
        compiler_params=_params(2),
        name="nsa_select_window",
    )(qk, qk, vvt, qk, vvt, selb, ocmp, gate, eg)


def _top_rows(v, row_f, count, fill):
    vals, rows = [], []
    for _ in range(count):
        best = jnp.max(v, axis=0, keepdims=True)
        idx = jnp.min(jnp.where(v == best, row_f, fill), axis=0, keepdims=True)
        vals.append(best)
        rows.append(idx)
        v = jnp.where(row_f == idx, -jnp.inf, v)
    return vals, rows


def _store_token_tiles(ref, val):
    for c in range(D_MODEL // LANES):
        ref[pl.ds(c, TILE, stride=D_MODEL // LANES), :] = val[:, c * LANES:(c + 1) * LANES]


def _merge_peer_kernel(x_ref, ya_ref, yb_ref, mg_ref, wa_ref, wb_ref, wo_ref, g2_ref, wq_ref,
                       k1h_ref, k1l_ref, k2h_ref, k2l_ref,
                       x1_ref, hn_ref, idx_ref, gt_ref, q_scr):
    ua = _dot(ya_ref[...], wa_ref[...])
    ub = _dot(yb_ref[...], wb_ref[...])
    merged = (mg_ref[:, 0:D_MODEL].astype(F32) * ua + mg_ref[:, D_MODEL:2 * D_MODEL].astype(F32) * ub)
    x1 = x_ref[...] + _dot(merged.astype(BF16), wo_ref[...])
    _store_token_tiles(x1_ref, x1)
    hn = x1 * lax.rsqrt(jnp.mean(x1 * x1, axis=-1, keepdims=True) + NORM_EPS) * g2_ref[...]
    _store_token_tiles(hn_ref, hn)
    q_scr[...] = _dot(hn.astype(BF16), wq_ref[...])

    key_row = lax.broadcasted_iota(I32, (PEER_NKEYS, TILE), 0).astype(F32)
    n_cand = 7 * SUBLANES
    cand_row = lax.broadcasted_iota(I32, (n_cand, TILE), 0).astype(F32)

    def head_scores(h):
        base = pl.multiple_of(h * PEER_QDIM, PEER_QDIM)
        q1h, q1l = _split(q_scr[:, pl.ds(base, PEER_HALF)])
        q2h, q2l = _split(q_scr[:, pl.ds(base + PEER_HALF, PEER_HALF)])
        k1h, k2h = k1h_ref[...], k2h_ref[...]
        s1 = _dot_nt(k1h, q1h) + _dot_nt(k1h, q1l) + _dot_nt(k1l_ref[...], q1h)
        s2 = _dot_nt(k2h, q2h) + _dot_nt(k2h, q2l) + _dot_nt(k2l_ref[...], q2h)
        return s1, s2

    def head_select(h, s1, s2):
        v1, i1 = _top_rows(s1, key_row, PEER_TOPK, float(PEER_NKEYS))
        v2, i2 = _top_rows(s2, key_row, PEER_TOPK, float(PEER_NKEYS))
        v1m, i1m = jnp.concatenate(v1, axis=0), jnp.concatenate(i1, axis=0)
        v2m, i2m = jnp.concatenate(v2, axis=0), jnp.concatenate(i2, axis=0)
        sub8 = lax.broadcasted_iota(I32, (SUBLANES, TILE), 0)
        lo_v, lo_i = v1m[0:SUBLANES], i1m[0:SUBLANES] * float(PEER_NKEYS)
        mid = (sub8 >= 2)

        def shifted(x, rows_up):
            return pltpu.roll(x, rows_up, axis=0)

        cand = [v1[0] + v2m,
                v1[1] + v2m[0:SUBLANES],
                v1m[SUBLANES:] + v2[0],
                jnp.where(mid, lo_v + v2[0], -jnp.inf),
                jnp.where(mid, lo_v + v2[1], -jnp.inf),
                jnp.where(sub8 < 2, -jnp.inf,
                          jnp.where(sub8 < 5, lo_v + v2[2],
                                    jnp.where(sub8 < 7, shifted(lo_v, 3) + v2[3],
                                              shifted(lo_v, 5) + v2[4])))]
        cidx = [i1[0] * float(PEER_NKEYS) + i2m,
                i1[1] * float(PEER_NKEYS) + i2m[0:SUBLANES],
                i1m[SUBLANES:] * float(PEER_NKEYS) + i2[0],
                lo_i + i2[0],
                lo_i + i2[1],
                jnp.where(sub8 < 5, lo_i + i2[2],
                          jnp.where(sub8 < 7, shifted(lo_i, 3) + i2[3], shifted(lo_i, 5) + i2[4]))]
        cand = jnp.concatenate(cand, axis=0)
        cidx = jnp.concatenate(cidx, axis=0)
        vals, experts = [], []
        v = cand
        for _ in range(PEER_TOPK):
            best = jnp.max(v, axis=0, keepdims=True)
            at = jnp.min(jnp.where(v == best, cand_row, float(n_cand)), axis=0, keepdims=True)
            hit = cand_row == at
            experts.append(jnp.max(jnp.where(hit, cidx, -1.0), axis=0, keepdims=True))
            vals.append(best)
            v = jnp.where(hit, -jnp.inf, v)
        vals = jnp.concatenate(vals, axis=0)
        e = jnp.exp(vals - vals[0:1])
        out_row = pl.multiple_of(h * PEER_TOPK, PEER_TOPK)
        gt_ref[pl.ds(out_row, PEER_TOPK), :] = e / jnp.sum(e, axis=0, keepdims=True)
        first_row = jnp.concatenate(experts, axis=0) * float(HALF_ROWS)
        idx_ref[pl.ds(out_row, PEER_TOPK), :] = first_row.astype(I32)

    def head_group(i, carry):
        hs = [i * PEER_HEAD_UNROLL + k for k in range(PEER_HEAD_UNROLL)]
        scores = [head_scores(h) for h in hs]
        for h, (s1, s2) in zip(hs, scores):
            head_select(h, s1, s2)
        return carry

    lax.fori_loop(0, PEER_HEADS // PEER_HEAD_UNROLL, head_group, 0)


def _merge_peer(x2d, ya, yb, mg, wa, wb, wo, g2, wq, k1h, k1l, k2h, k2l):
    tokens = x2d.shape[0]
    row = lambda i: (i, 0)
    return pl.pallas_call(
        _merge_peer_kernel,
        grid=(tokens // TILE,),
        in_specs=[
            pl.BlockSpec((TILE, D_MODEL), row),
            pl.BlockSpec((TILE, A_WIDTH), row),
            pl.BlockSpec((TILE, B_WIDTH), row),
            pl.BlockSpec((TILE, MG_COLS), row),
            _const_spec((A_WIDTH, D_MODEL)),
            _const_spec((B_WIDTH, D_MODEL)),
            _const_spec((D_MODEL, D_MODEL)),
            _const_spec((1, D_MODEL)),
            _const_spec((D_MODEL, PEER_HEADS * PEER_QDIM)),
            _const_spec((PEER_NKEYS, PEER_HALF)),
            _const_spec((PEER_NKEYS, PEER_HALF)),
            _const_spec((PEER_NKEYS, PEER_HALF)),
            _const_spec((PEER_NKEYS, PEER_HALF)),
        ],
        out_specs=[
            pl.BlockSpec((TILE * SUBLANES, LANES), row),
            pl.BlockSpec((TILE * SUBLANES, LANES), row),
            pl.BlockSpec((PEER_SLOTS, TILE), lambda i: (0, i)),
            pl.BlockSpec((PEER_SLOTS, TILE), lambda i: (0, i)),
        ],
        out_shape=[
            jax.ShapeDtypeStruct((tokens * SUBLANES, LANES), F32),
            jax.ShapeDtypeStruct((tokens * SUBLANES, LANES), F32),
            jax.ShapeDtypeStruct((PEER_SLOTS, tokens), I32),
            jax.ShapeDtypeStruct((PEER_SLOTS, tokens), F32),
        ],
        scratch_shapes=[pltpu.VMEM((TILE, PEER_HEADS * PEER_QDIM), F32)],
        compiler_params=_params(1),
        name="merge_peer_topk",
    )(x2d, ya, yb, mg, wa, wb, wo, g2, wq, k1h, k1l, k2h, k2l)


HALF_ROWS = SUBLANES // 2
HI_MASK = -65536
ROW_MASK = 65535
PEER_UNROLL = 16
PEER_OUT_UNROLL = 2


def _unpack(words):
    hi = pltpu.bitcast(words & HI_MASK, F32)
    lo = pltpu.bitcast(words << 16, F32)
    return hi, lo


def _expert_row(tab_ref, row):
    return tab_ref[pl.ds(pl.multiple_of(row, HALF_ROWS), HALF_ROWS), :]


def _slot_reader(head_refs, t):
    return lambda j: head_refs[j // PEER_TOPK][j % PEER_TOPK, t]


def _head_smem_specs():
    return [pl.BlockSpec((PEER_TOPK, PEER_TB), lambda i, h=h: (h, i), memory_space=pltpu.SMEM)
            for h in range(PEER_HEADS)]


def _fold_rows(v, shift):
    return v + pltpu.roll(v, shift, axis=0)


def _peer_u_kernel(*refs):
    row_refs = refs[:PEER_HEADS]
    rowv_ref, hn_ref, gt_ref, tab_ref, route_ref = refs[PEER_HEADS:]
    lane = lax.broadcasted_iota(I32, (SUBLANES, LANES), 1)
    sub = lax.broadcasted_iota(I32, (SUBLANES, LANES), 0)
    keep_hi_pair = (sub % 4) >= 2
    odd = (sub % 2) == 1
    groups = PEER_SLOTS // SUBLANES

    def one_token(t, accs):
        x = hn_ref[t]
        xa = jnp.concatenate([x[0:HALF_ROWS], x[0:HALF_ROWS]], axis=0)
        xb = jnp.concatenate([x[HALF_ROWS:], x[HALF_ROWS:]], axis=0)

        slot_row = _slot_reader(row_refs, t)

        def pair(ja, jb):
            words = jnp.concatenate([_expert_row(tab_ref, slot_row(ja)),
                                     _expert_row(tab_ref, slot_row(jb))], axis=0)
            hi, lo = _unpack(words)
            return hi * xa + lo * xb

        new = list(accs)
        for g in [2 * h + half for half in range(2) for h in range(PEER_HEADS)]:
            j = g * SUBLANES
            quads = []
            for (a, b, c, d) in ((j + 3, j + 7, j + 1, j + 5), (j + 2, j + 6, j, j + 4)):
                w_ab = _fold_rows(pair(a, b), 2)
                w_cd = _fold_rows(pair(c, d), 2)
                z = jnp.where(keep_hi_pair, w_ab, pltpu.roll(w_cd, 6, axis=0))
                quads.append(_fold_rows(z, 1))
            folded = jnp.where(odd, quads[0], pltpu.roll(quads[1], 7, axis=0))
            total = jnp.sum(folded, axis=1, keepdims=True)
            new[g] = jnp.where(lane == t, total, accs[g])
        return tuple(new)

    def step(i, accs):
        for k in range(PEER_UNROLL):
            accs = one_token(i * PEER_UNROLL + k, accs)
        return accs

    init = tuple(jnp.zeros((SUBLANES, LANES), F32) for _ in range(groups))
    accs = lax.fori_loop(0, PEER_TB // PEER_UNROLL, step, init)
    a = jnp.concatenate(accs, axis=0)
    w = (_gelu_tanh(a) * gt_ref[...]).astype(BF16).astype(F32)
    route_ref[...] = (pltpu.bitcast(w, I32) & HI_MASK) | rowv_ref[...]


def _peer_u(rows, hn3, gt, tab):
    tokens = hn3.shape[0]
    blk = lambda i: (0, i)
    return pl.pallas_call(
        _peer_u_kernel,
        grid=(tokens // PEER_TB,),
        in_specs=_head_smem_specs() + [
            pl.BlockSpec((PEER_SLOTS, PEER_TB), blk),
            pl.BlockSpec((PEER_TB, SUBLANES, LANES), lambda i: (i, 0, 0)),
            pl.BlockSpec((PEER_SLOTS, PEER_TB), blk),
            _const_spec((PEER_N * HALF_ROWS, LANES)),
        ],
        out_specs=pl.BlockSpec((PEER_SLOTS, PEER_TB), blk),
        out_shape=jax.ShapeDtypeStruct((PEER_SLOTS, tokens), I32),
        compiler_params=_params(1),
        name="peer_expert_in",
    )(*([rows] * PEER_HEADS), rows, hn3, gt, tab)


def _peer_v_kernel(*refs):
    route_refs = refs[:PEER_HEADS]
    x1_ref, tab_ref, o_ref = refs[PEER_HEADS:]

    upper = lax.broadcasted_iota(I32, (SUBLANES, LANES), 0) >= HALF_ROWS

    def one_token(t):
        acc_hi = jnp.zeros((SUBLANES, LANES), F32)
        acc_lo = jnp.zeros((SUBLANES, LANES), F32)
        slot_word = _slot_reader(route_refs, t)
        for j in [h * PEER_TOPK + k for k in range(0, PEER_TOPK, 2) for h in range(PEER_HEADS)]:
            wa, wb = slot_word(j), slot_word(j + 1)
            words = jnp.concatenate([_expert_row(tab_ref, wa & ROW_MASK),
                                     _expert_row(tab_ref, wb & ROW_MASK)], axis=0)
            hi, lo = _unpack(words)
            both = jnp.where(upper, jnp.full((SUBLANES, LANES), wb, I32),
                             jnp.full((SUBLANES, LANES), wa, I32))
            w = pltpu.bitcast(both & HI_MASK, F32)
            acc_hi = acc_hi + w * hi
            acc_lo = acc_lo + w * lo
        out = jnp.concatenate([acc_hi[0:HALF_ROWS] + acc_hi[HALF_ROWS:],
                               acc_lo[0:HALF_ROWS] + acc_lo[HALF_ROWS:]], axis=0)
        o_ref[t] = x1_ref[t] + out

    def step(i, carry):
        for k in range(PEER_OUT_UNROLL):
            one_token(i * PEER_OUT_UNROLL + k)
        return carry

    lax.fori_loop(0, PEER_TB // PEER_OUT_UNROLL, step, 0)


def _peer_v(route, x13, tab):
    tokens = x13.shape[0]
    return pl.pallas_call(
        _peer_v_kernel,
        grid=(tokens // PEER_TB,),
        in_specs=_head_smem_specs() + [
            pl.BlockSpec((PEER_TB, SUBLANES, LANES), lambda i: (i, 0, 0)),
            _const_spec((PEER_N * HALF_ROWS, LANES)),
        ],
        out_specs=pl.BlockSpec((PEER_TB, SUBLANES, LANES), lambda i: (i, 0, 0)),
        out_shape=jax.ShapeDtypeStruct((tokens, SUBLANES, LANES), F32),
        compiler_params=_params(1),
        name="peer_expert_out",
    )(*([route] * PEER_HEADS), x13, tab)


def _pack_table(t):
    bits = lax.bitcast_convert_type(t.astype(BF16), jnp.uint16).astype(jnp.uint32)
    half = D_MODEL // 2
    words = (bits[:, :half] << 16) | bits[:, half:]
    return lax.bitcast_convert_type(words, I32).reshape(t.shape[0] * HALF_ROWS, LANES)


def _rope_tables(seq):
    inv = ROPE_THETA ** (-jnp.arange(0, ROT_DIM, 2, dtype=F32) / ROT_DIM)
    ang = jnp.arange(seq, dtype=F32)[:, None] * inv[None, :]
    d = np.arange(LANES) % HEAD_DIM
    cos = jnp.where(d[None, :] < ROT_DIM, jnp.cos(ang)[:, d % ROT_HALF], 1.0)
    sin = jnp.where(d[None, :] < ROT_DIM, jnp.sin(ang)[:, d % ROT_HALF], 0.0)
    return cos.astype(F32), sin.astype(F32)


def _compress_weights(w1, pos):
    out_w, out_p = [], []
    for part in range(2):
        wpart = w1[part * CMP_STRIDE * HEAD_DIM:(part + 1) * CMP_STRIDE * HEAD_DIM]
        wpart = wpart.reshape(CMP_STRIDE, HEAD_DIM, CMP_HIDDEN)
        full = jnp.einsum("ldj,gh->lgdhj", wpart, jnp.eye(B_KV_HEADS, dtype=F32))
        out_w.append(full.reshape(CMP_STRIDE * LANES, B_KV_HEADS * CMP_HIDDEN))
        ppart = pos[part * CMP_STRIDE:(part + 1) * CMP_STRIDE]
        prow = jnp.tile(ppart[:, None, :], (1, B_KV_HEADS, 1)).reshape(1, CMP_STRIDE * LANES)
        out_p.append(jnp.tile(prow, (SUBLANES, 1)))
    return jnp.stack(out_w).astype(BF16), jnp.stack(out_p).astype(BF16)


def kernel(x, norm1_g, w_in, b_merge, a_q_g, a_k_g, b_q_g, b_kc_g, b_ks_g, b_kw_g, cmp_pos_k,
           cmp_k_w1, cmp_k_b1, cmp_k_w2, cmp_k_b2, cmp_pos_v, cmp_v_w1, cmp_v_b1, cmp_v_w2,
           cmp_v_b2, w_up_a, w_up_b, w_out, norm2_g, peer_wq, peer_k1, peer_k2, peer_u, peer_v):
    batch, seq, _ = x.shape
    assert seq % TILE == 0 and seq // SLC_BLOCK <= HEAD_DIM
    tokens = batch * seq
    l = 0
    x2d = x.reshape(tokens, D_MODEL)

    w = w_in[l]
    sizes = (A_WIDTH, A_WIDTH, A_WIDTH, B_WIDTH) + (B_KV_WIDTH,) * 6 + (3 * B_HEADS, MG_COLS)
    offs = np.concatenate([[0], np.cumsum(sizes)])
    seg = lambda k: w[:, offs[k]:offs[k + 1]]
    aq, ak, av, bq, bkc, bvc, bks, bvs, bkw, bvw, bgate, mgate = (seg(k) for k in range(12))
    bq_perm = bq.reshape(D_MODEL, B_HEADS, HEAD_DIM)[:, np.array(B_HEAD_ORDER)].reshape(D_MODEL, B_WIDTH)
    w_rope = jnp.concatenate([aq, ak, bq_perm, bks, bkw], axis=1).astype(BF16)
    w_plain = jnp.concatenate([av, bvs, bvw], axis=1).T.astype(BF16)
    w_small = jnp.concatenate([bkc, bvc, jnp.pad(bgate, ((0, 0), (0, LANES - 3 * B_HEADS)))],
                              axis=1).astype(BF16)
    hg = jnp.concatenate([jnp.tile(a_q_g[l], A_HEADS), jnp.tile(a_k_g[l], A_HEADS),
                          jnp.tile(b_q_g[l], B_HEADS), jnp.tile(b_ks_g[l], B_KV_HEADS),
                          jnp.tile(b_kw_g[l], B_KV_HEADS)])[None, :]
    cos_t, sin_t = _rope_tables(seq)
    bd = jnp.asarray(np.kron(np.eye(LANES // HEAD_DIM), np.full((HEAD_DIM, HEAD_DIM), 1.0 / HEAD_DIM)),
                     BF16)

    qk, vvt, kcin, vcin, gate, mg, kmean = _inproj(
        x2d, norm1_g[l][None, :], w_rope, w_plain, w_small, mgate.astype(BF16), hg, cos_t, sin_t,
        jnp.kron(jnp.eye(2, dtype=BF16), bd), b_merge[l].reshape(1, MG_COLS), seq)
    qk = qk.reshape(batch, seq, ROPE_COLS)

    n_blk = seq // A_BLOCK
    kmean = jnp.pad(kmean.reshape(batch, n_blk, A_WIDTH), ((0, 0), (0, LANES - n_blk), (0, 0)))
    km_hi = kmean.astype(BF16)
    km_lo = (kmean - km_hi.astype(F32)).astype(BF16)
    ya = _moba(qk, vvt, km_hi, km_lo, batch, seq)

    n_rows = seq // CMP_STRIDE
    wk1, pk = _compress_weights(cmp_k_w1[l], cmp_pos_k[l])
    wv1, pv = _compress_weights(cmp_v_w1[l], cmp_pos_v[l])
    blockdiag = lambda m: jnp.kron(jnp.eye(B_KV_HEADS, dtype=F32), m)
    w2 = jnp.stack([blockdiag(cmp_k_w2[l]), blockdiag(cmp_v_w2[l])]).astype(BF16)
    b1 = jnp.stack([jnp.tile(cmp_k_b1[l], B_KV_HEADS), jnp.tile(cmp_v_b1[l], B_KV_HEADS)])[:, None, :]
    b2 = jnp.stack([jnp.tile(cmp_k_b2[l], B_KV_HEADS), jnp.tile(cmp_v_b2[l], B_KV_HEADS)])[:, None, :]
    kc, vc = _compress(kcin.reshape(batch, n_rows, CMP_STRIDE * LANES),
                       vcin.reshape(batch, n_rows, CMP_STRIDE * LANES),
                       jnp.stack([wk1, wv1]), jnp.stack([pk, pv]), b1, w2, b2, bd,
                       jnp.tile(b_kc_g[l], B_KV_HEADS)[None, :], batch)

    n_cmp = (seq - CMP_LEN) // CMP_STRIDE + 1
    n_slc = seq // SLC_BLOCK
    ci = np.arange(n_rows)[:, None]
    sj = np.arange(HEAD_DIM)[None, :]
    ov = ((ci * CMP_STRIDE < (sj + 1) * SLC_BLOCK) & (ci * CMP_STRIDE + CMP_LEN > sj * SLC_BLOCK)
          & (ci < n_cmp) & (sj < n_slc)).astype(np.float32)
    ov = jnp.asarray(ov.T, BF16)
    ocmp, selb = _nsa_cmp(qk, kc, vc, ov, batch, seq)

    eg = np.zeros((B_PAIRS, LANES, 3 * LANES), np.float32)
    for p in range(B_PAIRS):
        for half in range(2):
            head = B_HEAD_ORDER[2 * p + half]
            for c in range(3):
                eg[p, head * 3 + c, c * LANES + half * HEAD_DIM:c * LANES + (half + 1) * HEAD_DIM] = 1.0
    yb = _nsa_attn(qk, vvt, selb, ocmp, gate.reshape(batch, seq, LANES), jnp.asarray(eg, BF16),
                   batch, seq)

    wb_perm = w_up_b[l].reshape(B_HEADS, HEAD_DIM, D_MODEL)[np.array(B_HEAD_ORDER)].reshape(
        B_WIDTH, D_MODEL)
    k1h, k1l = _split(peer_k1[l])
    k2h, k2l = _split(peer_k2[l])
    x1, hn, rows, gt = _merge_peer(
        x2d, ya.reshape(tokens, A_WIDTH), yb.reshape(tokens, B_WIDTH), mg,
        w_up_a[l].astype(BF16), wb_perm.astype(BF16), w_out[l].astype(BF16),
        norm2_g[l][None, :], peer_wq[l].astype(BF16), k1h, k1l, k2h, k2l)

    route = _peer_u(rows, hn.reshape(tokens, SUBLANES, LANES), gt, _pack_table(peer_u[l]))
    out = _peer_v(route, x1.reshape(tokens, SUBLANES, LANES), _pack_table(peer_v[l]))
    return out.reshape(batch, seq, D_MODEL)
```

```python
import functools
import math

import jax
import jax.numpy as jnp
import numpy as np
from jax import lax
from jax.experimental import pallas as pl
from jax.experimental.pallas import tpu as pltpu

D_MODEL = 1024
HEAD_DIM = 64
ROT_DIM = HEAD_DIM // 4
ROT_HALF = ROT_DIM // 2
ROPE_THETA = 500000.0
NORM_EPS = 1e-6
NEG_INF = -1e30
BIG = 1e30

A_HEADS = 8
A_BLOCK = 256
A_TOPK = 3

B_HEADS = 8
B_KV_HEADS = 2
B_GROUP = B_HEADS // B_KV_HEADS
CMP_LEN = 32
CMP_STRIDE = 16
CMP_HIDDEN = 256
SLC_BLOCK = 64
SLC_TOPN = 16
WINDOW = 512

PEER_HEADS = 8
PEER_NKEYS = 128
PEER_N = PEER_NKEYS * PEER_NKEYS
PEER_QDIM = 256
PEER_HALF = PEER_QDIM // 2
PEER_TOPK = 16
PEER_SLOTS = PEER_HEADS * PEER_TOPK
PEER_HEAD_UNROLL = 8

A_WIDTH = A_HEADS * HEAD_DIM
B_WIDTH = B_HEADS * HEAD_DIM
B_KV_WIDTH = B_KV_HEADS * HEAD_DIM
N_BRANCH = 2

LANES = 128
SUBLANES = 8
VMEM_LIMIT_BYTES = 56 * 1024 * 1024

ROPE_COLS = 2 * A_WIDTH + B_WIDTH + 2 * B_KV_WIDTH
QK_AQ, QK_AK, QK_BQ, QK_BKS, QK_BKW = 0, 4, 8, 12, 13
PLAIN_COLS = A_WIDTH + 2 * B_KV_WIDTH
VV_AV, VV_BVS, VV_BVW = 0, 4, 5
MG_COLS = N_BRANCH * D_MODEL
B_PAIRS = B_WIDTH // LANES
B_HEAD_ORDER = tuple(h for p in range(B_PAIRS) for h in (p, B_GROUP + p))

TILE = 256
PEER_TB = 128
F32 = jnp.float32
BF16 = jnp.bfloat16
I32 = jnp.int32


def _const_spec(shape):
    n = len(shape)
    return pl.BlockSpec(shape, lambda *_: (0,) * n, pipeline_mode=pl.Buffered(1))


def _params(n_axes):
    return pltpu.CompilerParams(dimension_semantics=("arbitrary",) * n_axes,
                                vmem_limit_bytes=VMEM_LIMIT_BYTES)


def _dot(a, b):
    return jnp.dot(a, b, preferred_element_type=F32)


def _dot_nt(a, b):
    return lax.dot_general(a, b, (((1,), (1,)), ((), ())), preferred_element_type=F32)


def _split(a_f32):
    hi = a_f32.astype(BF16)
    lo = (a_f32 - hi.astype(F32)).astype(BF16)
    return hi, lo


def _split_dot(a_f32, b_bf16):
    hi, lo = _split(a_f32)
    return _dot(hi, b_bf16) + _dot(lo, b_bf16)


def _gelu_tanh(x):
    return 0.5 * x * (1.0 + jnp.tanh(math.sqrt(2.0 / math.pi) * (x + 0.044715 * (x * x * x))))


def _group_rmsnorm(y, bd, gain):
    ms = _split_dot(y * y, bd)
    return y * lax.rsqrt(ms + NORM_EPS) * gain


def _inproj_kernel(x_ref, g1_ref, w_rope_ref, w_plain_ref, w_small_ref, w_mg_ref, hg_ref,
                   cos_ref, sin_ref, bd_ref, bm_ref,
                   qk_ref, vvt_ref, kcin_ref, vcin_ref, gate_ref, mg_ref, kmean_ref):
    x = x_ref[...]
    h = x * lax.rsqrt(jnp.mean(x * x, axis=-1, keepdims=True) + NORM_EPS) * g1_ref[...]
    hb = h.astype(BF16)

    y_all = _dot(hb, w_rope_ref[...])
    vvt_ref[0] = _dot_nt(w_plain_ref[...], hb).astype(BF16)
    small = _dot(hb, w_small_ref[...])
    kcin_ref[...] = small[:, 0:LANES].astype(BF16)
    vcin_ref[...] = small[:, LANES:2 * LANES].astype(BF16)
    gate_ref[...] = jax.nn.sigmoid(small[:, 2 * LANES:3 * LANES])
    mg_ref[...] = jax.nn.sigmoid(_dot(hb, w_mg_ref[...]) + bm_ref[...]).astype(BF16)

    cos = cos_ref[...]
    sin = sin_ref[...]
    bd = bd_ref[...]
    lane = lax.broadcasted_iota(I32, (TILE, LANES), 1)
    first = (lane % HEAD_DIM) < ROT_HALF
    wide = 2 * LANES
    for t2 in range(ROPE_COLS // wide):
        y2 = y_all[:, t2 * wide:(t2 + 1) * wide]
        ms2 = _split_dot(y2 * y2, bd)
        for half in range(2):
            t = 2 * t2 + half
            sl = slice(t * LANES, (t + 1) * LANES)
            hs = slice(half * LANES, (half + 1) * LANES)
            yn = y2[:, hs] * lax.rsqrt(ms2[:, hs] + NORM_EPS) * hg_ref[:, sl]
            up = pltpu.roll(yn, LANES - ROT_HALF, axis=1)
            dn = pltpu.roll(yn, ROT_HALF, axis=1)
            out = yn * cos + jnp.where(first, -up, dn) * sin
            qk_ref[:, sl] = out.astype(BF16)
            if QK_AK <= t < QK_AK + A_WIDTH // LANES:
                c = t - QK_AK
                kmean_ref[0, :, c * LANES:(c + 1) * LANES] = jnp.mean(out, axis=0, keepdims=True)


def _inproj(x2d, g1, w_rope, w_plain, w_small, w_mg, hg, cos_t, sin_t, bd2, bm, seq):
    tokens = x2d.shape[0]
    nt = tokens // TILE
    tiles_per_seq = seq // TILE
    row = lambda i: (i, 0)
    pos = lambda i: (i % tiles_per_seq, 0)
    return pl.pallas_call(
        _inproj_kernel,
        grid=(nt,),
        in_specs=[
            pl.BlockSpec((TILE, D_MODEL), row),
            _const_spec((1, D_MODEL)),
            _const_spec((D_MODEL, ROPE_COLS)),
            _const_spec((PLAIN_COLS, D_MODEL)),
            _const_spec((D_MODEL, 3 * LANES)),
            _const_spec((D_MODEL, MG_COLS)),
            _const_spec((1, ROPE_COLS)),
            pl.BlockSpec((TILE, LANES), pos),
            pl.BlockSpec((TILE, LANES), pos),
            _const_spec((2 * LANES, 2 * LANES)),
            _const_spec((1, MG_COLS)),
        ],
        out_specs=[
            pl.BlockSpec((TILE, ROPE_COLS), row),
            pl.BlockSpec((1, PLAIN_COLS, TILE), lambda i: (i // tiles_per_seq, 0, i % tiles_per_seq)),
            pl.BlockSpec((TILE, LANES), row),
            pl.BlockSpec((TILE, LANES), row),
            pl.BlockSpec((TILE, LANES), row),
            pl.BlockSpec((TILE, MG_COLS), row),
            pl.BlockSpec((1, 1, A_WIDTH), lambda i: (i, 0, 0)),
        ],
        out_shape=[
            jax.ShapeDtypeStruct((tokens, ROPE_COLS), BF16),
            jax.ShapeDtypeStruct((tokens // seq, PLAIN_COLS, seq), BF16),
            jax.ShapeDtypeStruct((tokens, LANES), BF16),
            jax.ShapeDtypeStruct((tokens, LANES), BF16),
            jax.ShapeDtypeStruct((tokens, LANES), F32),
            jax.ShapeDtypeStruct((tokens, MG_COLS), BF16),
            jax.ShapeDtypeStruct((nt, 1, A_WIDTH), F32),
        ],
        compiler_params=_params(1),
        name="inproj",
    )(x2d, g1, w_rope, w_plain, w_small, w_mg, hg, cos_t, sin_t, bd2, bm)


LOG2E = math.log2(math.e)


def _scaled_halves(q_pair):
    lane = lax.broadcasted_iota(I32, (TILE, LANES), 1)
    q = q_pair.astype(F32)
    c = HEAD_DIM ** -0.5 * LOG2E
    plain = [jnp.where((lane // HEAD_DIM) == h, q, 0.0).astype(BF16) for h in range(2)]
    scaled = [jnp.where((lane // HEAD_DIM) == h, q * c, 0.0).astype(BF16) for h in range(2)]
    return plain, scaled


def _col_max(s_t):
    part = jnp.max(s_t.reshape(-1, SUBLANES, s_t.shape[-1]), axis=0)
    return jnp.max(part, axis=0, keepdims=True)


def _values_and_sum(v_t, p_t):
    ones = jnp.ones((2 * SUBLANES, v_t.shape[1]), BF16)
    res = _dot(jnp.concatenate([v_t, ones], axis=0), p_t.astype(BF16))
    return res[0:HEAD_DIM], res[HEAD_DIM:HEAD_DIM + SUBLANES]


def _flash_step(s_t, shift_bias, v_t, m, l8, acc_ref, rows):
    m_new = jnp.maximum(m, _col_max(s_t) + shift_bias)
    p_t = jnp.exp2(s_t - (m_new - shift_bias))
    alpha = jnp.exp2(m - m_new)
    pv, psum = _values_and_sum(v_t, p_t)
    l8 = alpha * l8 + psum
    acc_ref[rows, :] = alpha * acc_ref[rows, :] + pv
    return m_new, l8


def _flash_finish(l8s, acc_ref):
    inv = [1.0 / l8[0:1] for l8 in l8s]
    out_t = jnp.concatenate([acc_ref[0:HEAD_DIM, :] * inv[0], acc_ref[HEAD_DIM:, :] * inv[1]], axis=0)
    return jnp.transpose(out_t)


def _moba_kernel(q_ref, k_ref, vt_ref, kmh_ref, kml_ref, o_ref, acc_ref, *, topk, gate_rows):
    i = pl.program_id(1)
    pairs = A_WIDTH // LANES
    heads = [(p, h) for p in range(pairs) for h in range(2)]
    key_i = lax.broadcasted_iota(I32, (TILE, TILE), 0)
    qry_i = lax.broadcasted_iota(I32, (TILE, TILE), 1)
    causal = key_i <= qry_i
    blk = lax.broadcasted_iota(I32, (gate_rows, TILE), 0)
    blk_f = blk.astype(F32)
    own = pl.multiple_of(i * TILE, TILE)
    lanes_of = lambda p: slice(p * LANES, (p + 1) * LANES)
    rows_of = lambda p, h: slice((2 * p + h) * HEAD_DIM, (2 * p + h + 1) * HEAD_DIM)
    zero = jnp.zeros((1, TILE), F32)
    m0 = jnp.full((1, TILE), NEG_INF, F32)
    l0 = jnp.zeros((SUBLANES, TILE), F32)

    plain, scaled = [], []
    for p in range(pairs):
        plain_p, scaled_p = _scaled_halves(q_ref[0, :, lanes_of(p)])
        plain.extend(plain_p)
        scaled.extend(scaled_p)
    gates = [(_dot_nt(kmh_ref[0, :, lanes_of(p)], plain[c])
              + _dot_nt(kml_ref[0, :, lanes_of(p)], plain[c]))[0:gate_rows]
             for c, (p, h) in enumerate(heads)]
    own_scores = [_dot_nt(k_ref[0, pl.ds(own, TILE), lanes_of(p)], scaled[c])
                  for c, (p, h) in enumerate(heads)]

    biases, state = [], []
    for c, (p, h) in enumerate(heads):
        gate = jnp.where(blk < i, gates[c], NEG_INF)
        bias = jnp.full((gate_rows, TILE), NEG_INF, F32)
        for _ in range(topk):
            best = jnp.max(gate, axis=0, keepdims=True)
            idx = jnp.min(jnp.where(gate == best, blk_f, float(gate_rows)), axis=0, keepdims=True)
            pick = blk_f == idx
            bias = jnp.where(pick, jnp.where(best > 0.5 * NEG_INF, 0.0, bias), bias)
            gate = jnp.where(pick, NEG_INF, gate)
        biases.append(bias)
        acc_ref[rows_of(p, h), :] = jnp.zeros((HEAD_DIM, TILE), F32)
        state.extend(_flash_step(jnp.where(causal, own_scores[c], NEG_INF), zero,
                                 vt_ref[0, rows_of(p, h), pl.ds(own, TILE)], m0, l0,
                                 acc_ref, rows_of(p, h)))

    def past_block(n, carry):
        start = pl.multiple_of(n * TILE, TILE)
        out = []
        scores = [_dot_nt(k_ref[0, pl.ds(start, TILE), lanes_of(p)], scaled[c])
                  for c, (p, h) in enumerate(heads)]
        for c, (p, h) in enumerate(heads):
            b = jnp.max(jnp.where(blk == n, biases[c], NEG_INF), axis=0, keepdims=True)
            out.extend(_flash_step(scores[c], b, vt_ref[0, rows_of(p, h), pl.ds(start, TILE)],
                                   carry[2 * c], carry[2 * c + 1], acc_ref, rows_of(p, h)))
        return tuple(out)

    state = lax.fori_loop(0, i, past_block, tuple(state))
    for p in range(pairs):
        l8s = (state[4 * p + 1], state[4 * p + 3])
        o_ref[0, :, lanes_of(p)] = _flash_finish(l8s, acc_ref.at[lanes_of(p)]).astype(BF16)


def _moba(qk, vvt, kmean_hi, kmean_lo, batch, seq):
    n_blk = seq // A_BLOCK
    gate_rows = -(-n_blk // SUBLANES) * SUBLANES
    kernel = functools.partial(_moba_kernel, topk=min(A_TOPK, n_blk), gate_rows=gate_rows)
    return pl.pallas_call(
        kernel,
        grid=(batch, n_blk),
        in_specs=[
            pl.BlockSpec((1, TILE, A_WIDTH), lambda b, i: (b, i, QK_AQ * LANES // A_WIDTH)),
            pl.BlockSpec((1, seq, A_WIDTH), lambda b, i: (b, 0, QK_AK * LANES // A_WIDTH)),
            pl.BlockSpec((1, A_WIDTH, seq), lambda b, i: (b, VV_AV * LANES // A_WIDTH, 0)),
            pl.BlockSpec((1, LANES, A_WIDTH), lambda b, i: (b, 0, 0)),
            pl.BlockSpec((1, LANES, A_WIDTH), lambda b, i: (b, 0, 0)),
        ],
        out_specs=pl.BlockSpec((1, TILE, A_WIDTH), lambda b, i: (b, i, 0)),
        out_shape=jax.ShapeDtypeStruct((batch, seq, A_WIDTH), BF16),
        scratch_shapes=[pltpu.VMEM((A_WIDTH, TILE), F32)],
        compiler_params=_params(2),
        name="moba",
    )(qk, qk, vvt, kmean_hi, kmean_lo)


def _compress_kernel(kin_ref, vin_ref, w1_ref, pos_ref, b1_ref, w2_ref, b2_ref, bd_ref, g_ref,
                     kc_ref, vc_ref):
    n_rows = kin_ref.shape[1]
    for c, (in_ref, out_ref) in enumerate(((kin_ref, kc_ref), (vin_ref, vc_ref))):
        r = in_ref[0]
        first = _dot(r, w1_ref[c, 0])
        second = _dot(r, w1_ref[c, 1])
        const = (_dot(pos_ref[c, 0], w1_ref[c, 0]) + _dot(pos_ref[c, 1], w1_ref[c, 1]))[0:1]
        hid = first + pltpu.roll(second, n_rows - 1, axis=0) + const + b1_ref[c]
        out = _dot(_gelu_tanh(hid).astype(BF16), w2_ref[c]) + b2_ref[c]
        if c == 0:
            out = _group_rmsnorm(out, bd_ref[...], g_ref[...])
        out_ref[0] = out.astype(BF16)


def _compress(kin, vin, w1, pos, b1, w2, b2, bd, gain, batch):
    n_rows = kin.shape[1]
    blk = lambda b: (b, 0, 0)
    hid2 = B_KV_HEADS * CMP_HIDDEN
    return pl.pallas_call(
        _compress_kernel,
        grid=(batch,),
        in_specs=[
            pl.BlockSpec((1, n_rows, CMP_STRIDE * LANES), blk),
            pl.BlockSpec((1, n_rows, CMP_STRIDE * LANES), blk),
            _const_spec((2, 2, CMP_STRIDE * LANES, hid2)),
            _const_spec((2, 2, SUBLANES, CMP_STRIDE * LANES)),
            _const_spec((2, 1, hid2)),
            _const_spec((2, hid2, LANES)),
            _const_spec((2, 1, LANES)),
            _const_spec((LANES, LANES)),
            _const_spec((1, LANES)),
        ],
        out_specs=[pl.BlockSpec((1, n_rows, LANES), blk), pl.BlockSpec((1, n_rows, LANES), blk)],
        out_shape=[jax.ShapeDtypeStruct((batch, n_rows, LANES), BF16)] * 2,
        compiler_params=_params(1),
        name="nsa_compress",
    )(kin, vin, w1, pos, b1, w2, b2, bd, gain)


def _nsa_cmp_kernel(q_ref, kc_ref, vc_ref, ov_ref, ocmp_ref, selb_ref, *, n_cmp, topn):
    i = pl.program_id(1)
    scale = HEAD_DIM ** -0.5
    n_pad = kc_ref.shape[1]
    lane = lax.broadcasted_iota(I32, (TILE, LANES), 1)
    pos = i * TILE + lax.broadcasted_iota(I32, (TILE, 1), 0)
    ncol = lax.broadcasted_iota(I32, (TILE, n_pad), 1)
    visible = (ncol * CMP_STRIDE + (CMP_LEN - 1) <= pos) & (ncol < n_cmp)
    kc = kc_ref[0]
    vc = vc_ref[0]
    ov_t = ov_ref[...]
    heads = [(p, h) for p in range(B_PAIRS) for h in range(2)]
    qs = []
    for p in range(B_PAIRS):
        q_pair = q_ref[0, :, p * LANES:(p + 1) * LANES]
        qs.extend(jnp.where((lane // HEAD_DIM) == h, q_pair, jnp.zeros_like(q_pair)) for h in range(2))
    scores = [_dot_nt(q, kc) for q in qs]
    probs = []
    for s in scores:
        s = jnp.where(visible, s * scale, NEG_INF)
        e = jnp.where(visible, jnp.exp(s - jnp.max(s, axis=-1, keepdims=True)), 0.0)
        tot = jnp.sum(e, axis=-1, keepdims=True)
        probs.append(jnp.where(tot > 0.0, e / tot, 0.0))
    outs = [_dot(prob.astype(BF16), vc) for prob in probs]
    parts = []
    for prob in probs:
        hi, lo = _split(prob)
        parts.append(_dot_nt(ov_t, hi) + _dot_nt(ov_t, lo))
    for p in range(B_PAIRS):
        o_pair = jnp.where(lane < HEAD_DIM, outs[2 * p], outs[2 * p + 1])
        ocmp_ref[0, :, p * LANES:(p + 1) * LANES] = o_pair.astype(BF16)

    blk = lax.broadcasted_iota(I32, (HEAD_DIM, TILE), 0)
    own = (i * TILE + lax.broadcasted_iota(I32, (1, TILE), 1)) // SLC_BLOCK
    forced = (blk == 0) | (blk == own) | (blk == own - 1)
    for g in range(B_KV_HEADS):
        imp = functools.reduce(lambda a, b: a + b, [parts[c] for c, (p, h) in enumerate(heads) if h == g])
        imp = jnp.where(forced, BIG, imp)
        imp = jnp.where(blk <= own, imp, NEG_INF)
        groups = [imp[r:r + SUBLANES] for r in range(0, HEAD_DIM, SUBLANES)]
        ranks = [jnp.zeros((SUBLANES, TILE), F32) for _ in groups]
        sub = lax.broadcasted_iota(I32, (SUBLANES, TILE), 0)
        for k in range(HEAD_DIM):
            other = imp[k:k + 1]
            for r, grp in enumerate(groups):
                ge = jnp.where(other >= grp, 1.0, 0.0)
                gt = jnp.where(other > grp, 1.0, 0.0)
                if r * SUBLANES > k:
                    ahead = ge
                elif r * SUBLANES + SUBLANES - 1 <= k:
                    ahead = gt
                else:
                    ahead = jnp.where(sub + r * SUBLANES > k, ge, gt)
                ranks[r] = ranks[r] + ahead
        rank = jnp.concatenate(ranks, axis=0)
        keep = jnp.where(rank < float(topn), jnp.where(imp > 0.5 * NEG_INF, 0.0, NEG_INF), NEG_INF)
        selb_ref[0, g * HEAD_DIM:(g + 1) * HEAD_DIM, :] = keep


def _nsa_cmp(qk, kc, vc, ov, batch, seq):
    n_pad = kc.shape[1]
    n_cmp = (seq - CMP_LEN) // CMP_STRIDE + 1
    topn = min(SLC_TOPN, seq // SLC_BLOCK)
    kernel = functools.partial(_nsa_cmp_kernel, n_cmp=n_cmp, topn=topn)
    return pl.pallas_call(
        kernel,
        grid=(batch, seq // TILE),
        in_specs=[
            pl.BlockSpec((1, TILE, B_WIDTH), lambda b, i: (b, i, QK_BQ * LANES // B_WIDTH)),
            pl.BlockSpec((1, n_pad, LANES), lambda b, i: (b, 0, 0)),
            pl.BlockSpec((1, n_pad, LANES), lambda b, i: (b, 0, 0)),
            _const_spec((HEAD_DIM, n_pad)),
        ],
        out_specs=[
            pl.BlockSpec((1, TILE, B_WIDTH), lambda b, i: (b, i, 0)),
            pl.BlockSpec((1, LANES, TILE), lambda b, i: (b, 0, i)),
        ],
        out_shape=[
            jax.ShapeDtypeStruct((batch, seq, B_WIDTH), BF16),
            jax.ShapeDtypeStruct((batch, LANES, seq), F32),
        ],
        compiler_params=_params(2),
        name="nsa_cmp_select",
    )(qk, kc, vc, ov)


def _nsa_attn_kernel(q_ref, ks_ref, vst_ref, kw_ref, vwt_ref, selb_ref, ocmp_ref, gate_ref, egt_ref,
                     o_ref, acc_s_ref, acc_w_ref):
    i = pl.program_id(1)
    blocks_per_tile = TILE // SLC_BLOCK
    heads = [(p, h) for p in range(B_PAIRS) for h in range(2)]
    key_i = lax.broadcasted_iota(I32, (TILE, TILE), 0)
    qry_i = lax.broadcasted_iota(I32, (TILE, TILE), 1)
    causal = key_i <= qry_i
    sub = lax.broadcasted_iota(I32, (SUBLANES, TILE), 0)
    own = pl.multiple_of(i * TILE, TILE)
    lanes_of = lambda p: slice(p * LANES, (p + 1) * LANES)
    rows_of = lambda p, h: slice((2 * p + h) * HEAD_DIM, (2 * p + h + 1) * HEAD_DIM)
    kv_rows = (slice(0, HEAD_DIM), slice(HEAD_DIM, LANES))
    zero = jnp.zeros((1, TILE), F32)
    m0 = jnp.full((1, TILE), NEG_INF, F32)
    l0 = jnp.zeros((SUBLANES, TILE), F32)
    scaled = []
    for p in range(B_PAIRS):
        scaled.extend(_scaled_halves(q_ref[0, :, lanes_of(p)])[1])

    def block_biases(h, n):
        first = h * HEAD_DIM + n * blocks_per_tile
        base = pl.multiple_of((first >> 3) << 3, SUBLANES)
        rows8 = selb_ref[0, pl.ds(base, SUBLANES), :]
        off = first - base
        return [jnp.max(jnp.where(sub == off + r, rows8, NEG_INF), axis=0, keepdims=True)
                for r in range(blocks_per_tile)]

    def slc_step(p, h, n, bias, s_t, m, l8):
        parts = [s_t[r * SLC_BLOCK:(r + 1) * SLC_BLOCK] for r in range(blocks_per_tile)]
        cand = [_col_max(parts[r]) + bias[r] for r in range(blocks_per_tile)]
        m_new = functools.reduce(jnp.maximum, cand, m)
        p_t = jnp.concatenate([jnp.exp2(parts[r] - (m_new - bias[r])) for r in range(blocks_per_tile)],
                              axis=0)
        alpha = jnp.exp2(m - m_new)
        start = pl.multiple_of(n * TILE, TILE)
        v_t = vst_ref[0, kv_rows[h], pl.ds(start, TILE)]
        pv, psum = _values_and_sum(v_t, p_t)
        acc_s_ref[rows_of(p, h), :] = alpha * acc_s_ref[rows_of(p, h), :] + pv
        return m_new, alpha * l8 + psum

    def scores_of(k_ref_, start):
        k = k_ref_[0, pl.ds(start, TILE), :]
        return [_dot_nt(k, scaled[c]) for c in range(len(heads))]

    state = []
    own_bias = [block_biases(h, i) for h in range(2)]
    own_scores = scores_of(ks_ref, own)
    for c, (p, h) in enumerate(heads):
        acc_s_ref[rows_of(p, h), :] = jnp.zeros((HEAD_DIM, TILE), F32)
        state.extend(slc_step(p, h, i, own_bias[h], jnp.where(causal, own_scores[c], NEG_INF), m0, l0))

    def past_tile(n, carry):
        scores = scores_of(ks_ref, pl.multiple_of(n * TILE, TILE))
        bias = [block_biases(h, n) for h in range(2)]
        out = []
        for c, (p, h) in enumerate(heads):
            out.extend(slc_step(p, h, n, bias[h], scores[c], carry[2 * c], carry[2 * c + 1]))
        return tuple(out)

    state = lax.fori_loop(0, i, past_tile, tuple(state))

    wstate = [(m0, l0)] * len(heads)
    for back, mask in ((0, causal), (1, None), (2, key_i > qry_i)):
        start = pl.multiple_of(jnp.maximum(i - back, 0) * TILE, TILE)
        scores = scores_of(kw_ref, start)
        exists = jnp.where(i >= back, 0.0, NEG_INF) + zero
        for c, (p, h) in enumerate(heads):
            if back == 0:
                acc_w_ref[rows_of(p, h), :] = jnp.zeros((HEAD_DIM, TILE), F32)
            s_t = scores[c] if mask is None else jnp.where(mask, scores[c], NEG_INF)
            wstate[c] = _flash_step(s_t, exists, vwt_ref[0, kv_rows[h], pl.ds(start, TILE)],
                                    wstate[c][0], wstate[c][1], acc_w_ref, rows_of(p, h))
    wstate = [l8 for (_, l8) in wstate]

    for p in range(B_PAIRS):
        o_slc = _flash_finish((state[4 * p + 1], state[4 * p + 3]), acc_s_ref.at[lanes_of(p)])
        o_win = _flash_finish((wstate[2 * p], wstate[2 * p + 1]), acc_w_ref.at[lanes_of(p)])
        g = _split_dot(gate_ref[0], egt_ref[p])
        out = (g[:, 0:LANES] * ocmp_ref[0, :, lanes_of(p)].astype(F32) + g[:, LANES:2 * LANES] * o_slc
               + g[:, 2 * LANES:3 * LANES] * o_win)
        o_ref[0, :, lanes_of(p)] = out.astype(BF16)


def _nsa_attn(qk, vvt, selb, ocmp, gate, eg, batch, seq):
    assert WINDOW == 2 * TILE
    k_spec = lambda t: pl.BlockSpec((1, seq, LANES), lambda b, i: (b, 0, t))
    vt_spec = lambda t: pl.BlockSpec((1, LANES, seq), lambda b, i: (b, t, 0))
    return pl.pallas_call(
        _nsa_attn_kernel,
        grid=(batch, seq // TILE),
        in_specs=[
            pl.BlockSpec((1, TILE, B_WIDTH), lambda b, i: (b, i, QK_BQ * LANES // B_WIDTH)),
            k_spec(QK_BKS),
            vt_spec(VV_BVS),
            k_spec(QK_BKW),
            vt_spec(VV_BVW),
            pl.BlockSpec((1, LANES, TILE), lambda b, i: (b, 0, i)),
            pl.BlockSpec((1, TILE, B_WIDTH), lambda b, i: (b, i, 0)),
            pl.BlockSpec((1, TILE, LANES), lambda b, i: (b, i, 0)),
            _const_spec((B_PAIRS, LANES, 3 * LANES)),
        ],
        out_specs=pl.BlockSpec((1, TILE, B_WIDTH), lambda b, i: (b, i, 0)),
        out_shape=jax.ShapeDtypeStruct((batch, seq, B_WIDTH), BF16),
        scratch_shapes=[pltpu.VMEM((B_WIDTH, TILE), F32), pltpu.VMEM((B_WIDTH, TILE), F32)],
        compiler_params=_params(2),
        name="nsa_select_window",
    )(qk, qk, vvt, qk, vvt, selb, ocmp, gate, eg)


def _top_rows(v, row_f, count, fill):
    vals, rows = [], []
    for _ in range(count):
        best = jnp.max(v, axis=0, keepdims=True)
        idx = jnp.min(jnp.where(v == best, row_f, fill), axis=0, keepdims=True)
        vals.append(best)
        rows.append(idx)
        v = jnp.where(row_f == idx, -jnp.inf, v)
    return vals, rows


def _store_token_tiles(ref, val):
    for c in range(D_MODEL // LANES):
        ref[pl.ds(c, TILE, stride=D_MODEL // LANES), :] = val[:, c * LANES:(c + 1) * LANES]


def _merge_peer_kernel(x_ref, ya_ref, yb_ref, mg_ref, wa_ref, wb_ref, wo_ref, g2_ref, wq_ref,
                       k1h_ref, k1l_ref, k2h_ref, k2l_ref,
                       x1_ref, hn_ref, idx_ref, gt_ref, q_scr):
    ua = _dot(ya_ref[...], wa_ref[...])
    ub = _dot(yb_ref[...], wb_ref[...])
    merged = (mg_ref[:, 0:D_MODEL].astype(F32) * ua + mg_ref[:, D_MODEL:2 * D_MODEL].astype(F32) * ub)
    x1 = x_ref[...] + _dot(merged.astype(BF16), wo_ref[...])
    _store_token_tiles(x1_ref, x1)
    hn = x1 * lax.rsqrt(jnp.mean(x1 * x1, axis=-1, keepdims=True) + NORM_EPS) * g2_ref[...]
    _store_token_tiles(hn_ref, hn)
    q_scr[...] = _dot(hn.astype(BF16), wq_ref[...])

    key_row = lax.broadcasted_iota(I32, (PEER_NKEYS, TILE), 0).astype(F32)
    n_cand = 7 * SUBLANES
    cand_row = lax.broadcasted_iota(I32, (n_cand, TILE), 0).astype(F32)

    def head_scores(h):
        base = pl.multiple_of(h * PEER_QDIM, PEER_QDIM)
        q1h, q1l = _split(q_scr[:, pl.ds(base, PEER_HALF)])
        q2h, q2l = _split(q_scr[:, pl.ds(base + PEER_HALF, PEER_HALF)])
        k1h, k2h = k1h_ref[...], k2h_ref[...]
        s1 = _dot_nt(k1h, q1h) + _dot_nt(k1h, q1l) + _dot_nt(k1l_ref[...], q1h)
        s2 = _dot_nt(k2h, q2h) + _dot_nt(k2h, q2l) + _dot_nt(k2l_ref[...], q2h)
        return s1, s2

    def head_select(h, s1, s2):
        v1, i1 = _top_rows(s1, key_row, PEER_TOPK, float(PEER_NKEYS))
        v2, i2 = _top_rows(s2, key_row, PEER_TOPK, float(PEER_NKEYS))
        v1m, i1m = jnp.concatenate(v1, axis=0), jnp.concatenate(i1, axis=0)
        v2m, i2m = jnp.concatenate(v2, axis=0), jnp.concatenate(i2, axis=0)
        sub8 = lax.broadcasted_iota(I32, (SUBLANES, TILE), 0)
        lo_v, lo_i = v1m[0:SUBLANES], i1m[0:SUBLANES] * float(PEER_NKEYS)
        mid = (sub8 >= 2)

        def shifted(x, rows_up):
            return pltpu.roll(x, rows_up, axis=0)

        cand = [v1[0] + v2m,
                v1[1] + v2m[0:SUBLANES],
                v1m[SUBLANES:] + v2[0],
                jnp.where(mid, lo_v + v2[0], -jnp.inf),
                jnp.where(mid, lo_v + v2[1], -jnp.inf),
                jnp.where(sub8 < 2, -jnp.inf,
                          jnp.where(sub8 < 5, lo_v + v2[2],
                                    jnp.where(sub8 < 7, shifted(lo_v, 3) + v2[3],
                                              shifted(lo_v, 5) + v2[4])))]
        cidx = [i1[0] * float(PEER_NKEYS) + i2m,
                i1[1] * float(PEER_NKEYS) + i2m[0:SUBLANES],
                i1m[SUBLANES:] * float(PEER_NKEYS) + i2[0],
                lo_i + i2[0],
                lo_i + i2[1],
                jnp.where(sub8 < 5, lo_i + i2[2],
                          jnp.where(sub8 < 7, shifted(lo_i, 3) + i2[3], shifted(lo_i, 5) + i2[4]))]
        cand = jnp.concatenate(cand, axis=0)
        cidx = jnp.concatenate(cidx, axis=0)
        vals, experts = [], []
        v = cand
        for _ in range(PEER_TOPK):
            best = jnp.max(v, axis=0, keepdims=True)
            at = jnp.min(jnp.where(v == best, cand_row, float(n_cand)), axis=0, keepdims=True)
            hit = cand_row == at
            experts.append(jnp.max(jnp.where(hit, cidx, -1.0), axis=0, keepdims=True))
            vals.append(best)
            v = jnp.where(hit, -jnp.inf, v)
        vals = jnp.concatenate(vals, axis=0)
        e = jnp.exp(vals - vals[0:1])
        out_row = pl.multiple_of(h * PEER_TOPK, PEER_TOPK)
        gt_ref[pl.ds(out_row, PEER_TOPK), :] = e / jnp.sum(e, axis=0, keepdims=True)
        first_row = jnp.concatenate(experts, axis=0) * float(HALF_ROWS)
        idx_ref[pl.ds(out_row, PEER_TOPK), :] = first_row.astype(I32)

    def head_group(i, carry):
        hs = [i * PEER_HEAD_UNROLL + k for k in range(PEER_HEAD_UNROLL)]
        scores = [head_scores(h) for h in hs]
        for h, (s1, s2) in zip(hs, scores):
            head_select(h, s1, s2)
        return carry

    lax.fori_loop(0, PEER_HEADS // PEER_HEAD_UNROLL, head_group, 0)


def _merge_peer(x2d, ya, yb, mg, wa, wb, wo, g2, wq, k1h, k1l, k2h, k2l):
    tokens = x2d.shape[0]
    row = lambda i: (i, 0)
    return pl.pallas_call(
        _merge_peer_kernel,
        grid=(tokens // TILE,),
        in_specs=[
            pl.BlockSpec((TILE, D_MODEL), row),
            pl.BlockSpec((TILE, A_WIDTH), row),
            pl.BlockSpec((TILE, B_WIDTH), row),
            pl.BlockSpec((TILE, MG_COLS), row),
            _const_spec((A_WIDTH, D_MODEL)),
            _const_spec((B_WIDTH, D_MODEL)),
            _const_spec((D_MODEL, D_MODEL)),
            _const_spec((1, D_MODEL)),
            _const_spec((D_MODEL, PEER_HEADS * PEER_QDIM)),
            _const_spec((PEER_NKEYS, PEER_HALF)),
            _const_spec((PEER_NKEYS, PEER_HALF)),
            _const_spec((PEER_NKEYS, PEER_HALF)),
            _const_spec((PEER_NKEYS, PEER_HALF)),
        ],
        out_specs=[
            pl.BlockSpec((TILE * SUBLANES, LANES), row),
            pl.BlockSpec((TILE * SUBLANES, LANES), row),
            pl.BlockSpec((PEER_SLOTS, TILE), lambda i: (0, i)),
            pl.BlockSpec((PEER_SLOTS, TILE), lambda i: (0, i)),
        ],
        out_shape=[
            jax.ShapeDtypeStruct((tokens * SUBLANES, LANES), F32),
            jax.ShapeDtypeStruct((tokens * SUBLANES, LANES), F32),
            jax.ShapeDtypeStruct((PEER_SLOTS, tokens), I32),
            jax.ShapeDtypeStruct((PEER_SLOTS, tokens), F32),
        ],
        scratch_shapes=[pltpu.VMEM((TILE, PEER_HEADS * PEER_QDIM), F32)],
        compiler_params=_params(1),
        name="merge_peer_topk",
    )(x2d, ya, yb, mg, wa, wb, wo, g2, wq, k1h, k1l, k2h, k2l)


HALF_ROWS = SUBLANES // 2
HI_MASK = -65536
ROW_MASK = 65535
PEER_UNROLL = 16
PEER_OUT_UNROLL = 2


def _unpack(words):
    hi = pltpu.bitcast(words & HI_MASK, F32)
    lo = pltpu.bitcast(words << 16, F32)
    return hi, lo


def _expert_row(tab_ref, row):
    return tab_ref[pl.ds(pl.multiple_of(row, HALF_ROWS), HALF_ROWS), :]


def _slot_reader(head_refs, t):
    return lambda j: head_refs[j // PEER_TOPK][j % PEER_TOPK, t]


def _head_smem_specs():
    return [pl.BlockSpec((PEER_TOPK, PEER_TB), lambda i, h=h: (h, i), memory_space=pltpu.SMEM)
            for h in range(PEER_HEADS)]


def _fold_rows(v, shift):
    return v + pltpu.roll(v, shift, axis=0)


def _peer_u_kernel(*refs):
    row_refs = refs[:PEER_HEADS]
    rowv_ref, hn_ref, gt_ref, tab_ref, route_ref = refs[PEER_HEADS:]
    lane = lax.broadcasted_iota(I32, (SUBLANES, LANES), 1)
    sub = lax.broadcasted_iota(I32, (SUBLANES, LANES), 0)
    keep_hi_pair = (sub % 4) >= 2
    odd = (sub % 2) == 1
    groups = PEER_SLOTS // SUBLANES

    def one_token(t, accs):
        x = hn_ref[t]
        xa = jnp.concatenate([x[0:HALF_ROWS], x[0:HALF_ROWS]], axis=0)
        xb = jnp.concatenate([x[HALF_ROWS:], x[HALF_ROWS:]], axis=0)

        slot_row = _slot_reader(row_refs, t)

        def pair(ja, jb):
            words = jnp.concatenate([_expert_row(tab_ref, slot_row(ja)),
                                     _expert_row(tab_ref, slot_row(jb))], axis=0)
            hi, lo = _unpack(words)
            return hi * xa + lo * xb

        new = list(accs)
        for g in [2 * h + half for half in range(2) for h in range(PEER_HEADS)]:
            j = g * SUBLANES
            quads = []
            for (a, b, c, d) in ((j + 3, j + 7, j + 1, j + 5), (j + 2, j + 6, j, j + 4)):
                w_ab = _fold_rows(pair(a, b), 2)
                w_cd = _fold_rows(pair(c, d), 2)
                z = jnp.where(keep_hi_pair, w_ab, pltpu.roll(w_cd, 6, axis=0))
                quads.append(_fold_rows(z, 1))
            folded = jnp.where(odd, quads[0], pltpu.roll(quads[1], 7, axis=0))
            total = jnp.sum(folded, axis=1, keepdims=True)
            new[g] = jnp.where(lane == t, total, accs[g])
        return tuple(new)

    def step(i, accs):
        for k in range(PEER_UNROLL):
            accs = one_token(i * PEER_UNROLL + k, accs)
        return accs

    init = tuple(jnp.zeros((SUBLANES, LANES), F32) for _ in range(groups))
    accs = lax.fori_loop(0, PEER_TB // PEER_UNROLL, step, init)
    a = jnp.concatenate(accs, axis=0)
    w = (_gelu_tanh(a) * gt_ref[...]).astype(BF16).astype(F32)
    route_ref[...] = (pltpu.bitcast(w, I32) & HI_MASK) | rowv_ref[...]


def _peer_u(rows, hn3, gt, tab):
    tokens = hn3.shape[0]
    blk = lambda i: (0, i)
    return pl.pallas_call(
        _peer_u_kernel,
        grid=(tokens // PEER_TB,),
        in_specs=_head_smem_specs() + [
            pl.BlockSpec((PEER_SLOTS, PEER_TB), blk),
            pl.BlockSpec((PEER_TB, SUBLANES, LANES), lambda i: (i, 0, 0)),
            pl.BlockSpec((PEER_SLOTS, PEER_TB), blk),
            _const_spec((PEER_N * HALF_ROWS, LANES)),
        ],
        out_specs=pl.BlockSpec((PEER_SLOTS, PEER_TB), blk),
        out_shape=jax.ShapeDtypeStruct((PEER_SLOTS, tokens), I32),
        compiler_params=_params(1),
        name="peer_expert_in",
    )(*([rows] * PEER_HEADS), rows, hn3, gt, tab)


def _peer_v_kernel(*refs):
    route_refs = refs[:PEER_HEADS]
    x1_ref, tab_ref, o_ref = refs[PEER_HEADS:]

    upper = lax.broadcasted_iota(I32, (SUBLANES, LANES), 0) >= HALF_ROWS

    def one_token(t):
        acc_hi = jnp.zeros((SUBLANES, LANES), F32)
        acc_lo = jnp.zeros((SUBLANES, LANES), F32)
        slot_word = _slot_reader(route_refs, t)
        for j in [h * PEER_TOPK + k for k in range(0, PEER_TOPK, 2) for h in range(PEER_HEADS)]:
            wa, wb = slot_word(j), slot_word(j + 1)
            words = jnp.concatenate([_expert_row(tab_ref, wa & ROW_MASK),
                                     _expert_row(tab_ref, wb & ROW_MASK)], axis=0)
            hi, lo = _unpack(words)
            both = jnp.where(upper, jnp.full((SUBLANES, LANES), wb, I32),
                             jnp.full((SUBLANES, LANES), wa, I32))
            w = pltpu.bitcast(both & HI_MASK, F32)
            acc_hi = acc_hi + w * hi
            acc_lo = acc_lo + w * lo
        out = jnp.concatenate([acc_hi[0:HALF_ROWS] + acc_hi[HALF_ROWS:],
                               acc_lo[0:HALF_ROWS] + acc_lo[HALF_ROWS:]], axis=0)
        o_ref[t] = x1_ref[t] + out

    def step(i, carry):
        for k in range(PEER_OUT_UNROLL):
            one_token(i * PEER_OUT_UNROLL + k)
        return carry

    lax.fori_loop(0, PEER_TB // PEER_OUT_UNROLL, step, 0)


def _peer_v(route, x13, tab):
    tokens = x13.shape[0]
    return pl.pallas_call(
        _peer_v_kernel,
        grid=(tokens // PEER_TB,),
        in_specs=_head_smem_specs() + [
            pl.BlockSpec((PEER_TB, SUBLANES, LANES), lambda i: (i, 0, 0)),
            _const_spec((PEER_N * HALF_ROWS, LANES)),
        ],
        out_specs=pl.BlockSpec((PEER_TB, SUBLANES, LANES), lambda i: (i, 0, 0)),
        out_shape=jax.ShapeDtypeStruct((tokens, SUBLANES, LANES), F32),
        compiler_params=_params(1),
        name="peer_expert_out",
    )(*([route] * PEER_HEADS), x13, tab)


def _pack_table(t):
    bits = lax.bitcast_convert_type(t.astype(BF16), jnp.uint16).astype(jnp.uint32)
    half = D_MODEL // 2
    words = (bits[:, :half] << 16) | bits[:, half:]
    return lax.bitcast_convert_type(words, I32).reshape(t.shape[0] * HALF_ROWS, LANES)


def _rope_tables(seq):
    inv = ROPE_THETA ** (-jnp.arange(0, ROT_DIM, 2, dtype=F32) / ROT_DIM)
    ang = jnp.arange(seq, dtype=F32)[:, None] * inv[None, :]
    d = np.arange(LANES) % HEAD_DIM
    cos = jnp.where(d[None, :] < ROT_DIM, jnp.cos(ang)[:, d % ROT_HALF], 1.0)
    sin = jnp.where(d[None, :] < ROT_DIM, jnp.sin(ang)[:, d % ROT_HALF], 0.0)
    return cos.astype(F32), sin.astype(F32)


def _compress_weights(w1, pos):
    out_w, out_p = [], []
    for part in range(2):
        wpart = w1[part * CMP_STRIDE * HEAD_DIM:(part + 1) * CMP_STRIDE * HEAD_DIM]
        wpart = wpart.reshape(CMP_STRIDE, HEAD_DIM, CMP_HIDDEN)
        full = jnp.einsum("ldj,gh->lgdhj", wpart, jnp.eye(B_KV_HEADS, dtype=F32))
        out_w.append(full.reshape(CMP_STRIDE * LANES, B_KV_HEADS * CMP_HIDDEN))
        ppart = pos[part * CMP_STRIDE:(part + 1) * CMP_STRIDE]
        prow = jnp.tile(ppart[:, None, :], (1, B_KV_HEADS, 1)).reshape(1, CMP_STRIDE * LANES)
        out_p.append(jnp.tile(prow, (SUBLANES, 1)))
    return jnp.stack(out_w).astype(BF16), jnp.stack(out_p).astype(BF16)


def kernel(x, norm1_g, w_in, b_merge, a_q_g, a_k_g, b_q_g, b_kc_g, b_ks_g, b_kw_g, cmp_pos_k,
           cmp_k_w1, cmp_k_b1, cmp_k_w2, cmp_k_b2, cmp_pos_v, cmp_v_w1, cmp_v_b1, cmp_v_w2,
           cmp_v_b2, w_up_a, w_up_b, w_out, norm2_g, peer_wq, peer_k1, peer_k2, peer_u, peer_v):
    batch, seq, _ = x.shape
    assert seq % TILE == 0 and seq // SLC_BLOCK <= HEAD_DIM
    tokens = batch * seq
    l = 0
    x2d = x.reshape(tokens, D_MODEL)

    w = w_in[l]
    sizes = (A_WIDTH, A_WIDTH, A_WIDTH, B_WIDTH) + (B_KV_WIDTH,) * 6 + (3 * B_HEADS, MG_COLS)
    offs = np.concatenate([[0], np.cumsum(sizes)])
    seg = lambda k: w[:, offs[k]:offs[k + 1]]
    aq, ak, av, bq, bkc, bvc, bks, bvs, bkw, bvw, bgate, mgate = (seg(k) for k in range(12))
    bq_perm = bq.reshape(D_MODEL, B_HEADS, HEAD_DIM)[:, np.array(B_HEAD_ORDER)].reshape(D_MODEL, B_WIDTH)
    w_rope = jnp.concatenate([aq, ak, bq_perm, bks, bkw], axis=1).astype(BF16)
    w_plain = jnp.concatenate([av, bvs, bvw], axis=1).T.astype(BF16)
    w_small = jnp.concatenate([bkc, bvc, jnp.pad(bgate, ((0, 0), (0, LANES - 3 * B_HEADS)))],
                              axis=1).astype(BF16)
    hg = jnp.concatenate([jnp.tile(a_q_g[l], A_HEADS), jnp.tile(a_k_g[l], A_HEADS),
                          jnp.tile(b_q_g[l], B_HEADS), jnp.tile(b_ks_g[l], B_KV_HEADS),
                          jnp.tile(b_kw_g[l], B_KV_HEADS)])[None, :]
    cos_t, sin_t = _rope_tables(seq)
    bd = jnp.asarray(np.kron(np.eye(LANES // HEAD_DIM), np.full((HEAD_DIM, HEAD_DIM), 1.0 / HEAD_DIM)),
                     BF16)

    qk, vvt, kcin, vcin, gate, mg, kmean = _inproj(
        x2d, norm1_g[l][None, :], w_rope, w_plain, w_small, mgate.astype(BF16), hg, cos_t, sin_t,
        jnp.kron(jnp.eye(2, dtype=BF16), bd), b_merge[l].reshape(1, MG_COLS), seq)
    qk = qk.reshape(batch, seq, ROPE_COLS)

    n_blk = seq // A_BLOCK
    kmean = jnp.pad(kmean.reshape(batch, n_blk, A_WIDTH), ((0, 0), (0, LANES - n_blk), (0, 0)))
    km_hi = kmean.astype(BF16)
    km_lo = (kmean - km_hi.astype(F32)).astype(BF16)
    ya = _moba(qk, vvt, km_hi, km_lo, batch, seq)

    n_rows = seq // CMP_STRIDE
    wk1, pk = _compress_weights(cmp_k_w1[l], cmp_pos_k[l])
    wv1, pv = _compress_weights(cmp_v_w1[l], cmp_pos_v[l])
    blockdiag = lambda m: jnp.kron(jnp.eye(B_KV_HEADS, dtype=F32), m)
    w2 = jnp.stack([blockdiag(cmp_k_w2[l]), blockdiag(cmp_v_w2[l])]).astype(BF16)
    b1 = jnp.stack([jnp.tile(cmp_k_b1[l], B_KV_HEADS), jnp.tile(cmp_v_b1[l], B_KV_HEADS)])[:, None, :]
    b2 = jnp.stack([jnp.tile(cmp_k_b2[l], B_KV_HEADS), jnp.tile(cmp_v_b2[l], B_KV_HEADS)])[:, None, :]
    kc, vc = _compress(kcin.reshape(batch, n_rows, CMP_STRIDE * LANES),
                       vcin.reshape(batch, n_rows, CMP_STRIDE * LANES),
                       jnp.stack([wk1, wv1]), jnp.stack([pk, pv]), b1, w2, b2, bd,
                       jnp.tile(b_kc_g[l], B_KV_HEADS)[None, :], batch)

    n_cmp = (seq - CMP_LEN) // CMP_STRIDE + 1
    n_slc = seq // SLC_BLOCK
    ci = np.arange(n_rows)[:, None]
    sj = np.arange(HEAD_DIM)[None, :]
    ov = ((ci * CMP_STRIDE < (sj + 1) * SLC_BLOCK) & (ci * CMP_STRIDE + CMP_LEN > sj * SLC_BLOCK)
          & (ci < n_cmp) & (sj < n_slc)).astype(np.float32)
    ov = jnp.asarray(ov.T, BF16)
    ocmp, selb = _nsa_cmp(qk, kc, vc, ov, batch, seq)

    eg = np.zeros((B_PAIRS, LANES, 3 * LANES), np.float32)
    for p in range(B_PAIRS):
        for half in range(2):
            head = B_HEAD_ORDER[2 * p + half]
            for c in range(3):
                eg[p, head * 3 + c, c * LANES + half * HEAD_DIM:c * LANES + (half + 1) * HEAD_DIM] = 1.0
    yb = _nsa_attn(qk, vvt, selb, ocmp, gate.reshape(batch, seq, LANES), jnp.asarray(eg, BF16),
                   batch, seq)

    wb_perm = w_up_b[l].reshape(B_HEADS, HEAD_DIM, D_MODEL)[np.array(B_HEAD_ORDER)].reshape(
        B_WIDTH, D_MODEL)
    k1h, k1l = _split(peer_k1[l])
    k2h, k2l = _split(peer_k2[l])
    x1, hn, rows, gt = _merge_peer(
        x2d, ya.reshape(tokens, A_WIDTH), yb.reshape(tokens, B_WIDTH), mg,
        w_up_a[l].astype(BF16), wb_perm.astype(BF16), w_out[l].astype(BF16),
        norm2_g[l][None, :], peer_wq[l].astype(BF16), k1h, k1l, k2h, k2l)

    route = _peer_u(rows, hn.reshape(tokens, SUBLANES, LANES), gt, _pack_table(peer_u[l]))
    out = _peer_v(route, x1.reshape(tokens, SUBLANES, LANES), _pack_table(peer_v[l]))
    return out.reshape(batch, seq, D_MODEL)
```

```python
import functools
import math

import jax
import jax.numpy as jnp
import numpy as np
from jax import lax
from jax.experimental import pallas as pl
from jax.experimental.pallas import tpu as pltpu

D_MODEL = 1024
HEAD_DIM = 64
ROT_DIM = HEAD_DIM // 4
ROT_HALF = ROT_DIM // 2
ROPE_THETA = 500000.0
NORM_EPS = 1e-6
NEG_INF = -1e30
BIG = 1e30

A_HEADS = 8
A_BLOCK = 256
A_TOPK = 3

B_HEADS = 8
B_KV_HEADS = 2
B_GROUP = B_HEADS // B_KV_HEADS
CMP_LEN = 32
CMP_STRIDE = 16
CMP_HIDDEN = 256
SLC_BLOCK = 64
SLC_TOPN = 16
WINDOW = 512

PEER_HEADS = 8
PEER_NKEYS = 128
PEER_N = PEER_NKEYS * PEER_NKEYS
PEER_QDIM = 256
PEER_HALF = PEER_QDIM // 2
PEER_TOPK = 16
PEER_SLOTS = PEER_HEADS * PEER_TOPK
PEER_HEAD_UNROLL = 8

A_WIDTH = A_HEADS * HEAD_DIM
B_WIDTH = B_HEADS * HEAD_DIM
B_KV_WIDTH = B_KV_HEADS * HEAD_DIM
N_BRANCH = 2

LANES = 128
SUBLANES = 8
VMEM_LIMIT_BYTES = 56 * 1024 * 1024

ROPE_COLS = 2 * A_WIDTH + B_WIDTH + 2 * B_KV_WIDTH
QK_AQ, QK_AK, QK_BQ, QK_BKS, QK_BKW = 0, 4, 8, 12, 13
PLAIN_COLS = A_WIDTH + 2 * B_KV_WIDTH
VV_AV, VV_BVS, VV_BVW = 0, 4, 5
MG_COLS = N_BRANCH * D_MODEL
B_PAIRS = B_WIDTH // LANES
B_HEAD_ORDER = tuple(h for p in range(B_PAIRS) for h in (p, B_GROUP + p))

TILE = 256
PEER_TB = 128
F32 = jnp.float32
BF16 = jnp.bfloat16
I32 = jnp.int32


def _const_spec(shape):
    n = len(shape)
    return pl.BlockSpec(shape, lambda *_: (0,) * n, pipeline_mode=pl.Buffered(1))


def _params(n_axes):
    return pltpu.CompilerParams(dimension_semantics=("arbitrary",) * n_axes,
                                vmem_limit_bytes=VMEM_LIMIT_BYTES)


def _dot(a, b):
    return jnp.dot(a, b, preferred_element_type=F32)


def _dot_nt(a, b):
    return lax.dot_general(a, b, (((1,), (1,)), ((), ())), preferred_element_type=F32)


def _split(a_f32):
    hi = a_f32.astype(BF16)
    lo = (a_f32 - hi.astype(F32)).astype(BF16)
    return hi, lo


def _split_dot(a_f32, b_bf16):
    hi, lo = _split(a_f32)
    return _dot(hi, b_bf16) + _dot(lo, b_bf16)


def _gelu_tanh(x):
    return 0.5 * x * (1.0 + jnp.tanh(math.sqrt(2.0 / math.pi) * (x + 0.044715 * (x * x * x))))


def _group_rmsnorm(y, bd, gain):
    ms = _split_dot(y * y, bd)
    return y * lax.rsqrt(ms + NORM_EPS) * gain


def _inproj_kernel(x_ref, g1_ref, w_rope_ref, w_plain_ref, w_small_ref, w_mg_ref, hg_ref,
                   cos_ref, sin_ref, bd_ref, bm_ref,
                   qk_ref, vvt_ref, kcin_ref, vcin_ref, gate_ref, mg_ref, kmean_ref):
    x = x_ref[...]
    h = x * lax.rsqrt(jnp.mean(x * x, axis=-1, keepdims=True) + NORM_EPS) * g1_ref[...]
    hb = h.astype(BF16)

    y_all = _dot(hb, w_rope_ref[...])
    vvt_ref[0] = _dot_nt(w_plain_ref[...], hb).astype(BF16)
    small = _dot(hb, w_small_ref[...])
    kcin_ref[...] = small[:, 0:LANES].astype(BF16)
    vcin_ref[...] = small[:, LANES:2 * LANES].astype(BF16)
    gate_ref[...] = jax.nn.sigmoid(small[:, 2 * LANES:3 * LANES])
    mg_ref[...] = jax.nn.sigmoid(_dot(hb, w_mg_ref[...]) + bm_ref[...]).astype(BF16)

    cos = cos_ref[...]
    sin = sin_ref[...]
    bd = bd_ref[...]
    lane = lax.broadcasted_iota(I32, (TILE, LANES), 1)
    first = (lane % HEAD_DIM) < ROT_HALF
    wide = 2 * LANES
    for t2 in range(ROPE_COLS // wide):
        y2 = y_all[:, t2 * wide:(t2 + 1) * wide]
        ms2 = _split_dot(y2 * y2, bd)
        for half in range(2):
            t = 2 * t2 + half
            sl = slice(t * LANES, (t + 1) * LANES)
            hs = slice(half * LANES, (half + 1) * LANES)
            yn = y2[:, hs] * lax.rsqrt(ms2[:, hs] + NORM_EPS) * hg_ref[:, sl]
            up = pltpu.roll(yn, LANES - ROT_HALF, axis=1)
            dn = pltpu.roll(yn, ROT_HALF, axis=1)
            out = yn * cos + jnp.where(first, -up, dn) * sin
            qk_ref[:, sl] = out.astype(BF16)
            if QK_AK <= t < QK_AK + A_WIDTH // LANES:
                c = t - QK_AK
                kmean_ref[0, :, c * LANES:(c + 1) * LANES] = jnp.mean(out, axis=0, keepdims=True)


def _inproj(x2d, g1, w_rope, w_plain, w_small, w_mg, hg, cos_t, sin_t, bd2, bm, seq):
    tokens = x2d.shape[0]
    nt = tokens // TILE
    tiles_per_seq = seq // TILE
    row = lambda i: (i, 0)
    pos = lambda i: (i % tiles_per_seq, 0)
    return pl.pallas_call(
        _inproj_kernel,
        grid=(nt,),
        in_specs=[
            pl.BlockSpec((TILE, D_MODEL), row),
            _const_spec((1, D_MODEL)),
            _const_spec((D_MODEL, ROPE_COLS)),
            _const_spec((PLAIN_COLS, D_MODEL)),
            _const_spec((D_MODEL, 3 * LANES)),
            _const_spec((D_MODEL, MG_COLS)),
            _const_spec((1, ROPE_COLS)),
            pl.BlockSpec((TILE, LANES), pos),
            pl.BlockSpec((TILE, LANES), pos),
            _const_spec((2 * LANES, 2 * LANES)),
            _const_spec((1, MG_COLS)),
        ],
        out_specs=[
            pl.BlockSpec((TILE, ROPE_COLS), row),
            pl.BlockSpec((1, PLAIN_COLS, TILE), lambda i: (i // tiles_per_seq, 0, i % tiles_per_seq)),
            pl.BlockSpec((TILE, LANES), row),
            pl.BlockSpec((TILE, LANES), row),
            pl.BlockSpec((TILE, LANES), row),
            pl.BlockSpec((TILE, MG_COLS), row),
            pl.BlockSpec((1, 1, A_WIDTH), lambda i: (i, 0, 0)),
        ],
        out_shape=[
            jax.ShapeDtypeStruct((tokens, ROPE_COLS), BF16),
            jax.ShapeDtypeStruct((tokens // seq, PLAIN_COLS, seq), BF16),
            jax.ShapeDtypeStruct((tokens, LANES), BF16),
            jax.ShapeDtypeStruct((tokens, LANES), BF16),
            jax.ShapeDtypeStruct((tokens, LANES), F32),
            jax.ShapeDtypeStruct((tokens, MG_COLS), BF16),
            jax.ShapeDtypeStruct((nt, 1, A_WIDTH), F32),
        ],
        compiler_params=_params(1),
        name="inproj",
    )(x2d, g1, w_rope, w_plain, w_small, w_mg, hg, cos_t, sin_t, bd2, bm)


LOG2E = math.log2(math.e)


def _scaled_halves(q_pair):
    lane = lax.broadcasted_iota(I32, (TILE, LANES), 1)
    q = q_pair.astype(F32)
    c = HEAD_DIM ** -0.5 * LOG2E
    plain = [jnp.where((lane // HEAD_DIM) == h, q, 0.0).astype(BF16) for h in range(2)]
    scaled = [jnp.where((lane // HEAD_DIM) == h, q * c, 0.0).astype(BF16) for h in range(2)]
    return plain, scaled


def _col_max(s_t):
    part = jnp.max(s_t.reshape(-1, SUBLANES, s_t.shape[-1]), axis=0)
    return jnp.max(part, axis=0, keepdims=True)


def _values_and_sum(v_t, p_t):
    ones = jnp.ones((2 * SUBLANES, v_t.shape[1]), BF16)
    res = _dot(jnp.concatenate([v_t, ones], axis=0), p_t.astype(BF16))
    return res[0:HEAD_DIM], res[HEAD_DIM:HEAD_DIM + SUBLANES]


def _flash_step(s_t, shift_bias, v_t, m, l8, acc_ref, rows):
    m_new = jnp.maximum(m, _col_max(s_t) + shift_bias)
    p_t = jnp.exp2(s_t - (m_new - shift_bias))
    alpha = jnp.exp2(m - m_new)
    pv, psum = _values_and_sum(v_t, p_t)
    l8 = alpha * l8 + psum
    acc_ref[rows, :] = alpha * acc_ref[rows, :] + pv
    return m_new, l8


def _flash_finish(l8s, acc_ref):
    inv = [1.0 / l8[0:1] for l8 in l8s]
    out_t = jnp.concatenate([acc_ref[0:HEAD_DIM, :] * inv[0], acc_ref[HEAD_DIM:, :] * inv[1]], axis=0)
    return jnp.transpose(out_t)


def _moba_kernel(q_ref, k_ref, vt_ref, kmh_ref, kml_ref, o_ref, acc_ref, *, topk, gate_rows):
    i = pl.program_id(1)
    pairs = A_WIDTH // LANES
    heads = [(p, h) for p in range(pairs) for h in range(2)]
    key_i = lax.broadcasted_iota(I32, (TILE, TILE), 0)
    qry_i = lax.broadcasted_iota(I32, (TILE, TILE), 1)
    causal = key_i <= qry_i
    blk = lax.broadcasted_iota(I32, (gate_rows, TILE), 0)
    blk_f = blk.astype(F32)
    own = pl.multiple_of(i * TILE, TILE)
    lanes_of = lambda p: slice(p * LANES, (p + 1) * LANES)
    rows_of = lambda p, h: slice((2 * p + h) * HEAD_DIM, (2 * p + h + 1) * HEAD_DIM)
    zero = jnp.zeros((1, TILE), F32)
    m0 = jnp.full((1, TILE), NEG_INF, F32)
    l0 = jnp.zeros((SUBLANES, TILE), F32)

    plain, scaled = [], []
    for p in range(pairs):
        plain_p, scaled_p = _scaled_halves(q_ref[0, :, lanes_of(p)])
        plain.extend(plain_p)
        scaled.extend(scaled_p)
    gates = [(_dot_nt(kmh_ref[0, :, lanes_of(p)], plain[c])
              + _dot_nt(kml_ref[0, :, lanes_of(p)], plain[c]))[0:gate_rows]
             for c, (p, h) in enumerate(heads)]
    own_scores = [_dot_nt(k_ref[0, pl.ds(own, TILE), lanes_of(p)], scaled[c])
                  for c, (p, h) in enumerate(heads)]

    biases, state = [], []
    for c, (p, h) in enumerate(heads):
        gate = jnp.where(blk < i, gates[c], NEG_INF)
        bias = jnp.full((gate_rows, TILE), NEG_INF, F32)
        for _ in range(topk):
            best = jnp.max(gate, axis=0, keepdims=True)
            idx = jnp.min(jnp.where(gate == best, blk_f, float(gate_rows)), axis=0, keepdims=True)
            pick = blk_f == idx
            bias = jnp.where(pick, jnp.where(best > 0.5 * NEG_INF, 0.0, bias), bias)
            gate = jnp.where(pick, NEG_INF, gate)
        biases.append(bias)
        acc_ref[rows_of(p, h), :] = jnp.zeros((HEAD_DIM, TILE), F32)
        state.extend(_flash_step(jnp.where(causal, own_scores[c], NEG_INF), zero,
                                 vt_ref[0, rows_of(p, h), pl.ds(own, TILE)], m0, l0,
                                 acc_ref, rows_of(p, h)))

    def past_block(n, carry):
        start = pl.multiple_of(n * TILE, TILE)
        out = []
        scores = [_dot_nt(k_ref[0, pl.ds(start, TILE), lanes_of(p)], scaled[c])
                  for c, (p, h) in enumerate(heads)]
        for c, (p, h) in enumerate(heads):
            b = jnp.max(jnp.where(blk == n, biases[c], NEG_INF), axis=0, keepdims=True)
            out.extend(_flash_step(scores[c], b, vt_ref[0, rows_of(p, h), pl.ds(start, TILE)],
                                   carry[2 * c], carry[2 * c + 1], acc_ref, rows_of(p, h)))
        return tuple(out)

    state = lax.fori_loop(0, i, past_block, tuple(state))
    for p in range(pairs):
        l8s = (state[4 * p + 1], state[4 * p + 3])
        o_ref[0, :, lanes_of(p)] = _flash_finish(l8s, acc_ref.at[lanes_of(p)]).astype(BF16)


def _moba(qk, vvt, kmean_hi, kmean_lo, batch, seq):
    n_blk = seq // A_BLOCK
    gate_rows = -(-n_blk // SUBLANES) * SUBLANES
    kernel = functools.partial(_moba_kernel, topk=min(A_TOPK, n_blk), gate_rows=gate_rows)
    return pl.pallas_call(
        kernel,
        grid=(batch, n_blk),
        in_specs=[
            pl.BlockSpec((1, TILE, A_WIDTH), lambda b, i: (b, i, QK_AQ * LANES // A_WIDTH)),
            pl.BlockSpec((1, seq, A_WIDTH), lambda b, i: (b, 0, QK_AK * LANES // A_WIDTH)),
            pl.BlockSpec((1, A_WIDTH, seq), lambda b, i: (b, VV_AV * LANES // A_WIDTH, 0)),
            pl.BlockSpec((1, LANES, A_WIDTH), lambda b, i: (b, 0, 0)),
            pl.BlockSpec((1, LANES, A_WIDTH), lambda b, i: (b, 0, 0)),
        ],
        out_specs=pl.BlockSpec((1, TILE, A_WIDTH), lambda b, i: (b, i, 0)),
        out_shape=jax.ShapeDtypeStruct((batch, seq, A_WIDTH), BF16),
        scratch_shapes=[pltpu.VMEM((A_WIDTH, TILE), F32)],
        compiler_params=_params(2),
        name="moba",
    )(qk, qk, vvt, kmean_hi, kmean_lo)


def _compress_kernel(kin_ref, vin_ref, w1_ref, pos_ref, b1_ref, w2_ref, b2_ref, bd_ref, g_ref,
                     kc_ref, vc_ref):
    n_rows = kin_ref.shape[1]
    for c, (in_ref, out_ref) in enumerate(((kin_ref, kc_ref), (vin_ref, vc_ref))):
        r = in_ref[0]
        first = _dot(r, w1_ref[c, 0])
        second = _dot(r, w1_ref[c, 1])
        const = (_dot(pos_ref[c, 0], w1_ref[c, 0]) + _dot(pos_ref[c, 1], w1_ref[c, 1]))[0:1]
        hid = first + pltpu.roll(second, n_rows - 1, axis=0) + const + b1_ref[c]
        out = _dot(_gelu_tanh(hid).astype(BF16), w2_ref[c]) + b2_ref[c]
        if c == 0:
            out = _group_rmsnorm(out, bd_ref[...], g_ref[...])
        out_ref[0] = out.astype(BF16)


def _compress(kin, vin, w1, pos, b1, w2, b2, bd, gain, batch):
    n_rows = kin.shape[1]
    blk = lambda b: (b, 0, 0)
    hid2 = B_KV_HEADS * CMP_HIDDEN
    return pl.pallas_call(
        _compress_kernel,
        grid=(batch,),
        in_specs=[
            pl.BlockSpec((1, n_rows, CMP_STRIDE * LANES), blk),
            pl.BlockSpec((1, n_rows, CMP_STRIDE * LANES), blk),
            _const_spec((2, 2, CMP_STRIDE * LANES, hid2)),
            _const_spec((2, 2, SUBLANES, CMP_STRIDE * LANES)),
            _const_spec((2, 1, hid2)),
            _const_spec((2, hid2, LANES)),
            _const_spec((2, 1, LANES)),
            _const_spec((LANES, LANES)),
            _const_spec((1, LANES)),
        ],
        out_specs=[pl.BlockSpec((1, n_rows, LANES), blk), pl.BlockSpec((1, n_rows, LANES), blk)],
        out_shape=[jax.ShapeDtypeStruct((batch, n_rows, LANES), BF16)] * 2,
        compiler_params=_params(1),
        name="nsa_compress",
    )(kin, vin, w1, pos, b1, w2, b2, bd, gain)


def _nsa_cmp_kernel(q_ref, kc_ref, vc_ref, ov_ref, ocmp_ref, selb_ref, *, n_cmp, topn):
    i = pl.program_id(1)
    scale = HEAD_DIM ** -0.5
    n_pad = kc_ref.shape[1]
    lane = lax.broadcasted_iota(I32, (TILE, LANES), 1)
    pos = i * TILE + lax.broadcasted_iota(I32, (TILE, 1), 0)
    ncol = lax.broadcasted_iota(I32, (TILE, n_pad), 1)
    visible = (ncol * CMP_STRIDE + (CMP_LEN - 1) <= pos) & (ncol < n_cmp)
    kc = kc_ref[0]
    vc = vc_ref[0]
    ov_t = ov_ref[...]
    heads = [(p, h) for p in range(B_PAIRS) for h in range(2)]
    qs = []
    for p in range(B_PAIRS):
        q_pair = q_ref[0, :, p * LANES:(p + 1) * LANES]
        qs.extend(jnp.where((lane // HEAD_DIM) == h, q_pair, jnp.zeros_like(q_pair)) for h in range(2))
    scores = [_dot_nt(q, kc) for q in qs]
    probs = []
    for s in scores:
        s = jnp.where(visible, s * scale, NEG_INF)
        e = jnp.where(visible, jnp.exp(s - jnp.max(s, axis=-1, keepdims=True)), 0.0)
        tot = jnp.sum(e, axis=-1, keepdims=True)
        probs.append(jnp.where(tot > 0.0, e / tot, 0.0))
    outs = [_dot(prob.astype(BF16), vc) for prob in probs]
    parts = []
    for prob in probs:
        hi, lo = _split(prob)
        parts.append(_dot_nt(ov_t, hi) + _dot_nt(ov_t, lo))
    for p in range(B_PAIRS):
        o_pair = jnp.where(lane < HEAD_DIM, outs[2 * p], outs[2 * p + 1])
        ocmp_ref[0, :, p * LANES:(p + 1) * LANES] = o_pair.astype(BF16)

    blk = lax.broadcasted_iota(I32, (HEAD_DIM, TILE), 0)
    own = (i * TILE + lax.broadcasted_iota(I32, (1, TILE), 1)) // SLC_BLOCK
    forced = (blk == 0) | (blk == own) | (blk == own - 1)
    for g in range(B_KV_HEADS):
        imp = functools.reduce(lambda a, b: a + b, [parts[c] for c, (p, h) in enumerate(heads) if h == g])
        imp = jnp.where(forced, BIG, imp)
        imp = jnp.where(blk <= own, imp, NEG_INF)
        groups = [imp[r:r + SUBLANES] for r in range(0, HEAD_DIM, SUBLANES)]
        ranks = [jnp.zeros((SUBLANES, TILE), F32) for _ in groups]
        sub = lax.broadcasted_iota(I32, (SUBLANES, TILE), 0)
        for k in range(HEAD_DIM):
            other = imp[k:k + 1]
            for r, grp in enumerate(groups):
                ge = jnp.where(other >= grp, 1.0, 0.0)
                gt = jnp.where(other > grp, 1.0, 0.0)
                if r * SUBLANES > k:
                    ahead = ge
                elif r * SUBLANES + SUBLANES - 1 <= k:
                    ahead = gt
                else:
                    ahead = jnp.where(sub + r * SUBLANES > k, ge, gt)
                ranks[r] = ranks[r] + ahead
        rank = jnp.concatenate(ranks, axis=0)
        keep = jnp.where(rank < float(topn), jnp.where(imp > 0.5 * NEG_INF, 0.0, NEG_INF), NEG_INF)
        selb_ref[0, g * HEAD_DIM:(g + 1) * HEAD_DIM, :] = keep


def _nsa_cmp(qk, kc, vc, ov, batch, seq):
    n_pad = kc.shape[1]
    n_cmp = (seq - CMP_LEN) // CMP_STRIDE + 1
    topn = min(SLC_TOPN, seq // SLC_BLOCK)
    kernel = functools.partial(_nsa_cmp_kernel, n_cmp=n_cmp, topn=topn)
    return pl.pallas_call(
        kernel,
        grid=(batch, seq // TILE),
        in_specs=[
            pl.BlockSpec((1, TILE, B_WIDTH), lambda b, i: (b, i, QK_BQ * LANES // B_WIDTH)),
            pl.BlockSpec((1, n_pad, LANES), lambda b, i: (b, 0, 0)),
            pl.BlockSpec((1, n_pad, LANES), lambda b, i: (b, 0, 0)),
            _const_spec((HEAD_DIM, n_pad)),
        ],
        out_specs=[
            pl.BlockSpec((1, TILE, B_WIDTH), lambda b, i: (b, i, 0)),
            pl.BlockSpec((1, LANES, TILE), lambda b, i: (b, 0, i)),
        ],
        out_shape=[
            jax.ShapeDtypeStruct((batch, seq, B_WIDTH), BF16),
            jax.ShapeDtypeStruct((batch, LANES, seq), F32),
        ],
        compiler_params=_params(2),
        name="nsa_cmp_select",
    )(qk, kc, vc, ov)


def _nsa_attn_kernel(q_ref, ks_ref, vst_ref, kw_ref, vwt_ref, selb_ref, ocmp_ref, gate_ref, egt_ref,
                     o_ref, acc_s_ref, acc_w_ref):
    i = pl.program_id(1)
    blocks_per_tile = TILE // SLC_BLOCK
    heads = [(p, h) for p in range(B_PAIRS) for h in range(2)]
    key_i = lax.broadcasted_iota(I32, (TILE, TILE), 0)
    qry_i = lax.broadcasted_iota(I32, (TILE, TILE), 1)
    causal = key_i <= qry_i
    sub = lax.broadcasted_iota(I32, (SUBLANES, TILE), 0)
    own = pl.multiple_of(i * TILE, TILE)
    lanes_of = lambda p: slice(p * LANES, (p + 1) * LANES)
    rows_of = lambda p, h: slice((2 * p + h) * HEAD_DIM, (2 * p + h + 1) * HEAD_DIM)
    kv_rows = (slice(0, HEAD_DIM), slice(HEAD_DIM, LANES))
    zero = jnp.zeros((1, TILE), F32)
    m0 = jnp.full((1, TILE), NEG_INF, F32)
    l0 = jnp.zeros((SUBLANES, TILE), F32)
    scaled = []
    for p in range(B_PAIRS):
        scaled.extend(_scaled_halves(q_ref[0, :, lanes_of(p)])[1])

    def block_biases(h, n):
        first = h * HEAD_DIM + n * blocks_per_tile
        base = pl.multiple_of((first >> 3) << 3, SUBLANES)
        rows8 = selb_ref[0, pl.ds(base, SUBLANES), :]
        off = first - base
        return [jnp.max(jnp.where(sub == off + r, rows8, NEG_INF), axis=0, keepdims=True)
                for r in range(blocks_per_tile)]

    def slc_step(p, h, n, bias, s_t, m, l8):
        parts = [s_t[r * SLC_BLOCK:(r + 1) * SLC_BLOCK] for r in range(blocks_per_tile)]
        cand = [_col_max(parts[r]) + bias[r] for r in range(blocks_per_tile)]
        m_new = functools.reduce(jnp.maximum, cand, m)
        p_t = jnp.concatenate([jnp.exp2(parts[r] - (m_new - bias[r])) for r in range(blocks_per_tile)],
                              axis=0)
        alpha = jnp.exp2(m - m_new)
        start = pl.multiple_of(n * TILE, TILE)
        v_t = vst_ref[0, kv_rows[h], pl.ds(start, TILE)]
        pv, psum = _values_and_sum(v_t, p_t)
        acc_s_ref[rows_of(p, h), :] = alpha * acc_s_ref[rows_of(p, h), :] + pv
        return m_new, alpha * l8 + psum

    def scores_of(k_ref_, start):
        k = k_ref_[0, pl.ds(start, TILE), :]
        return [_dot_nt(k, scaled[c]) for c in range(len(heads))]

    state = []
    own_bias = [block_biases(h, i) for h in range(2)]
    own_scores = scores_of(ks_ref, own)
    for c, (p, h) in enumerate(heads):
        acc_s_ref[rows_of(p, h), :] = jnp.zeros((HEAD_DIM, TILE), F32)
        state.extend(slc_step(p, h, i, own_bias[h], jnp.where(causal, own_scores[c], NEG_INF), m0, l0))

    def past_tile(n, carry):
        scores = scores_of(ks_ref, pl.multiple_of(n * TILE, TILE))
        bias = [block_biases(h, n) for h in range(2)]
        out = []
        for c, (p, h) in enumerate(heads):
            out.extend(slc_step(p, h, n, bias[h], scores[c], carry[2 * c], carry[2 * c + 1]))
        return tuple(out)

    state = lax.fori_loop(0, i, past_tile, tuple(state))

    wstate = [(m0, l0)] * len(heads)
    for back, mask in ((0, causal), (1, None), (2, key_i > qry_i)):
        start = pl.multiple_of(jnp.maximum(i - back, 0) * TILE, TILE)
        scores = scores_of(kw_ref, start)
        exists = jnp.where(i >= back, 0.0, NEG_INF) + zero
        for c, (p, h) in enumerate(heads):
            if back == 0:
                acc_w_ref[rows_of(p, h), :] = jnp.zeros((HEAD_DIM, TILE), F32)
            s_t = scores[c] if mask is None else jnp.where(mask, scores[c], NEG_INF)
            wstate[c] = _flash_step(s_t, exists, vwt_ref[0, kv_rows[h], pl.ds(start, TILE)],
                                    wstate[c][0], wstate[c][1], acc_w_ref, rows_of(p, h))
    wstate = [l8 for (_, l8) in wstate]

    for p in range(B_PAIRS):
        o_slc = _flash_finish((state[4 * p + 1], state[4 * p + 3]), acc_s_ref.at[lanes_of(p)])
        o_win = _flash_finish((wstate[2 * p], wstate[2 * p + 1]), acc_w_ref.at[lanes_of(p)])
        g = _split_dot(gate_ref[0], egt_ref[p])
        out = (g[:, 0:LANES] * ocmp_ref[0, :, lanes_of(p)].astype(F32) + g[:, LANES:2 * LANES] * o_slc
               + g[:, 2 * LANES:3 * LANES] * o_win)
        o_ref[0, :, lanes_of(p)] = out.astype(BF16)


def _nsa_attn(qk, vvt, selb, ocmp, gate, eg, batch, seq):
    assert WINDOW == 2 * TILE
    k_spec = lambda t: pl.BlockSpec((1, seq, LANES), lambda b, i: (b, 0, t))
    vt_spec = lambda t: pl.BlockSpec((1, LANES, seq), lambda b, i: (b, t, 0))
    return pl.pallas_call(
        _nsa_attn_kernel,
        grid=(batch, seq // TILE),
        in_specs=[
            pl.BlockSpec((1, TILE, B_WIDTH), lambda b, i: (b, i, QK_BQ * LANES // B_WIDTH)),
            k_spec(QK_BKS),
            vt_spec(VV_BVS),
            k_spec(QK_BKW),
            vt_spec(VV_BVW),
            pl.BlockSpec((1, LANES, TILE), lambda b, i: (b, 0, i)),
            pl.BlockSpec((1, TILE, B_WIDTH), lambda b, i: (b, i, 0)),
            pl.BlockSpec((1, TILE, LANES), lambda b, i: (b, i, 0)),
            _const_spec((B_PAIRS, LANES, 3 * LANES)),
        ],
        out_specs=pl.BlockSpec((1, TILE, B_WIDTH), lambda b, i: (b, i, 0)),
        out_shape=jax.ShapeDtypeStruct((batch, seq, B_WIDTH), BF16),
        scratch_shapes=[pltpu.VMEM((B_WIDTH, TILE), F32), pltpu.VMEM((B_WIDTH, TILE), F32)],
        compiler_params=_params(2),
        name="nsa_select_window",
    )(qk, qk, vvt, qk, vvt, selb, ocmp, gate, eg)


def _top_rows(v, row_f, count, fill):
    vals, rows = [], []
    for _ in range(count):
        best = jnp.max(v, axis=0, keepdims=True)
        idx = jnp.min(jnp.where(v == best, row_f, fill), axis=0, keepdims=True)
        vals.append(best)
        rows.append(idx)
        v = jnp.where(row_f == idx, -jnp.inf, v)
    return vals, rows


def _store_token_tiles(ref, val):
    for c in range(D_MODEL // LANES):
        ref[pl.ds(c, TILE, stride=D_MODEL // LANES), :] = val[:, c * LANES:(c + 1) * LANES]


def _merge_peer_kernel(x_ref, ya_ref, yb_ref, mg_ref, wa_ref, wb_ref, wo_ref, g2_ref, wq_ref,
                       k1h_ref, k1l_ref, k2h_ref, k2l_ref,
                       x1_ref, hn_ref, idx_ref, gt_ref, q_scr):
    ua = _dot(ya_ref[...], wa_ref[...])
    ub = _dot(yb_ref[...], wb_ref[...])
    merged = (mg_ref[:, 0:D_MODEL].astype(F32) * ua + mg_ref[:, D_MODEL:2 * D_MODEL].astype(F32) * ub)
    x1 = x_ref[...] + _dot(merged.astype(BF16), wo_ref[...])
    _store_token_tiles(x1_ref, x1)
    hn = x1 * lax.rsqrt(jnp.mean(x1 * x1, axis=-1, keepdims=True) + NORM_EPS) * g2_ref[...]
    _store_token_tiles(hn_ref, hn)
    q_scr[...] = _dot(hn.astype(BF16), wq_ref[...])

    key_row = lax.broadcasted_iota(I32, (PEER_NKEYS, TILE), 0).astype(F32)
    n_cand = 7 * SUBLANES
    cand_row = lax.broadcasted_iota(I32, (n_cand, TILE), 0).astype(F32)

    def head_scores(h):
        base = pl.multiple_of(h * PEER_QDIM, PEER_QDIM)
        q1h, q1l = _split(q_scr[:, pl.ds(base, PEER_HALF)])
        q2h, q2l = _split(q_scr[:, pl.ds(base + PEER_HALF, PEER_HALF)])
        k1h, k2h = k1h_ref[...], k2h_ref[...]
        s1 = _dot_nt(k1h, q1h) + _dot_nt(k1h, q1l) + _dot_nt(k1l_ref[...], q1h)
        s2 = _dot_nt(k2h, q2h) + _dot_nt(k2h, q2l) + _dot_nt(k2l_ref[...], q2h)
        return s1, s2

    def head_select(h, s1, s2):
        v1, i1 = _top_rows(s1, key_row, PEER_TOPK, float(PEER_NKEYS))
        v2, i2 = _top_rows(s2, key_row, PEER_TOPK, float(PEER_NKEYS))
        v1m, i1m = jnp.concatenate(v1, axis=0), jnp.concatenate(i1, axis=0)
        v2m, i2m = jnp.concatenate(v2, axis=0), jnp.concatenate(i2, axis=0)
        sub8 = lax.broadcasted_iota(I32, (SUBLANES, TILE), 0)
        lo_v, lo_i = v1m[0:SUBLANES], i1m[0:SUBLANES] * float(PEER_NKEYS)
        mid = (sub8 >= 2)

        def shifted(x, rows_up):
            return pltpu.roll(x, rows_up, axis=0)

        cand = [v1[0] + v2m,
                v1[1] + v2m[0:SUBLANES],
                v1m[SUBLANES:] + v2[0],
                jnp.where(mid, lo_v + v2[0], -jnp.inf),
                jnp.where(mid, lo_v + v2[1], -jnp.inf),
                jnp.where(sub8 < 2, -jnp.inf,
                          jnp.where(sub8 < 5, lo_v + v2[2],
                                    jnp.where(sub8 < 7, shifted(lo_v, 3) + v2[3],
                                              shifted(lo_v, 5) + v2[4])))]
        cidx = [i1[0] * float(PEER_NKEYS) + i2m,
                i1[1] * float(PEER_NKEYS) + i2m[0:SUBLANES],
                i1m[SUBLANES:] * float(PEER_NKEYS) + i2[0],
                lo_i + i2[0],
                lo_i + i2[1],
                jnp.where(sub8 < 5, lo_i + i2[2],
                          jnp.where(sub8 < 7, shifted(lo_i, 3) + i2[3], shifted(lo_i, 5) + i2[4]))]
        cand = jnp.concatenate(cand, axis=0)
        cidx = jnp.concatenate(cidx, axis=0)
        vals, experts = [], []
        v = cand
        for _ in range(PEER_TOPK):
            best = jnp.max(v, axis=0, keepdims=True)
            at = jnp.min(jnp.where(v == best, cand_row, float(n_cand)), axis=0, keepdims=True)
            hit = cand_row == at
            experts.append(jnp.max(jnp.where(hit, cidx, -1.0), axis=0, keepdims=True))
            vals.append(best)
            v = jnp.where(hit, -jnp.inf, v)
        vals = jnp.concatenate(vals, axis=0)
        e = jnp.exp(vals - vals[0:1])
        out_row = pl.multiple_of(h * PEER_TOPK, PEER_TOPK)
        gt_ref[pl.ds(out_row, PEER_TOPK), :] = e / jnp.sum(e, axis=0, keepdims=True)
        first_row = jnp.concatenate(experts, axis=0) * float(HALF_ROWS)
        idx_ref[pl.ds(out_row, PEER_TOPK), :] = first_row.astype(I32)

    def head_group(i, carry):
        hs = [i * PEER_HEAD_UNROLL + k for k in range(PEER_HEAD_UNROLL)]
        scores = [head_scores(h) for h in hs]
        for h, (s1, s2) in zip(hs, scores):
            head_select(h, s1, s2)
        return carry

    lax.fori_loop(0, PEER_HEADS // PEER_HEAD_UNROLL, head_group, 0)


def _merge_peer(x2d, ya, yb, mg, wa, wb, wo, g2, wq, k1h, k1l, k2h, k2l):
    tokens = x2d.shape[0]
    row = lambda i: (i, 0)
    return pl.pallas_call(
        _merge_peer_kernel,
        grid=(tokens // TILE,),
        in_specs=[
            pl.BlockSpec((TILE, D_MODEL), row),
            pl.BlockSpec((TILE, A_WIDTH), row),
            pl.BlockSpec((TILE, B_WIDTH), row),
            pl.BlockSpec((TILE, MG_COLS), row),
            _const_spec((A_WIDTH, D_MODEL)),
            _const_spec((B_WIDTH, D_MODEL)),
            _const_spec((D_MODEL, D_MODEL)),
            _const_spec((1, D_MODEL)),
            _const_spec((D_MODEL, PEER_HEADS * PEER_QDIM)),
            _const_spec((PEER_NKEYS, PEER_HALF)),
            _const_spec((PEER_NKEYS, PEER_HALF)),
            _const_spec((PEER_NKEYS, PEER_HALF)),
            _const_spec((PEER_NKEYS, PEER_HALF)),
        ],
        out_specs=[
            pl.BlockSpec((TILE * SUBLANES, LANES), row),
            pl.BlockSpec((TILE * SUBLANES, LANES), row),
            pl.BlockSpec((PEER_SLOTS, TILE), lambda i: (0, i)),
            pl.BlockSpec((PEER_SLOTS, TILE), lambda i: (0, i)),
        ],
        out_shape=[
            jax.ShapeDtypeStruct((tokens * SUBLANES, LANES), F32),
            jax.ShapeDtypeStruct((tokens * SUBLANES, LANES), F32),
            jax.ShapeDtypeStruct((PEER_SLOTS, tokens), I32),
            jax.ShapeDtypeStruct((PEER_SLOTS, tokens), F32),
        ],
        scratch_shapes=[pltpu.VMEM((TILE, PEER_HEADS * PEER_QDIM), F32)],
        compiler_params=_params(1),
        name="merge_peer_topk",
    )(x2d, ya, yb, mg, wa, wb, wo, g2, wq, k1h, k1l, k2h, k2l)


HALF_ROWS = SUBLANES // 2
HI_MASK = -65536
ROW_MASK = 65535
PEER_UNROLL = 16
PEER_OUT_UNROLL = 2


def _unpack(words):
    hi = pltpu.bitcast(words & HI_MASK, F32)
    lo = pltpu.bitcast(words << 16, F32)
    return hi, lo


def _expert_row(tab_ref, row):
    return tab_ref[pl.ds(pl.multiple_of(row, HALF_ROWS), HALF_ROWS), :]


def _slot_reader(head_refs, t):
    return lambda j: head_refs[j // PEER_TOPK][j % PEER_TOPK, t]


def _head_smem_specs():
    return [pl.BlockSpec((PEER_TOPK, PEER_TB), lambda i, h=h: (h, i), memory_space=pltpu.SMEM)
            for h in range(PEER_HEADS)]


def _peer_u_kernel(*refs):
    row_refs = refs[:PEER_HEADS]
    rowv_ref, hn_ref, gt_ref, tab_ref, route_ref = refs[PEER_HEADS:]
    lane = lax.broadcasted_iota(I32, (SUBLANES, LANES), 1)
    sub = lax.broadcasted_iota(I32, (SUBLANES, LANES), 0)
    keep_hi_pair = (sub % 4) >= 2
    odd = (sub % 2) == 1
    groups = PEER_SLOTS // SUBLANES

    def one_token(t, accs):
        x = hn_ref[t]
        xa = jnp.concatenate([x[0:HALF_ROWS], x[0:HALF_ROWS]], axis=0)
        xb = jnp.concatenate([x[HALF_ROWS:], x[HALF_ROWS:]], axis=0)

        slot_row = _slot_reader(row_refs, t)

        def pair(ja, jb):
            words = jnp.concatenate([_expert_row(tab_ref, slot_row(ja)),
                                     _expert_row(tab_ref, slot_row(jb))], axis=0)
            hi, lo = _unpack(words)
            return hi * xa + lo * xb

        new = list(accs)
        for g in [2 * h + half for half in range(2) for h in range(PEER_HEADS)]:
            j = g * SUBLANES
            quads = []
            for (a, b, c, d) in ((j, j + 4, j + 2, j + 6), (j + 1, j + 5, j + 3, j + 7)):
                p_ab, p_cd = pair(a, b), pair(c, d)
                quads.append(jnp.where(keep_hi_pair, pltpu.roll(p_cd, 2, axis=0), p_ab)
                             + jnp.where(keep_hi_pair, p_cd, pltpu.roll(p_ab, 6, axis=0)))
            folded = (jnp.where(odd, pltpu.roll(quads[1], 1, axis=0), quads[0])
                      + jnp.where(odd, quads[1], pltpu.roll(quads[0], 7, axis=0)))
            total = jnp.sum(folded, axis=1, keepdims=True)
            new[g] = jnp.where(lane == t, total, accs[g])
        return tuple(new)

    def step(i, accs):
        for k in range(PEER_UNROLL):
            accs = one_token(i * PEER_UNROLL + k, accs)
        return accs

    init = tuple(jnp.zeros((SUBLANES, LANES), F32) for _ in range(groups))
    accs = lax.fori_loop(0, PEER_TB // PEER_UNROLL, step, init)
    a = jnp.concatenate(accs, axis=0)
    w = (_gelu_tanh(a) * gt_ref[...]).astype(BF16).astype(F32)
    route_ref[...] = (pltpu.bitcast(w, I32) & HI_MASK) | rowv_ref[...]


def _peer_u(rows, hn3, gt, tab):
    tokens = hn3.shape[0]
    blk = lambda i: (0, i)
    return pl.pallas_call(
        _peer_u_kernel,
        grid=(tokens // PEER_TB,),
        in_specs=_head_smem_specs() + [
            pl.BlockSpec((PEER_SLOTS, PEER_TB), blk),
            pl.BlockSpec((PEER_TB, SUBLANES, LANES), lambda i: (i, 0, 0)),
            pl.BlockSpec((PEER_SLOTS, PEER_TB), blk),
            _const_spec((PEER_N * HALF_ROWS, LANES)),
        ],
        out_specs=pl.BlockSpec((PEER_SLOTS, PEER_TB), blk),
        out_shape=jax.ShapeDtypeStruct((PEER_SLOTS, tokens), I32),
        compiler_params=_params(1),
        name="peer_expert_in",
    )(*([rows] * PEER_HEADS), rows, hn3, gt, tab)


def _peer_v_kernel(*refs):
    route_refs = refs[:PEER_HEADS]
    x1_ref, tab_ref, o_ref = refs[PEER_HEADS:]

    upper = lax.broadcasted_iota(I32, (SUBLANES, LANES), 0) >= HALF_ROWS

    def one_token(t):
        acc_hi = jnp.zeros((SUBLANES, LANES), F32)
        acc_lo = jnp.zeros((SUBLANES, LANES), F32)
        slot_word = _slot_reader(route_refs, t)
        for j in [h * PEER_TOPK + k for k in range(0, PEER_TOPK, 2) for h in range(PEER_HEADS)]:
            wa, wb = slot_word(j), slot_word(j + 1)
            words = jnp.concatenate([_expert_row(tab_ref, wa & ROW_MASK),
                                     _expert_row(tab_ref, wb & ROW_MASK)], axis=0)
            hi, lo = _unpack(words)
            both = jnp.where(upper, jnp.full((SUBLANES, LANES), wb, I32),
                             jnp.full((SUBLANES, LANES), wa, I32))
            w = pltpu.bitcast(both & HI_MASK, F32)
            acc_hi = acc_hi + w * hi
            acc_lo = acc_lo + w * lo
        out = jnp.concatenate([acc_hi[0:HALF_ROWS] + acc_hi[HALF_ROWS:],
                               acc_lo[0:HALF_ROWS] + acc_lo[HALF_ROWS:]], axis=0)
        o_ref[t] = x1_ref[t] + out

    def step(i, carry):
        for k in range(PEER_OUT_UNROLL):
            one_token(i * PEER_OUT_UNROLL + k)
        return carry

    lax.fori_loop(0, PEER_TB // PEER_OUT_UNROLL, step, 0)


def _peer_v(route, x13, tab):
    tokens = x13.shape[0]
    return pl.pallas_call(
        _peer_v_kernel,
        grid=(tokens // PEER_TB,),
        in_specs=_head_smem_specs() + [
            pl.BlockSpec((PEER_TB, SUBLANES, LANES), lambda i: (i, 0, 0)),
            _const_spec((PEER_N * HALF_ROWS, LANES)),
        ],
        out_specs=pl.BlockSpec((PEER_TB, SUBLANES, LANES), lambda i: (i, 0, 0)),
        out_shape=jax.ShapeDtypeStruct((tokens, SUBLANES, LANES), F32),
        compiler_params=_params(1),
        name="peer_expert_out",
    )(*([route] * PEER_HEADS), x13, tab)


def _pack_table(t):
    bits = lax.bitcast_convert_type(t.astype(BF16), jnp.uint16).astype(jnp.uint32)
    half = D_MODEL // 2
    words = (bits[:, :half] << 16) | bits[:, half:]
    return lax.bitcast_convert_type(words, I32).reshape(t.shape[0] * HALF_ROWS, LANES)


def _rope_tables(seq):
    inv = ROPE_THETA ** (-jnp.arange(0, ROT_DIM, 2, dtype=F32) / ROT_DIM)
    ang = jnp.arange(seq, dtype=F32)[:, None] * inv[None, :]
    d = np.arange(LANES) % HEAD_DIM
    cos = jnp.where(d[None, :] < ROT_DIM, jnp.cos(ang)[:, d % ROT_HALF], 1.0)
    sin = jnp.where(d[None, :] < ROT_DIM, jnp.sin(ang)[:, d % ROT_HALF], 0.0)
    return cos.astype(F32), sin.astype(F32)


def _compress_weights(w1, pos):
    out_w, out_p = [], []
    for part in range(2):
        wpart = w1[part * CMP_STRIDE * HEAD_DIM:(part + 1) * CMP_STRIDE * HEAD_DIM]
        wpart = wpart.reshape(CMP_STRIDE, HEAD_DIM, CMP_HIDDEN)
        full = jnp.einsum("ldj,gh->lgdhj", wpart, jnp.eye(B_KV_HEADS, dtype=F32))
        out_w.append(full.reshape(CMP_STRIDE * LANES, B_KV_HEADS * CMP_HIDDEN))
        ppart = pos[part * CMP_STRIDE:(part + 1) * CMP_STRIDE]
        prow = jnp.tile(ppart[:, None, :], (1, B_KV_HEADS, 1)).reshape(1, CMP_STRIDE * LANES)
        out_p.append(jnp.tile(prow, (SUBLANES, 1)))
    return jnp.stack(out_w).astype(BF16), jnp.stack(out_p).astype(BF16)


def kernel(x, norm1_g, w_in, b_merge, a_q_g, a_k_g, b_q_g, b_kc_g, b_ks_g, b_kw_g, cmp_pos_k,
           cmp_k_w1, cmp_k_b1, cmp_k_w2, cmp_k_b2, cmp_pos_v, cmp_v_w1, cmp_v_b1, cmp_v_w2,
           cmp_v_b2, w_up_a, w_up_b, w_out, norm2_g, peer_wq, peer_k1, peer_k2, peer_u, peer_v):
    batch, seq, _ = x.shape
    assert seq % TILE == 0 and seq // SLC_BLOCK <= HEAD_DIM
    tokens = batch * seq
    l = 0
    x2d = x.reshape(tokens, D_MODEL)

    w = w_in[l]
    sizes = (A_WIDTH, A_WIDTH, A_WIDTH, B_WIDTH) + (B_KV_WIDTH,) * 6 + (3 * B_HEADS, MG_COLS)
    offs = np.concatenate([[0], np.cumsum(sizes)])
    seg = lambda k: w[:, offs[k]:offs[k + 1]]
    aq, ak, av, bq, bkc, bvc, bks, bvs, bkw, bvw, bgate, mgate = (seg(k) for k in range(12))
    bq_perm = bq.reshape(D_MODEL, B_HEADS, HEAD_DIM)[:, np.array(B_HEAD_ORDER)].reshape(D_MODEL, B_WIDTH)
    w_rope = jnp.concatenate([aq, ak, bq_perm, bks, bkw], axis=1).astype(BF16)
    w_plain = jnp.concatenate([av, bvs, bvw], axis=1).T.astype(BF16)
    w_small = jnp.concatenate([bkc, bvc, jnp.pad(bgate, ((0, 0), (0, LANES - 3 * B_HEADS)))],
                              axis=1).astype(BF16)
    hg = jnp.concatenate([jnp.tile(a_q_g[l], A_HEADS), jnp.tile(a_k_g[l], A_HEADS),
                          jnp.tile(b_q_g[l], B_HEADS), jnp.tile(b_ks_g[l], B_KV_HEADS),
                          jnp.tile(b_kw_g[l], B_KV_HEADS)])[None, :]
    cos_t, sin_t = _rope_tables(seq)
    bd = jnp.asarray(np.kron(np.eye(LANES // HEAD_DIM), np.full((HEAD_DIM, HEAD_DIM), 1.0 / HEAD_DIM)),
                     BF16)

    qk, vvt, kcin, vcin, gate, mg, kmean = _inproj(
        x2d, norm1_g[l][None, :], w_rope, w_plain, w_small, mgate.astype(BF16), hg, cos_t, sin_t,
        jnp.kron(jnp.eye(2, dtype=BF16), bd), b_merge[l].reshape(1, MG_COLS), seq)
    qk = qk.reshape(batch, seq, ROPE_COLS)

    n_blk = seq // A_BLOCK
    kmean = jnp.pad(kmean.reshape(batch, n_blk, A_WIDTH), ((0, 0), (0, LANES - n_blk), (0, 0)))
    km_hi = kmean.astype(BF16)
    km_lo = (kmean - km_hi.astype(F32)).astype(BF16)
    ya = _moba(qk, vvt, km_hi, km_lo, batch, seq)

    n_rows = seq // CMP_STRIDE
    wk1, pk = _compress_weights(cmp_k_w1[l], cmp_pos_k[l])
    wv1, pv = _compress_weights(cmp_v_w1[l], cmp_pos_v[l])
    blockdiag = lambda m: jnp.kron(jnp.eye(B_KV_HEADS, dtype=F32), m)
    w2 = jnp.stack([blockdiag(cmp_k_w2[l]), blockdiag(cmp_v_w2[l])]).astype(BF16)
    b1 = jnp.stack([jnp.tile(cmp_k_b1[l], B_KV_HEADS), jnp.tile(cmp_v_b1[l], B_KV_HEADS)])[:, None, :]
    b2 = jnp.stack([jnp.tile(cmp_k_b2[l], B_KV_HEADS), jnp.tile(cmp_v_b2[l], B_KV_HEADS)])[:, None, :]
    kc, vc = _compress(kcin.reshape(batch, n_rows, CMP_STRIDE * LANES),
                       vcin.reshape(batch, n_rows, CMP_STRIDE * LANES),
                       jnp.stack([wk1, wv1]), jnp.stack([pk, pv]), b1, w2, b2, bd,
                       jnp.tile(b_kc_g[l], B_KV_HEADS)[None, :], batch)

    n_cmp = (seq - CMP_LEN) // CMP_STRIDE + 1
    n_slc = seq // SLC_BLOCK
    ci = np.arange(n_rows)[:, None]
    sj = np.arange(HEAD_DIM)[None, :]
    ov = ((ci * CMP_STRIDE < (sj + 1) * SLC_BLOCK) & (ci * CMP_STRIDE + CMP_LEN > sj * SLC_BLOCK)
          & (ci < n_cmp) & (sj < n_slc)).astype(np.float32)
    ov = jnp.asarray(ov.T, BF16)
    ocmp, selb = _nsa_cmp(qk, kc, vc, ov, batch, seq)

    eg = np.zeros((B_PAIRS, LANES, 3 * LANES), np.float32)
    for p in range(B_PAIRS):
        for half in range(2):
            head = B_HEAD_ORDER[2 * p + half]
            for c in range(3):
                eg[p, head * 3 + c, c * LANES + half * HEAD_DIM:c * LANES + (half + 1) * HEAD_DIM] = 1.0
    yb = _nsa_attn(qk, vvt, selb, ocmp, gate.reshape(batch, seq, LANES), jnp.asarray(eg, BF16),
                   batch, seq)

    wb_perm = w_up_b[l].reshape(B_HEADS, HEAD_DIM, D_MODEL)[np.array(B_HEAD_ORDER)].reshape(
        B_WIDTH, D_MODEL)
    k1h, k1l = _split(peer_k1[l])
    k2h, k2l = _split(peer_k2[l])
    x1, hn, rows, gt = _merge_peer(
        x2d, ya.reshape(tokens, A_WIDTH), yb.reshape(tokens, B_WIDTH), mg,
        w_up_a[l].astype(BF16), wb_perm.astype(BF16), w_out[l].astype(BF16),
        norm2_g[l][None, :], peer_wq[l].astype(BF16), k1h, k1l, k2h, k2l)

    route = _peer_u(rows, hn.reshape(tokens, SUBLANES, LANES), gt, _pack_table(peer_u[l]))
    out = _peer_v(route, x1.reshape(tokens, SUBLANES, LANES), _pack_table(peer_v[l]))
    return out.reshape(batch, seq, D_MODEL)
```
